```python
import math
import jax, jax.numpy as jnp
from jax import lax
import numpy as np

D_MODEL = 1024
BATCH = 8
SEQ = 4096
DEPTH = 2

CTX_LEN = 256
GRID_W = 64

A_HEADS = 4
A_DK = 128
A_DV = A_DK
CONV_K = 5
A_CHUNK = 64
B_HEADS = 4
B_DH = 64
WIN_R = 8
WIN_C = 16
C_HEADS = 4
C_DK = 32
C_DV = 64
C_RANK = 16
C_GATE_NORM = 16.0
C_CHUNK = 16
ROPE_THETA = 10000.0

MIX_WIDTH = A_HEADS * A_DV + B_HEADS * B_DH + C_HEADS * C_DV
DN_ALPHA = (2 * DEPTH) ** 0.25
DN_BETA = (8 * DEPTH) ** -0.25
LN_EPS = 1e-6
NEG_INF = -1e30

PROJ_SIZES = (
    3 * A_HEADS * A_DK,
    4 * A_HEADS,
    A_HEADS * A_DV,
    B_HEADS * B_DH,
    B_HEADS * B_DH,
    B_HEADS * B_DH,
    B_HEADS * B_DH,
    C_HEADS * C_DK,
    C_HEADS * C_DK,
    C_HEADS * C_DV,
    2 * C_RANK,
    C_HEADS * C_DV,
)
PROJ_SPLITS = tuple(int(s) for s in np.cumsum(PROJ_SIZES)[:-1])
PROJ_TOTAL = int(sum(PROJ_SIZES))

kernel_name = 'hybrid_gdn_natten_gla_diffusion_block'

F32 = jnp.float32


def layer_norm(x):
    xf = x.astype(F32)
    mu = jnp.mean(xf, -1, keepdims=True)
    var = jnp.mean(jnp.square(xf - mu), -1, keepdims=True)
    return (xf - mu) * lax.rsqrt(var + LN_EPS)


def post_norm(h, g, b, dtype):
    return (layer_norm(h) * g.astype(F32) + b.astype(F32)).astype(dtype)


def ada_modulation(cond, w_mod, b_mod):
    h = (jax.nn.silu(cond) @ w_mod + b_mod)[..., None, :]
    return jnp.split(h, 3, axis=-1)


def l2norm(x):
    return x * lax.rsqrt(jnp.sum(x * x, -1, keepdims=True) + 1e-6)


def head_rmsnorm_gate(o, w, z):
    b_, h, l, dv = o.shape
    o = jnp.transpose(o, (0, 2, 1, 3))
    o = o * lax.rsqrt(jnp.mean(o * o, -1, keepdims=True) + 1e-6) * w.astype(F32)
    return (o.reshape(b_, l, h * dv) * jax.nn.silu(z.astype(F32))).astype(z.dtype)


def short_conv(x, w):
    ch = x.shape[-1]
    return lax.conv_general_dilated(
        x, w[:, None, :].astype(x.dtype), window_strides=(1,),
        padding=[(CONV_K // 2, CONV_K // 2)],
        dimension_numbers=('NWC', 'WIO', 'NWC'), feature_group_count=ch)


def axial_rope(x):
    l = x.shape[1]
    t = jnp.arange(l)
    half = x.shape[-1] // 2
    nf = half // 2
    inv = ROPE_THETA ** (-jnp.arange(nf, dtype=F32) / nf)

    def rotate(xp, pos):
        ang = pos.astype(F32)[:, None] * inv
        cos = jnp.cos(ang)[None, :, None, :]
        sin = jnp.sin(ang)[None, :, None, :]
        x1, x2 = xp[..., :nf], xp[..., nf:]
        return jnp.concatenate([x1 * cos - x2 * sin, x1 * sin + x2 * cos], -1)

    xf = x.astype(F32)
    return jnp.concatenate([rotate(xf[..., :half], t // GRID_W),
                            rotate(xf[..., half:], t % GRID_W)], -1)


def _orient(t, reverse):
    return jnp.flip(t, axis=2) if reverse else t


def bidirectional(core, seq_c, gates_c, seq_l, gates_l, s0):
    out_c, out_l = 0.0, 0.0
    for d, rev in enumerate((False, True)):
        o_c, s_c = core(*[_orient(t, rev) for t in seq_c], *[_orient(g[d], rev) for g in gates_c], s0)
        o_l, _ = core(*[_orient(t, rev) for t in seq_l], *[_orient(g[d], rev) for g in gates_l], s_c)
        out_c = out_c + _orient(o_c, rev)
        out_l = out_l + _orient(o_l, rev)
    return out_c, out_l


def gated_delta_chunked(q, k, v, g, beta, s0):
    b_, h, l, dk = q.shape
    dv = v.shape[-1]
    c = A_CHUNK
    n = l // c
    q, k, v = (t.reshape(b_, h, n, c, -1) for t in (q, k, v))
    g = jnp.cumsum(g.reshape(b_, h, n, c), axis=-1)
    beta = beta.reshape(b_, h, n, c)
    tril = jnp.tril(jnp.ones((c, c), bool))
    strict = jnp.tril(jnp.ones((c, c), bool), -1)
    diff = g[..., :, None] - g[..., None, :]
    decay = jnp.where(tril, jnp.exp(jnp.where(tril, diff, 0.0)), 0.0)
    kb = k * beta[..., None]
    lmat = jnp.where(strict, jnp.einsum('bhnid,bhnjd->bhnij', kb, k) * decay, 0.0)
    amat = jnp.eye(c, dtype=F32) + lmat
    rhs = jnp.concatenate([v * beta[..., None], kb * jnp.exp(g)[..., None]], -1)
    sol = lax.linalg.triangular_solve(amat, rhs, left_side=True, lower=True, unit_diagonal=True)
    u, w = sol[..., :dv], sol[..., dv:]
    qk = jnp.einsum('bhnid,bhnjd->bhnij', q, k) * decay
    q_dec = q * jnp.exp(g)[..., None]
    k_dec = k * jnp.exp(g[..., -1:] - g)[..., None]
    g_last = jnp.exp(g[..., -1])

    def step(s, inp):
        qk_i, qd_i, kd_i, u_i, w_i, gl_i = inp
        v_new = u_i - jnp.einsum('bhck,bhkv->bhcv', w_i, s)
        o = jnp.einsum('bhck,bhkv->bhcv', qd_i, s) + jnp.einsum('bhij,bhjv->bhiv', qk_i, v_new)
        s = s * gl_i[..., None, None] + jnp.einsum('bhck,bhcv->bhkv', kd_i, v_new)
        return s, o

    xs = tuple(jnp.moveaxis(t, 2, 0) for t in (qk, q_dec, k_dec, u, w, g_last))
    s_fin, o = lax.scan(step, s0, xs)
    return jnp.moveaxis(o, 0, 2).reshape(b_, h, l, dv), s_fin


def gdn_heads(qkv, conv_w):
    b_, l, _ = qkv.shape
    qkv = jax.nn.silu(short_conv(qkv, conv_w)).astype(F32)
    qkv = jnp.transpose(qkv.reshape(b_, l, 3, A_HEADS, A_DK), (2, 0, 3, 1, 4))
    q = l2norm(qkv[0]) * (A_DK ** -0.5)
    k = l2norm(qkv[1])
    return q, k, qkv[2]


def gdn_gates(ab, a_log, dt_bias):
    b_, l, _ = ab.shape
    ab = ab.astype(F32).reshape(b_, l, 4, A_HEADS)
    g = -jnp.exp(a_log.astype(F32)) * jax.nn.softplus(ab[:, :, :2] + dt_bias.astype(F32))
    beta = jax.nn.sigmoid(ab[:, :, 2:])
    return jnp.transpose(g, (2, 0, 3, 1)), jnp.transpose(beta, (2, 0, 3, 1))


def gdn_branch(qkv_c, ab_c, qkv_l, ab_l, conv_w, a_log, dt_bias):
    qc, kc, vc = gdn_heads(qkv_c, conv_w)
    ql, kl, vl = gdn_heads(qkv_l, conv_w)
    gc, bc = gdn_gates(ab_c, a_log, dt_bias)
    gl, bl = gdn_gates(ab_l, a_log, dt_bias)
    s0 = jnp.zeros(qc.shape[:2] + (A_DK, A_DV), F32)
    return bidirectional(gated_delta_chunked, (qc, kc, vc), (gc, bc), (ql, kl, vl), (gl, bl), s0)


def na_branch(q_c, k_c, v_c, q_l, k_l, v_l, rpb, ctx_out):
    b_, s, _ = q_l.shape
    lc = q_c.shape[1]
    rows = s // GRID_W
    wr = min(WIN_R, rows)
    scale = B_DH ** -0.5
    qc = q_c.reshape(b_, lc, B_HEADS, B_DH) * scale
    kc = k_c.reshape(b_, lc, B_HEADS, B_DH)
    vc = v_c.reshape(b_, lc, B_HEADS, B_DH)
    o_c = None
    if ctx_out:
        p_cc = jax.nn.softmax(jnp.einsum('bqhd,bkhd->bhqk', qc, kc).astype(F32), axis=-1)
        o_c = jnp.einsum('bhqk,bkhd->bqhd', p_cc.astype(vc.dtype), vc).reshape(b_, lc, B_HEADS * B_DH)
    qg = q_l.reshape(b_, rows, GRID_W, B_HEADS, B_DH) * scale
    kg = k_l.reshape(b_, rows, GRID_W, B_HEADS, B_DH)
    vg = v_l.reshape(b_, rows, GRID_W, B_HEADS, B_DH)
    r = jnp.arange(rows)
    r0 = jnp.clip(r - wr // 2, 0, rows - wr)
    row_idx = r0[:, None] + jnp.arange(wr)
    k_rows = kg[:, row_idx]
    v_rows = vg[:, row_idx]
    s_win = jnp.einsum('brqhd,brikhd->bhrqik', qg, k_rows).astype(F32)
    cq = jnp.arange(GRID_W)
    c0 = jnp.clip(cq - WIN_C // 2, 0, GRID_W - WIN_C)
    col_ok = (cq[None, :] >= c0[:, None]) & (cq[None, :] < c0[:, None] + WIN_C)
    di = row_idx - r[:, None] + (WIN_R - 1)
    dj = jnp.clip(cq[None, :] - cq[:, None] + (WIN_C - 1), 0, 2 * WIN_C - 2)
    bias = rpb.astype(F32)[:, di[:, None, :, None], dj[None, :, None, :]]
    s_win = jnp.where(col_ok[:, None, :], s_win + bias, NEG_INF)
    s_lc = jnp.einsum('brqhd,bkhd->bhrqk', qg, kc).astype(F32)
    n_win = wr * GRID_W
    p = jax.nn.softmax(jnp.concatenate([s_win.reshape(b_, B_HEADS, rows, GRID_W, n_win), s_lc], -1), axis=-1)
    p = p.astype(v_l.dtype)
    p_win = p[..., :n_win].reshape(b_, B_HEADS, rows, GRID_W, wr, GRID_W)
    o_l = (jnp.einsum('bhrqik,brikhd->brqhd', p_win, v_rows)
           + jnp.einsum('bhrqk,bkhd->brqhd', p[..., n_win:], vc))
    return o_c, o_l.reshape(b_, s, B_HEADS * B_DH)


def gla_chunked(q, k, v, gk, s0):
    b_, h, l, dk = q.shape
    dv = v.shape[-1]
    c = C_CHUNK
    n = l // c
    q, k, v, gk = (t.reshape(b_, h, n, c, -1) for t in (q, k, v, gk))
    bcum = jnp.cumsum(gk, axis=3)
    tril = jnp.tril(jnp.ones((c, c), bool))[:, :, None]
    diff = bcum[..., :, None, :] - bcum[..., None, :, :]
    decay = jnp.where(tril, jnp.exp(jnp.where(tril, diff, 0.0)), 0.0)
    amat = jnp.einsum('bhnid,bhnjd,bhnijd->bhnij', q, k, decay)
    o_intra = jnp.einsum('bhnij,bhnjv->bhniv', amat, v)
    q_dec = q * jnp.exp(bcum)
    k_dec = k * jnp.exp(bcum[..., -1:, :] - bcum)
    g_last = jnp.exp(bcum[..., -1, :])

    def step(s, inp):
        qd_i, kd_i, v_i, gl_i = inp
        o = jnp.einsum('bhck,bhkv->bhcv', qd_i, s)
        s = s * gl_i[..., None] + jnp.einsum('bhck,bhcv->bhkv', kd_i, v_i)
        return s, o

    xs = tuple(jnp.moveaxis(t, 2, 0) for t in (q_dec, k_dec, v, g_last))
    s_fin, o_inter = lax.scan(step, s0, xs)
    o = o_intra + jnp.moveaxis(o_inter, 0, 2)
    return o.reshape(b_, h, l, dv), s_fin


def gla_gates(r, w2, b2):
    b_, l, _ = r.shape
    r = r.astype(F32).reshape(b_, l, 2, C_RANK)
    gk = jax.nn.log_sigmoid(jnp.einsum('bldr,drk->dblk', r, w2.astype(F32))
                            + b2.astype(F32)[:, None, None, :]) / C_GATE_NORM
    return jnp.transpose(gk.reshape(2, b_, l, C_HEADS, C_DK), (0, 1, 3, 2, 4))


def gla_heads(t, d, rope):
    t = t.reshape(t.shape[0], t.shape[1], C_HEADS, d)
    t = axial_rope(t) if rope else t.astype(F32)
    return jnp.transpose(t, (0, 2, 1, 3))


def gla_branch(q_c, k_c, v_c, r_c, q_l, k_l, v_l, r_l, w2, b2):
    scale = C_DK ** -0.5
    qc = gla_heads(q_c, C_DK, False) * scale
    kc = gla_heads(k_c, C_DK, False)
    vc = gla_heads(v_c, C_DV, False)
    ql = gla_heads(q_l, C_DK, True) * scale
    kl = gla_heads(k_l, C_DK, True)
    vl = gla_heads(v_l, C_DV, False)
    s0 = jnp.zeros(qc.shape[:2] + (C_DK, C_DV), F32)
    return bidirectional(gla_chunked, (qc, kc, vc), (gla_gates(r_c, w2, b2),),
                         (ql, kl, vl), (gla_gates(r_l, w2, b2),), s0)


def setup_inputs(seed: int = 0) -> dict:
    key = jax.random.key(seed)
    ks = jax.random.split(key, 18)

    def nrm(k, shape, s):
        return jax.random.normal(k, shape, F32) * s

    x = nrm(ks[0], (BATCH, SEQ, D_MODEL), 1.0)
    c = nrm(ks[1], (BATCH, D_MODEL), 1.0)
    ctx = nrm(ks[2], (BATCH, CTX_LEN, D_MODEL), 1.0)
    c_ctx = nrm(ks[3], (D_MODEL,), 1.0)
    w_mod = nrm(ks[4], (DEPTH, D_MODEL, 3 * D_MODEL), 0.25 * D_MODEL ** -0.5)
    b_mod = nrm(ks[5], (DEPTH, 3 * D_MODEL), 0.01)
    w_in = nrm(ks[6], (DEPTH, D_MODEL, PROJ_TOTAL), D_MODEL ** -0.5)
    conv_w = nrm(ks[7], (DEPTH, CONV_K, 3 * A_HEADS * A_DK), CONV_K ** -0.5)
    a_log = jnp.log(jax.random.uniform(ks[8], (DEPTH, 2, A_HEADS), F32, 1.0, 16.0))
    dt = jnp.exp(jax.random.uniform(ks[9], (DEPTH, 2, A_HEADS), F32, math.log(1e-3), math.log(1e-1)))
    dt_bias = dt + jnp.log(-jnp.expm1(-dt))
    gdn_norm = 1.0 + nrm(ks[10], (DEPTH, A_DV), 0.02)
    rpb = nrm(ks[11], (DEPTH, B_HEADS, 2 * WIN_R - 1, 2 * WIN_C - 1), 0.1)
    gla_w2 = nrm(ks[12], (DEPTH, 2, C_RANK, C_HEADS * C_DK), C_RANK ** -0.5)
    gla_b2 = nrm(ks[13], (DEPTH, 2, C_HEADS * C_DK), 0.1)
    gla_norm = 1.0 + nrm(ks[14], (DEPTH, C_DV), 0.02)
    w_out = nrm(ks[15], (DEPTH, MIX_WIDTH, D_MODEL), DN_BETA * MIX_WIDTH ** -0.5)
    ln_g = 1.0 + nrm(ks[16], (DEPTH, D_MODEL), 0.02)
    ln_b = nrm(ks[17], (DEPTH, D_MODEL), 0.01)
    return {'x': x, 'c': c, 'ctx': ctx, 'c_ctx': c_ctx, 'w_mod': w_mod, 'b_mod': b_mod,
            'w_in': w_in, 'conv_w': conv_w, 'a_log': a_log, 'dt_bias': dt_bias,
            'gdn_norm': gdn_norm, 'rpb': rpb, 'gla_w2': gla_w2, 'gla_b2': gla_b2,
            'gla_norm': gla_norm, 'w_out': w_out, 'ln_g': ln_g, 'ln_b': ln_b}


def reference(x, c, ctx, c_ctx, w_mod, b_mod, w_in, conv_w, a_log, dt_bias, gdn_norm, rpb,
              gla_w2, gla_b2, gla_norm, w_out, ln_g, ln_b):
    xl, xc = x, ctx
    for i in range(DEPTH):
        ctx_out = i < DEPTH - 1
        sh_l, sc_l, gt_l = ada_modulation(c, w_mod[i], b_mod[i])
        sh_c, sc_c, gt_c = ada_modulation(c_ctx, w_mod[i], b_mod[i])
        ml = (layer_norm(xl) * (1.0 + sc_l) + sh_l).astype(xl.dtype)
        mc = (layer_norm(xc) * (1.0 + sc_c) + sh_c).astype(xc.dtype)
        (a_qkv_l, a_ab_l, a_z_l, b_q_l, b_k_l, b_v_l, b_z_l,
         c_q_l, c_k_l, c_v_l, c_r_l, c_z_l) = jnp.split(ml @ w_in[i], PROJ_SPLITS, axis=-1)
        (a_qkv_c, a_ab_c, a_z_c, b_q_c, b_k_c, b_v_c, b_z_c,
         c_q_c, c_k_c, c_v_c, c_r_c, c_z_c) = jnp.split(mc @ w_in[i], PROJ_SPLITS, axis=-1)
        oa_c, oa_l = gdn_branch(a_qkv_c, a_ab_c, a_qkv_l, a_ab_l, conv_w[i], a_log[i], dt_bias[i])
        ob_c, ob_l = na_branch(b_q_c, b_k_c, b_v_c, b_q_l, b_k_l, b_v_l, rpb[i], ctx_out)
        oc_c, oc_l = gla_branch(c_q_c, c_k_c, c_v_c, c_r_c, c_q_l, c_k_l, c_v_l, c_r_l,
                                gla_w2[i], gla_b2[i])
        y_l = jnp.concatenate([head_rmsnorm_gate(oa_l, gdn_norm[i], a_z_l),
                               ob_l * jax.nn.silu(b_z_l),
                               head_rmsnorm_gate(oc_l, gla_norm[i], c_z_l)], -1) @ w_out[i]
        if ctx_out:
            y_c = jnp.concatenate([head_rmsnorm_gate(oa_c, gdn_norm[i], a_z_c),
                                   ob_c * jax.nn.silu(b_z_c),
                                   head_rmsnorm_gate(oc_c, gla_norm[i], c_z_c)], -1) @ w_out[i]
            xc = post_norm(DN_ALPHA * xc + gt_c * y_c, ln_g[i], ln_b[i], xc.dtype)
        xl = post_norm(DN_ALPHA * xl + gt_l * y_l, ln_g[i], ln_b[i], xl.dtype)
    return xl
```

```python
import functools
import math

import numpy as np
import jax
import jax.numpy as jnp
from jax import lax
from jax.experimental import pallas as pl
from jax.experimental.pallas import tpu as pltpu

F32 = jnp.float32
BF16 = jnp.bfloat16
HI = lax.Precision.HIGHEST

D_MODEL = 1024
GRID_W = 64
A_HEADS, A_DK, A_DV, CONV_K, A_CHUNK = 4, 128, 128, 5, 64
B_HEADS, B_DH, WIN_R, WIN_C = 4, 64, 8, 16
C_HEADS, C_DK, C_DV, C_RANK, C_CHUNK = 4, 32, 64, 16, 16
C_GATE_NORM = 16.0
ROPE_THETA = 10000.0
LN_EPS = 1e-6
NEG_INF = -1e30

A_W = A_HEADS * A_DK
B_W = B_HEADS * B_DH
C_QW = C_HEADS * C_DK
C_VW = C_HEADS * C_DV
MAIN_W = 3 * A_W + A_W + 4 * B_W + 2 * C_QW + 2 * C_VW
SMALL_W = 128
PROJ_W = MAIN_W + SMALL_W
LANES = 128
VMEM_LIMIT = 56 * 1024 * 1024


def _nt(a, b, precision=None):
    return lax.dot_general(a, b, (((1,), (1,)), ((), ())), precision=precision,
                           preferred_element_type=F32)


def _tn(a, b):
    return lax.dot_general(a, b, (((0,), (0,)), ((), ())), preferred_element_type=F32)


def _mm(a, b, precision=None):
    return jnp.dot(a, b, precision=precision, preferred_element_type=F32)


def _sigmoid(x):
    return 1.0 / (1.0 + jnp.exp(-x))


def _silu(x):
    return x * _sigmoid(x)


def _softplus(x):
    return jnp.maximum(x, 0.0) + jnp.log1p(jnp.exp(-jnp.abs(x)))


def _iota2(shape, axis):
    return lax.broadcasted_iota(jnp.int32, shape, axis)


def _mod_kernel(c_ref, w_ref, b_ref, o_ref):
    o_ref[...] = _mm(_silu(c_ref[...]), w_ref[...], HI) + b_ref[...]


def _modulation(c_all, w_mod, b_mod):
    n = c_all.shape[0]
    tn = 768
    return pl.pallas_call(
        _mod_kernel,
        grid=(3 * D_MODEL // tn,),
        in_specs=[pl.BlockSpec((n, D_MODEL), lambda j: (0, 0)),
                  pl.BlockSpec((D_MODEL, tn), lambda j: (0, j)),
                  pl.BlockSpec((1, tn), lambda j: (0, j))],
        out_specs=pl.BlockSpec((n, tn), lambda j: (0, j)),
        out_shape=jax.ShapeDtypeStruct((n, 3 * D_MODEL), F32),
        name="adaln_mod",
    )(c_all, w_mod, b_mod.reshape(1, -1))


def _layer_norm(x):
    mu = jnp.mean(x, -1, keepdims=True)
    xc = x - mu
    var = jnp.mean(xc * xc, -1, keepdims=True)
    return xc * lax.rsqrt(var + LN_EPS)


def _proj_kernel(x_ref, sh_ref, sc_ref, w_ref, o_ref, s_ref):
    m = (_layer_norm(x_ref[0]) * (1.0 + sc_ref[0]) + sh_ref[0]).astype(BF16)
    step = 640
    for c0 in range(0, MAIN_W, step):
        o_ref[0, :, c0:c0 + step] = _mm(m, w_ref[:, c0:c0 + step]).astype(BF16)
    s_ref[0] = _mm(m, w_ref[:, MAIN_W:])


def _project(x, sh, sc, w_perm, per_batch):
    b_, l, _ = x.shape
    t = min(512, l)
    mod_map = (lambda b, i: (b, 0, 0)) if per_batch else (lambda b, i: (0, 0, 0))
    return pl.pallas_call(
        _proj_kernel,
        grid=(b_, l // t),
        in_specs=[pl.BlockSpec((1, t, D_MODEL), lambda b, i: (b, i, 0)),
                  pl.BlockSpec((1, 1, D_MODEL), mod_map),
                  pl.BlockSpec((1, 1, D_MODEL), mod_map),
                  pl.BlockSpec((D_MODEL, PROJ_W), lambda b, i: (0, 0))],
        out_specs=[pl.BlockSpec((1, t, MAIN_W), lambda b, i: (b, i, 0)),
                   pl.BlockSpec((1, t, SMALL_W), lambda b, i: (b, i, 0))],
        out_shape=[jax.ShapeDtypeStruct((b_, l, MAIN_W), BF16),
                   jax.ShapeDtypeStruct((b_, l, SMALL_W), F32)],
        compiler_params=pltpu.CompilerParams(vmem_limit_bytes=VMEM_LIMIT),
        name="ln_mod_proj",
    )(x, sh, sc, w_perm)


def _out_kernel(alpha, x_ref, ya_ref, yb_ref, yc_ref, w_ref, gt_ref, g_ref, b_ref, o_ref):
    y = jnp.concatenate([ya_ref[0], yb_ref[0], yc_ref[0]], axis=-1)
    h = alpha * x_ref[0] + gt_ref[0] * _mm(y, w_ref[...])
    o_ref[0] = _layer_norm(h) * g_ref[...] + b_ref[...]


def _out_project(x, ya, yb, yc, w_out, gt, ln_g, ln_b, per_batch, alpha):
    b_, l, _ = x.shape
    t = min(512, l)
    mod_map = (lambda b, i: (b, 0, 0)) if per_batch else (lambda b, i: (0, 0, 0))
    tok = lambda w: pl.BlockSpec((1, t, w), lambda b, i: (b, i, 0))
    return pl.pallas_call(
        functools.partial(_out_kernel, alpha),
        grid=(b_, l // t),
        in_specs=[tok(D_MODEL), tok(A_W), tok(B_W), tok(C_VW),
                  pl.BlockSpec((D_MODEL, D_MODEL), lambda b, i: (0, 0)),
                  pl.BlockSpec((1, 1, D_MODEL), mod_map),
                  pl.BlockSpec((1, D_MODEL), lambda b, i: (0, 0)),
                  pl.BlockSpec((1, D_MODEL), lambda b, i: (0, 0))],
        out_specs=tok(D_MODEL),
        out_shape=jax.ShapeDtypeStruct((b_, l, D_MODEL), F32),
        compiler_params=pltpu.CompilerParams(vmem_limit_bytes=VMEM_LIMIT),
        name="out_proj_postnorm",
    )(x, ya, yb, yc, w_out, gt, ln_g.reshape(1, -1), ln_b.reshape(1, -1))


NA_ROWS_PER_STEP = 4


def _stack_heads(q):
    head = _iota2(q.shape, 1) // B_DH
    return jnp.concatenate([jnp.where(head == h, q, jnp.zeros_like(q)) for h in range(B_HEADS)], axis=0)


def _unstack_heads(o, n):
    head = _iota2((n, B_W), 1) // B_DH
    out = jnp.zeros((n, B_W), F32)
    for h in range(B_HEADS):
        out = jnp.where(head == h, o[h * n:(h + 1) * n], out)
    return out


def _na_kernel(rows, q_ref, k_ref, v_ref, kc_ref, vc_ref, z_ref, bias_ref, o_ref):
    kc = kc_ref[0]
    vc = vc_ref[0]
    n_win = WIN_R * GRID_W
    for rr in range(NA_ROWS_PER_STEP):
        r = pl.program_id(1) * NA_ROWS_PER_STEP + rr
        r0 = jnp.clip(r - WIN_R // 2, 0, rows - WIN_R)
        start = pl.multiple_of(r0 * GRID_W, GRID_W)
        kw = k_ref[0, pl.ds(start, n_win), :]
        vw = v_ref[0, pl.ds(start, n_win), :]
        tok = slice(rr * GRID_W, (rr + 1) * GRID_W)
        qs = _stack_heads(q_ref[0, tok, :] * (B_DH ** -0.5))
        s_win = _nt(qs, kw) + bias_ref[r - r0]
        s_ctx = _nt(qs, kc)
        m = jnp.maximum(jnp.max(s_win, -1, keepdims=True), jnp.max(s_ctx, -1, keepdims=True))
        p_win = jnp.exp(s_win - m)
        p_ctx = jnp.exp(s_ctx - m)
        den = jnp.sum(p_win, -1, keepdims=True) + jnp.sum(p_ctx, -1, keepdims=True)
        o = (_mm(p_win.astype(BF16), vw) + _mm(p_ctx.astype(BF16), vc)) / den
        o = _unstack_heads(o, GRID_W)
        o_ref[0, tok, :] = (o * _silu(z_ref[0, tok, :].astype(F32))).astype(BF16)


def _na_bias_table(rpb):
    cq = np.arange(GRID_W)
    c0 = np.clip(cq - WIN_C // 2, 0, GRID_W - WIN_C)
    col_ok = (cq[None, :] >= c0[:, None]) & (cq[None, :] < c0[:, None] + WIN_C)
    dj = np.clip(cq[None, :] - cq[:, None] + (WIN_C - 1), 0, 2 * WIN_C - 2)
    di = np.arange(WIN_R)[None, :] + (WIN_R - 1) - np.arange(WIN_R)[:, None]
    t = rpb.astype(F32)[:, di[:, None, :, None], dj[None, :, None, :]]
    t = jnp.where(col_ok[None, None, :, None, :], t, NEG_INF)
    t = jnp.transpose(t, (1, 0, 2, 3, 4))
    return t.reshape(WIN_R, B_HEADS * GRID_W, WIN_R * GRID_W)


def _na_latent(main_l, main_c, bias):
    b_, s, _ = main_l.shape
    lc = main_c.shape[1]
    rows = s // GRID_W
    assert rows >= WIN_R and rows % NA_ROWS_PER_STEP == 0
    tq = NA_ROWS_PER_STEP * GRID_W
    col = lambda idx: (lambda b, i: (b, 0, idx))
    return pl.pallas_call(
        functools.partial(_na_kernel, rows),
        grid=(b_, rows // NA_ROWS_PER_STEP),
        in_specs=[pl.BlockSpec((1, tq, B_W), lambda b, i: (b, i, 8)),
                  pl.BlockSpec((1, s, B_W), col(9)),
                  pl.BlockSpec((1, s, B_W), col(10)),
                  pl.BlockSpec((1, lc, B_W), col(9)),
                  pl.BlockSpec((1, lc, B_W), col(10)),
                  pl.BlockSpec((1, tq, B_W), lambda b, i: (b, i, 11)),
                  pl.BlockSpec(bias.shape, lambda b, i: (0, 0, 0))],
        out_specs=pl.BlockSpec((1, tq, B_W), lambda b, i: (b, i, 0)),
        out_shape=jax.ShapeDtypeStruct((b_, s, B_W), BF16),
        compiler_params=pltpu.CompilerParams(vmem_limit_bytes=VMEM_LIMIT),
        name="na_latent",
    )(main_l, main_l, main_l, main_c, main_c, main_l, bias)


def _na_ctx_kernel(q_ref, k_ref, v_ref, z_ref, o_ref):
    lc = q_ref.shape[1]
    qs = _stack_heads(q_ref[0] * (B_DH ** -0.5))
    s = _nt(qs, k_ref[0])
    p = jnp.exp(s - jnp.max(s, -1, keepdims=True))
    o = _mm(p.astype(BF16), v_ref[0]) / jnp.sum(p, -1, keepdims=True)
    o = _unstack_heads(o, lc)
    o_ref[0] = (o * _silu(z_ref[0].astype(F32))).astype(BF16)


def _na_ctx(main_c):
    b_, lc, _ = main_c.shape
    col = lambda idx: pl.BlockSpec((1, lc, B_W), lambda b: (b, 0, idx))
    return pl.pallas_call(
        _na_ctx_kernel,
        grid=(b_,),
        in_specs=[col(8), col(9), col(10), col(11)],
        out_specs=pl.BlockSpec((1, lc, B_W), lambda b: (b, 0, 0)),
        out_shape=jax.ShapeDtypeStruct((b_, lc, B_W), BF16),
        name="na_ctx",
    )(main_c, main_c, main_c, main_c)


GLA_TILE = 128


def _bwd_chunk(n, nc_c, nc_l):
    return jnp.where(n < nc_c, nc_c - 1 - n, nc_c + nc_l - 1 - (n - nc_c))


def _gla_kernel(lc, ll,
                qc_ref, kc_ref, vc_ref, rc_ref, zc_ref, ql_ref, kl_ref, vl_ref, rl_ref, zl_ref,
                cos_ref, sin_ref, w2f_ref, w2b_ref, b2_ref, nw_ref, oc_ref, ol_ref,
                q_s, k_s, v_s, bf_s, bb_s, qdf_s, kdf_s, qdb_s, kdb_s, glf_s, glb_s, o_s, st_s):
    c = C_CHUNK
    nc_c, nc_l = lc // c, ll // c
    per_tile = GLA_TILE // c
    scale = C_DK ** -0.5

    ti = _iota2((GLA_TILE, GLA_TILE), 0)
    tj = _iota2((GLA_TILE, GLA_TILE), 1)
    same = (ti // c) == (tj // c)
    tri_l = jnp.where(same & (tj <= ti), 1.0, 0.0).astype(F32)
    tri_u = jnp.where(same & (tj >= ti), 1.0, 0.0).astype(F32)
    ones_b = jnp.where(same, 1.0, 0.0).astype(F32)
    rot = (jnp.where((tj % 16 < 8) & (ti == tj + 8), -1.0, 0.0)
           + jnp.where((tj % 16 >= 8) & (ti == tj - 8), 1.0, 0.0)).astype(BF16)
    expand = jnp.where(_iota2((C_QW, C_VW), 0) // C_DK == _iota2((C_QW, C_VW), 1) // C_DV,
                       1.0, 0.0).astype(BF16)
    st_mask = jnp.where(_iota2((C_VW, C_QW), 0) // C_DV == _iota2((C_VW, C_QW), 1) // C_DK,
                        1.0, 0.0).astype(F32)
    head_ones = jnp.where(_iota2((C_VW, C_VW), 0) // C_DV == _iota2((C_VW, C_VW), 1) // C_DV,
                          1.0, 0.0).astype(BF16)

    def log_sigmoid(x):
        return jnp.minimum(x, 0.0) - jnp.log1p(jnp.exp(-jnp.abs(x)))

    def prologue(q_ref, k_ref, v_ref, r_ref, base, n_tiles, rope):
        def body(t, carry):
            off = pl.multiple_of(t * GLA_TILE, GLA_TILE)
            uoff = pl.multiple_of(base + t * GLA_TILE, GLA_TILE)
            qb = q_ref[0, pl.ds(off, GLA_TILE), :]
            kb = k_ref[0, pl.ds(off, GLA_TILE), :]
            q = qb.astype(F32)
            k = kb.astype(F32)
            if rope:
                cs = cos_ref[pl.ds(off, GLA_TILE), :]
                sn = sin_ref[pl.ds(off, GLA_TILE), :]
                q = q * cs + _mm(qb, rot) * sn
                k = k * cs + _mm(kb, rot) * sn
            q = q * scale
            r = r_ref[0, pl.ds(off, GLA_TILE), :]
            gkf = log_sigmoid(_mm(r, w2f_ref[...], HI) + b2_ref[0:1, :]) / C_GATE_NORM
            gkb = log_sigmoid(_mm(r, w2b_ref[...], HI) + b2_ref[1:2, :]) / C_GATE_NORM
            bf = _mm(tri_l, gkf, HI)
            tf = _mm(ones_b, gkf, HI)
            bb = _mm(tri_u, gkb, HI)
            tb = _mm(ones_b, gkb, HI)
            sl = pl.ds(uoff, GLA_TILE)
            q_s[sl, :] = q.astype(BF16)
            k_s[sl, :] = k.astype(BF16)
            v_s[sl, :] = v_ref[0, pl.ds(off, GLA_TILE), :]
            bf_s[sl, :] = bf
            bb_s[sl, :] = bb
            qdf_s[sl, :] = (q * jnp.exp(bf)).astype(BF16)
            kdf_s[sl, :] = (k * jnp.exp(tf - bf)).astype(BF16)
            qdb_s[sl, :] = (q * jnp.exp(bb)).astype(BF16)
            kdb_s[sl, :] = (k * jnp.exp(tb - bb)).astype(BF16)
            goff = pl.multiple_of((base // c + t * per_tile) * 8, 8)
            glf_s[pl.ds(goff, per_tile * 8), :] = jnp.exp(
                jnp.concatenate([tf[i * c:i * c + 8] for i in range(per_tile)], axis=0))
            glb_s[pl.ds(goff, per_tile * 8), :] = jnp.exp(
                jnp.concatenate([tb[i * c:i * c + 8] for i in range(per_tile)], axis=0))
            return carry
        lax.fori_loop(0, n_tiles, body, 0)

    prologue(qc_ref, kc_ref, vc_ref, rc_ref, 0, lc // GLA_TILE, False)
    prologue(ql_ref, kl_ref, vl_ref, rl_ref, lc, ll // GLA_TILE, True)

    jj = _iota2((c, C_QW), 0)

    def intra(ci, carry):
        off = pl.multiple_of(ci * c, c)
        q = q_s[pl.ds(off, c), :].astype(F32)
        k = k_s[pl.ds(off, c), :].astype(F32)
        v = v_s[pl.ds(off, c), :].astype(F32)
        bf = bf_s[pl.ds(off, c), :]
        bb = bb_s[pl.ds(off, c), :]
        rows = []
        for i in range(c):
            ef = jnp.where(jj <= i, jnp.exp(jnp.minimum(bf[i:i + 1] - bf, 0.0)), 0.0)
            eb = jnp.where(jj >= i, jnp.exp(jnp.minimum(bb[i:i + 1] - bb, 0.0)), 0.0)
            rows.append(((ef + eb) * q[i:i + 1] * k).astype(BF16))
        a = _mm(jnp.concatenate(rows, axis=0), expand)
        o_s[pl.ds(off, c), :] = jnp.sum(a.reshape(c, c, C_VW) * v[None], axis=1)
        return carry
    lax.fori_loop(0, nc_c + nc_l, intra, 0)

    st_s[...] = jnp.zeros_like(st_s)

    def scan(n, carry):
        for d, (qd_s, kd_s, gl_s) in enumerate(((qdf_s, kdf_s, glf_s), (qdb_s, kdb_s, glb_s))):
            ci = n if d == 0 else _bwd_chunk(n, nc_c, nc_l)
            off = pl.multiple_of(ci * c, c)
            st = st_s[d]
            o_s[pl.ds(off, c), :] += _nt(qd_s[pl.ds(off, c), :], st.astype(BF16))
            upd = _tn(v_s[pl.ds(off, c), :], kd_s[pl.ds(off, c), :])
            gl = gl_s[pl.ds(pl.multiple_of(ci * 8, 8), 8), :][0:1]
            st_s[d] = st * gl + upd * st_mask
        return carry
    lax.fori_loop(0, nc_c + nc_l, scan, 0)

    def epilogue(z_ref, o_ref, base, n_tiles):
        def body(t, carry):
            off = pl.multiple_of(t * GLA_TILE, GLA_TILE)
            o = o_s[pl.ds(pl.multiple_of(base + t * GLA_TILE, GLA_TILE), GLA_TILE), :]
            sq = o * o
            hi = sq.astype(BF16)
            lo = (sq - hi.astype(F32)).astype(BF16)
            ms = (_mm(hi, head_ones) + _mm(lo, head_ones)) * (1.0 / C_DV)
            o = o * lax.rsqrt(ms + 1e-6) * nw_ref[...]
            z = z_ref[0, pl.ds(off, GLA_TILE), :].astype(F32)
            o_ref[0, pl.ds(off, GLA_TILE), :] = (o * _silu(z)).astype(BF16)
            return carry
        lax.fori_loop(0, n_tiles, body, 0)

    epilogue(zc_ref, oc_ref, 0, lc // GLA_TILE)
    epilogue(zl_ref, ol_ref, lc, ll // GLA_TILE)


def _rope_tables(s):
    t = np.arange(s)
    nf = C_DK // 4
    inv = ROPE_THETA ** (-np.arange(nf, dtype=np.float64) / nf)
    lane = np.arange(C_DK)
    pos = np.where((lane // (C_DK // 2))[None, :] == 0, (t // GRID_W)[:, None], (t % GRID_W)[:, None])
    ang = pos.astype(np.float32) * inv.astype(np.float32)[lane % nf][None, :]
    cos = np.tile(np.cos(ang).astype(np.float32), (1, C_HEADS))
    sin = np.tile(np.sin(ang).astype(np.float32), (1, C_HEADS))
    return jnp.asarray(cos), jnp.asarray(sin)


def _gla(main_c, small_c, main_l, small_l, w2, b2, norm_w):
    b_, lc, _ = main_c.shape
    ll = main_l.shape[1]
    lt = lc + ll
    assert lc % GLA_TILE == 0 and ll % GLA_TILE == 0
    cos, sin = _rope_tables(ll)
    w2f = jnp.zeros((SMALL_W, C_QW), F32).at[16:16 + C_RANK].set(w2[0].astype(F32))
    w2b = jnp.zeros((SMALL_W, C_QW), F32).at[16 + C_RANK:16 + 2 * C_RANK].set(w2[1].astype(F32))
    nw = jnp.tile(norm_w.astype(F32), C_HEADS).reshape(1, C_VW)
    nchunks = lt // C_CHUNK

    def seg(l, w, idx):
        return pl.BlockSpec((1, l, w), lambda b: (b, 0, idx))

    def full(shape):
        return pl.BlockSpec(shape, lambda b: (0,) * len(shape))

    return pl.pallas_call(
        functools.partial(_gla_kernel, lc, ll),
        grid=(b_,),
        in_specs=[seg(lc, C_QW, 24), seg(lc, C_QW, 25), seg(lc, C_VW, 13), seg(lc, SMALL_W, 0), seg(lc, C_VW, 14),
                  seg(ll, C_QW, 24), seg(ll, C_QW, 25), seg(ll, C_VW, 13), seg(ll, SMALL_W, 0), seg(ll, C_VW, 14),
                  full((ll, C_QW)), full((ll, C_QW)), full((SMALL_W, C_QW)), full((SMALL_W, C_QW)),
                  full((2, C_QW)), full((1, C_VW))],
        out_specs=[seg(lc, C_VW, 0), seg(ll, C_VW, 0)],
        out_shape=[jax.ShapeDtypeStruct((b_, lc, C_VW), BF16), jax.ShapeDtypeStruct((b_, ll, C_VW), BF16)],
        scratch_shapes=[pltpu.VMEM((lt, C_QW), BF16), pltpu.VMEM((lt, C_QW), BF16), pltpu.VMEM((lt, C_VW), BF16),
                        pltpu.VMEM((lt, C_QW), F32), pltpu.VMEM((lt, C_QW), F32),
                        pltpu.VMEM((lt, C_QW), BF16), pltpu.VMEM((lt, C_QW), BF16),
                        pltpu.VMEM((lt, C_QW), BF16), pltpu.VMEM((lt, C_QW), BF16),
                        pltpu.VMEM((nchunks * 8, C_QW), F32), pltpu.VMEM((nchunks * 8, C_QW), F32),
                        pltpu.VMEM((lt, C_VW), F32), pltpu.VMEM((2, C_VW, C_QW), F32)],
        compiler_params=pltpu.CompilerParams(vmem_limit_bytes=VMEM_LIMIT),
        name="gla",
    )(main_c, main_c, main_c, small_c, main_c, main_l, main_l, main_l, small_l, main_l,
      cos, sin, w2f, w2b, b2.astype(F32), nw)


GDN_TILE = 128
CONV_HALO = 8


def _gdn_kernel(lc, ll,
                xqc_ref, xkc_ref, xvc_ref, sc_ref, zc_ref, xql_ref, xkl_ref, xvl_ref, sl_ref, zl_ref,
                cwq_ref, cwk_ref, cwv_ref, alog_ref, dtb_ref, nw_ref, oc_ref, ol_ref,
                pad_s, q_s, k_s, v_s, gf_s, gb_s, bef_s, beb_s, o_s, st_s):
    c = A_CHUNK
    nc_c, nc_l = lc // c, ll // c
    h = pl.program_id(1)

    def conv_seg(x_ref, w_ref, dst, base, l, mode):
        t = min(256, l)
        pad_s[0:CONV_HALO, :] = jnp.zeros((CONV_HALO, LANES), F32)
        pad_s[CONV_HALO + l:2 * CONV_HALO + l, :] = jnp.zeros((CONV_HALO, LANES), F32)

        def fill(i, carry):
            off = pl.multiple_of(i * t, t)
            pad_s[pl.ds(pl.multiple_of(CONV_HALO + i * t, 8), t), :] = x_ref[0, pl.ds(off, t), :].astype(F32)
            return carry
        lax.fori_loop(0, l // t, fill, 0)
        w = w_ref[...]

        def body(i, carry):
            win = pad_s[pl.ds(pl.multiple_of(i * t, t), t + 2 * CONV_HALO), :]
            acc = jnp.zeros((t, LANES), F32)
            for j in range(CONV_K):
                shift = (CONV_K // 2 - j) % (t + 2 * CONV_HALO)
                sh = win if shift == 0 else pltpu.roll(win, shift, 0)
                acc = acc + sh[CONV_HALO:CONV_HALO + t] * w[j:j + 1]
            y = _silu(acc)
            if mode != "v":
                y = y * lax.rsqrt(jnp.sum(y * y, -1, keepdims=True) + 1e-6)
            if mode == "q":
                y = y * (A_DK ** -0.5)
            dst[pl.ds(pl.multiple_of(base + i * t, t), t), :] = y
            return carry
        lax.fori_loop(0, l // t, body, 0)

    for x_ref, w_ref, dst, mode in ((xqc_ref, cwq_ref, q_s, "q"), (xkc_ref, cwk_ref, k_s, "k"),
                                    (xvc_ref, cwv_ref, v_s, "v")):
        conv_seg(x_ref, w_ref, dst, 0, lc, mode)
    for x_ref, w_ref, dst, mode in ((xql_ref, cwq_ref, q_s, "q"), (xkl_ref, cwk_ref, k_s, "k"),
                                    (xvl_ref, cwv_ref, v_s, "v")):
        conv_seg(x_ref, w_ref, dst, lc, ll, mode)

    ti = _iota2((GDN_TILE, GDN_TILE), 0)
    tj = _iota2((GDN_TILE, GDN_TILE), 1)
    same = (ti // c) == (tj // c)
    tri_l = jnp.where(same & (tj <= ti), 1.0, 0.0).astype(F32)
    tri_u = jnp.where(same & (tj >= ti), 1.0, 0.0).astype(F32)
    sels = [jnp.where(ti == h + A_HEADS * kind, 1.0, 0.0).astype(F32) for kind in range(4)]
    neg_a = -jnp.exp(alog_ref[...])
    lane1 = _iota2((GDN_TILE, LANES), 1)

    def gates(s_ref, base, n_tiles):
        def body(t, carry):
            off = pl.multiple_of(t * GDN_TILE, GDN_TILE)
            x = s_ref[0, pl.ds(off, GDN_TILE), :]
            g = jnp.where(lane1 < 2 * A_HEADS, neg_a * _softplus(x + dtb_ref[...]), _sigmoid(x))
            sl = pl.ds(pl.multiple_of(base + t * GDN_TILE, GDN_TILE), GDN_TILE)
            gf_s[sl, :] = _mm(tri_l, _mm(g, sels[0], HI), HI)
            gb_s[sl, :] = _mm(tri_u, _mm(g, sels[1], HI), HI)
            bef_s[sl, :] = _mm(g, sels[2], HI)
            beb_s[sl, :] = _mm(g, sels[3], HI)
            return carry
        lax.fori_loop(0, n_tiles, body, 0)

    gates(sc_ref, 0, lc // GDN_TILE)
    gates(sl_ref, lc, ll // GDN_TILE)

    ci_ = _iota2((c, c), 0)
    cj_ = _iota2((c, c), 1)
    eye = jnp.where(ci_ == cj_, 1.0, 0.0).astype(F32)
    e0 = jnp.where(_iota2((c, LANES), 1) == 0, 1.0, 0.0).astype(F32)
    o_s[...] = jnp.zeros_like(o_s)
    st_s[...] = jnp.zeros_like(st_s)

    def scan(n, carry):
        for d, (g_s, be_s) in enumerate(((gf_s, bef_s), (gb_s, beb_s))):
            ci = n if d == 0 else _bwd_chunk(n, nc_c, nc_l)
            sl = pl.ds(pl.multiple_of(ci * c, c), c)
            q = q_s[sl, :]
            k = k_s[sl, :]
            v = v_s[sl, :]
            gc = g_s[sl, :]
            beta = be_s[sl, :]
            incl = (cj_ <= ci_) if d == 0 else (cj_ >= ci_)
            strict = (cj_ < ci_) if d == 0 else (cj_ > ci_)
            diff = gc[:, :c] - _nt(e0, gc, HI)
            decay = jnp.where(incl, jnp.exp(jnp.minimum(diff, 0.0)), 0.0)
            kb = k * beta
            kbf = k.astype(BF16)
            neg = -jnp.where(strict, _nt(kb.astype(BF16), kbf) * decay, 0.0)
            tinv = eye + neg
            pw = neg
            for _ in range(5):
                pwb = pw.astype(BF16)
                pw = _mm(pwb, pwb)
                tinv = tinv + _mm(tinv.astype(BF16), pw.astype(BF16))
            eg = jnp.exp(gc)
            rhs = jnp.concatenate([v * beta, kb * eg], axis=-1).astype(BF16)
            sol = _mm(tinv.astype(BF16), rhs)
            u, w = sol[:, :A_DV], sol[:, A_DV:]
            qk = _nt(q.astype(BF16), kbf) * decay
            g_last = gc[c - 1:c] if d == 0 else gc[0:1]
            k_dec = k * jnp.exp(g_last - gc)
            st = st_s[d]
            stb = st.astype(BF16)
            v_new = u - _mm(w.astype(BF16), stb)
            vnb = v_new.astype(BF16)
            o_s[sl, :] += _mm((q * eg).astype(BF16), stb) + _mm(qk.astype(BF16), vnb)
            st_s[d] = st * jnp.exp(g_last) + _tn(k_dec.astype(BF16), vnb)
        return carry
    lax.fori_loop(0, nc_c + nc_l, scan, 0)

    def epilogue(z_ref, o_ref, base, n_tiles):
        def body(t, carry):
            off = pl.multiple_of(t * GDN_TILE, GDN_TILE)
            o = o_s[pl.ds(pl.multiple_of(base + t * GDN_TILE, GDN_TILE), GDN_TILE), :]
            o = o * lax.rsqrt(jnp.mean(o * o, -1, keepdims=True) + 1e-6) * nw_ref[...]
            z = z_ref[0, pl.ds(off, GDN_TILE), :].astype(F32)
            o_ref[0, pl.ds(off, GDN_TILE), :] = (o * _silu(z)).astype(BF16)
            return carry
        lax.fori_loop(0, n_tiles, body, 0)

    epilogue(zc_ref, oc_ref, 0, lc // GDN_TILE)
    epilogue(zl_ref, ol_ref, lc, ll // GDN_TILE)


def _gdn(main_c, small_c, main_l, small_l, conv_w, a_log, dt_bias, norm_w):
    b_, lc, _ = main_c.shape
    ll = main_l.shape[1]
    lt = lc + ll
    assert lc % 256 == 0 and ll % 256 == 0
    pad8 = lambda p: jnp.zeros((1, LANES), F32).at[0, :2 * A_HEADS].set(p.astype(F32).reshape(-1))
    cw = conv_w.astype(F32)

    def seg(l, w, idx_fn):
        return pl.BlockSpec((1, l, w), lambda b, h: (b, 0, idx_fn(h)))

    def cw_spec(part):
        return pl.BlockSpec((CONV_K, A_DK), lambda b, h: (0, part * A_HEADS + h))

    const = lambda shape: pl.BlockSpec(shape, lambda b, h: (0,) * len(shape))
    head_col = lambda part: (lambda h: part * A_HEADS + h)
    in_specs = []
    for l in (lc, ll):
        in_specs += [seg(l, A_DK, head_col(0)), seg(l, A_DK, head_col(1)), seg(l, A_DK, head_col(2)),
                     seg(l, SMALL_W, lambda h: 0), seg(l, A_DV, head_col(3))]
    in_specs += [cw_spec(0), cw_spec(1), cw_spec(2), const((1, LANES)), const((1, LANES)), const((1, A_DV))]
    tok = lambda: pltpu.VMEM((lt, LANES), F32)
    return pl.pallas_call(
        functools.partial(_gdn_kernel, lc, ll),
        grid=(b_, A_HEADS),
        in_specs=in_specs,
        out_specs=[pl.BlockSpec((1, lc, A_DV), lambda b, h: (b, 0, h)),
                   pl.BlockSpec((1, ll, A_DV), lambda b, h: (b, 0, h))],
        out_shape=[jax.ShapeDtypeStruct((b_, lc, A_W), BF16), jax.ShapeDtypeStruct((b_, ll, A_W), BF16)],
        scratch_shapes=[pltpu.VMEM((max(lc, ll) + 2 * CONV_HALO, LANES), F32),
                        tok(), tok(), tok(), tok(), tok(), tok(), tok(), tok(),
                        pltpu.VMEM((2, A_DK, A_DV), F32)],
        compiler_params=pltpu.CompilerParams(vmem_limit_bytes=VMEM_LIMIT),
        name="gdn",
    )(main_c, main_c, main_c, small_c, main_c, main_l, main_l, main_l, small_l, main_l,
      cw, cw, cw, pad8(a_log), pad8(dt_bias), norm_w.astype(F32).reshape(1, A_DV))


def _permute_w_in(w):
    sizes = (3 * A_W, 4 * A_HEADS, A_W, B_W, B_W, B_W, B_W, C_QW, C_QW, C_VW, 2 * C_RANK, C_VW)
    offs = np.concatenate([[0], np.cumsum(sizes)])
    part = lambda i: w[:, offs[i]:offs[i + 1]]
    pad = jnp.zeros((w.shape[0], SMALL_W - 4 * A_HEADS - 2 * C_RANK), w.dtype)
    order = [part(0), part(2), part(3), part(4), part(5), part(6), part(7), part(8), part(9), part(11),
             part(1), part(10), pad]
    return jnp.concatenate(order, axis=1).astype(BF16)


def kernel(x, c, ctx, c_ctx, w_mod, b_mod, w_in, conv_w, a_log, dt_bias, gdn_norm, rpb,
           gla_w2, gla_b2, gla_norm, w_out, ln_g, ln_b):
    depth = w_mod.shape[0]
    b_ = x.shape[0]
    alpha = (2 * depth) ** 0.25
    n_mod = -(-(b_ + 1) // 8) * 8
    c_all = jnp.zeros((n_mod, D_MODEL), F32).at[:b_].set(c).at[b_].set(c_ctx)
    xl, xc = x, ctx
    for i in range(depth):
        ctx_out = i < depth - 1
        mod = _modulation(c_all, w_mod[i], b_mod[i]).reshape(n_mod, 3, 1, D_MODEL)
        sh_l, sc_l, gt_l = mod[:b_, 0], mod[:b_, 1], mod[:b_, 2]
        sh_c, sc_c, gt_c = mod[b_:b_ + 1, 0], mod[b_:b_ + 1, 1], mod[b_:b_ + 1, 2]
        w_perm = _permute_w_in(w_in[i])
        main_l, small_l = _project(xl, sh_l, sc_l, w_perm, True)
        main_c, small_c = _project(xc, sh_c, sc_c, w_perm, False)
        ya_c, ya_l = _gdn(main_c, small_c, main_l, small_l, conv_w[i], a_log[i], dt_bias[i], gdn_norm[i])
        yb_l = _na_latent(main_l, main_c, _na_bias_table(rpb[i]))
        yc_c, yc_l = _gla(main_c, small_c, main_l, small_l, gla_w2[i], gla_b2[i], gla_norm[i])
        w_o = w_out[i].astype(BF16)
        if ctx_out:
            yb_c = _na_ctx(main_c)
            xc = _out_project(xc, ya_c, yb_c, yc_c, w_o, gt_c, ln_g[i], ln_b[i], False, alpha)
        xl = _out_project(xl, ya_l, yb_l, yc_l, w_o, gt_l, ln_g[i], ln_b[i], True, alpha)
    return xl
```

```python
import functools
import math

import numpy as np
import jax
import jax.numpy as jnp
from jax import lax
from jax.experimental import pallas as pl
from jax.experimental.pallas import tpu as pltpu

F32 = jnp.float32
BF16 = jnp.bfloat16
HI = lax.Precision.HIGHEST

D_MODEL = 1024
GRID_W = 64
A_HEADS, A_DK, A_DV, CONV_K, A_CHUNK = 4, 128, 128, 5, 64
B_HEADS, B_DH, WIN_R, WIN_C = 4, 64, 8, 16
C_HEADS, C_DK, C_DV, C_RANK, C_CHUNK = 4, 32, 64, 16, 16
C_GATE_NORM = 16.0
ROPE_THETA = 10000.0
LN_EPS = 1e-6
NEG_INF = -1e30

A_W = A_HEADS * A_DK
B_W = B_HEADS * B_DH
C_QW = C_HEADS * C_DK
C_VW = C_HEADS * C_DV
MAIN_W = 3 * A_W + A_W + 4 * B_W + 2 * C_QW + 2 * C_VW
SMALL_W = 128
PROJ_W = MAIN_W + SMALL_W
LANES = 128
VMEM_LIMIT = 56 * 1024 * 1024


def _nt(a, b, precision=None):
    return lax.dot_general(a, b, (((1,), (1,)), ((), ())), precision=precision,
                           preferred_element_type=F32)


def _tn(a, b):
    return lax.dot_general(a, b, (((0,), (0,)), ((), ())), preferred_element_type=F32)


def _mm(a, b, precision=None):
    return jnp.dot(a, b, precision=precision, preferred_element_type=F32)


def _sigmoid(x):
    return 1.0 / (1.0 + jnp.exp(-x))


def _silu(x):
    return x * _sigmoid(x)


def _softplus(x):
    return jnp.maximum(x, 0.0) + jnp.log1p(jnp.exp(-jnp.abs(x)))


def _iota2(shape, axis):
    return lax.broadcasted_iota(jnp.int32, shape, axis)


def _mod_kernel(c_ref, w_ref, b_ref, o_ref):
    o_ref[...] = _mm(_silu(c_ref[...]), w_ref[...], HI) + b_ref[...]


def _modulation(c_all, w_mod, b_mod):
    n = c_all.shape[0]
    tn = 768
    return pl.pallas_call(
        _mod_kernel,
        grid=(3 * D_MODEL // tn,),
        in_specs=[pl.BlockSpec((n, D_MODEL), lambda j: (0, 0)),
                  pl.BlockSpec((D_MODEL, tn), lambda j: (0, j)),
                  pl.BlockSpec((1, tn), lambda j: (0, j))],
        out_specs=pl.BlockSpec((n, tn), lambda j: (0, j)),
        out_shape=jax.ShapeDtypeStruct((n, 3 * D_MODEL), F32),
        name="adaln_mod",
    )(c_all, w_mod, b_mod.reshape(1, -1))


def _layer_norm(x):
    mu = jnp.mean(x, -1, keepdims=True)
    xc = x - mu
    var = jnp.mean(xc * xc, -1, keepdims=True)
    return xc * lax.rsqrt(var + LN_EPS)


def _proj_kernel(x_ref, sh_ref, sc_ref, w_ref, o_ref, s_ref):
    m = (_layer_norm(x_ref[0]) * (1.0 + sc_ref[0]) + sh_ref[0]).astype(BF16)
    step = 640
    for c0 in range(0, MAIN_W, step):
        o_ref[0, :, c0:c0 + step] = _mm(m, w_ref[:, c0:c0 + step]).astype(BF16)
    s_ref[0] = _mm(m, w_ref[:, MAIN_W:])


def _project(x, sh, sc, w_perm, per_batch):
    b_, l, _ = x.shape
    t = min(512, l)
    mod_map = (lambda b, i: (b, 0, 0)) if per_batch else (lambda b, i: (0, 0, 0))
    return pl.pallas_call(
        _proj_kernel,
        grid=(b_, l // t),
        in_specs=[pl.BlockSpec((1, t, D_MODEL), lambda b, i: (b, i, 0)),
                  pl.BlockSpec((1, 1, D_MODEL), mod_map),
                  pl.BlockSpec((1, 1, D_MODEL), mod_map),
                  pl.BlockSpec((D_MODEL, PROJ_W), lambda b, i: (0, 0))],
        out_specs=[pl.BlockSpec((1, t, MAIN_W), lambda b, i: (b, i, 0)),
                   pl.BlockSpec((1, t, SMALL_W), lambda b, i: (b, i, 0))],
        out_shape=[jax.ShapeDtypeStruct((b_, l, MAIN_W), BF16),
                   jax.ShapeDtypeStruct((b_, l, SMALL_W), F32)],
        compiler_params=pltpu.CompilerParams(vmem_limit_bytes=VMEM_LIMIT),
        name="ln_mod_proj",
    )(x, sh, sc, w_perm)


def _out_kernel(alpha, x_ref, ya_ref, yb_ref, yc_ref, w_ref, gt_ref, g_ref, b_ref, o_ref):
    y = jnp.concatenate([ya_ref[0], yb_ref[0], yc_ref[0]], axis=-1)
    h = alpha * x_ref[0] + gt_ref[0] * _mm(y, w_ref[...])
    o_ref[0] = _layer_norm(h) * g_ref[...] + b_ref[...]


def _out_project(x, ya, yb, yc, w_out, gt, ln_g, ln_b, per_batch, alpha):
    b_, l, _ = x.shape
    t = min(512, l)
    mod_map = (lambda b, i: (b, 0, 0)) if per_batch else (lambda b, i: (0, 0, 0))
    tok = lambda w: pl.BlockSpec((1, t, w), lambda b, i: (b, i, 0))
    return pl.pallas_call(
        functools.partial(_out_kernel, alpha),
        grid=(b_, l // t),
        in_specs=[tok(D_MODEL), tok(A_W), tok(B_W), tok(C_VW),
                  pl.BlockSpec((D_MODEL, D_MODEL), lambda b, i: (0, 0)),
                  pl.BlockSpec((1, 1, D_MODEL), mod_map),
                  pl.BlockSpec((1, D_MODEL), lambda b, i: (0, 0)),
                  pl.BlockSpec((1, D_MODEL), lambda b, i: (0, 0))],
        out_specs=tok(D_MODEL),
        out_shape=jax.ShapeDtypeStruct((b_, l, D_MODEL), F32),
        compiler_params=pltpu.CompilerParams(vmem_limit_bytes=VMEM_LIMIT),
        name="out_proj_postnorm",
    )(x, ya, yb, yc, w_out, gt, ln_g.reshape(1, -1), ln_b.reshape(1, -1))


NA_ROWS_PER_STEP = 4


def _stack_heads(q):
    head = _iota2(q.shape, 1) // B_DH
    return jnp.concatenate([jnp.where(head == h, q, jnp.zeros_like(q)) for h in range(B_HEADS)], axis=0)


def _unstack_heads(o, n):
    head = _iota2((n, B_W), 1) // B_DH
    out = jnp.zeros((n, B_W), F32)
    for h in range(B_HEADS):
        out = jnp.where(head == h, o[h * n:(h + 1) * n], out)
    return out


def _na_kernel(rows, q_ref, k_ref, v_ref, kc_ref, vc_ref, z_ref, bias_ref, o_ref):
    kc = kc_ref[0]
    vc = vc_ref[0]
    n_win = WIN_R * GRID_W
    for rr in range(NA_ROWS_PER_STEP):
        r = pl.program_id(1) * NA_ROWS_PER_STEP + rr
        r0 = jnp.clip(r - WIN_R // 2, 0, rows - WIN_R)
        start = pl.multiple_of(r0 * GRID_W, GRID_W)
        kw = k_ref[0, pl.ds(start, n_win), :]
        vw = v_ref[0, pl.ds(start, n_win), :]
        tok = slice(rr * GRID_W, (rr + 1) * GRID_W)
        qs = _stack_heads(q_ref[0, tok, :] * (B_DH ** -0.5))
        s_win = _nt(qs, kw) + bias_ref[r - r0]
        s_ctx = _nt(qs, kc)
        m = jnp.maximum(jnp.max(s_win, -1, keepdims=True), jnp.max(s_ctx, -1, keepdims=True))
        p_win = jnp.exp(s_win - m)
        p_ctx = jnp.exp(s_ctx - m)
        den = jnp.sum(p_win, -1, keepdims=True) + jnp.sum(p_ctx, -1, keepdims=True)
        o = (_mm(p_win.astype(BF16), vw) + _mm(p_ctx.astype(BF16), vc)) / den
        o = _unstack_heads(o, GRID_W)
        o_ref[0, tok, :] = (o * _silu(z_ref[0, tok, :].astype(F32))).astype(BF16)


def _na_bias_table(rpb):
    cq = np.arange(GRID_W)
    c0 = np.clip(cq - WIN_C // 2, 0, GRID_W - WIN_C)
    col_ok = (cq[None, :] >= c0[:, None]) & (cq[None, :] < c0[:, None] + WIN_C)
    dj = np.clip(cq[None, :] - cq[:, None] + (WIN_C - 1), 0, 2 * WIN_C - 2)
    onehot = (dj[None] == np.arange(2 * WIN_C - 1)[:, None, None]).astype(np.float32)
    t = jnp.einsum('hdk,kqc->hdqc', rpb.astype(F32), jnp.asarray(onehot), precision=HI)
    t = jnp.where(col_ok[None, None], t, NEG_INF)
    slabs = [jnp.transpose(t[:, WIN_R - 1 - v:2 * WIN_R - 1 - v], (0, 2, 1, 3)) for v in range(WIN_R)]
    return jnp.stack(slabs).reshape(WIN_R, B_HEADS * GRID_W, WIN_R * GRID_W)


def _na_latent(main_l, main_c, bias):
    b_, s, _ = main_l.shape
    lc = main_c.shape[1]
    rows = s // GRID_W
    assert rows >= WIN_R and rows % NA_ROWS_PER_STEP == 0
    tq = NA_ROWS_PER_STEP * GRID_W
    col = lambda idx: (lambda b, i: (b, 0, idx))
    return pl.pallas_call(
        functools.partial(_na_kernel, rows),
        grid=(b_, rows // NA_ROWS_PER_STEP),
        in_specs=[pl.BlockSpec((1, tq, B_W), lambda b, i: (b, i, 8)),
                  pl.BlockSpec((1, s, B_W), col(9)),
                  pl.BlockSpec((1, s, B_W), col(10)),
                  pl.BlockSpec((1, lc, B_W), col(9)),
                  pl.BlockSpec((1, lc, B_W), col(10)),
                  pl.BlockSpec((1, tq, B_W), lambda b, i: (b, i, 11)),
                  pl.BlockSpec(bias.shape, lambda b, i: (0, 0, 0))],
        out_specs=pl.BlockSpec((1, tq, B_W), lambda b, i: (b, i, 0)),
        out_shape=jax.ShapeDtypeStruct((b_, s, B_W), BF16),
        compiler_params=pltpu.CompilerParams(vmem_limit_bytes=VMEM_LIMIT),
        name="na_latent",
    )(main_l, main_l, main_l, main_c, main_c, main_l, bias)


def _na_ctx_kernel(q_ref, k_ref, v_ref, z_ref, o_ref):
    lc = q_ref.shape[1]
    qs = _stack_heads(q_ref[0] * (B_DH ** -0.5))
    s = _nt(qs, k_ref[0])
    p = jnp.exp(s - jnp.max(s, -1, keepdims=True))
    o = _mm(p.astype(BF16), v_ref[0]) / jnp.sum(p, -1, keepdims=True)
    o = _unstack_heads(o, lc)
    o_ref[0] = (o * _silu(z_ref[0].astype(F32))).astype(BF16)


def _na_ctx(main_c):
    b_, lc, _ = main_c.shape
    col = lambda idx: pl.BlockSpec((1, lc, B_W), lambda b: (b, 0, idx))
    return pl.pallas_call(
        _na_ctx_kernel,
        grid=(b_,),
        in_specs=[col(8), col(9), col(10), col(11)],
        out_specs=pl.BlockSpec((1, lc, B_W), lambda b: (b, 0, 0)),
        out_shape=jax.ShapeDtypeStruct((b_, lc, B_W), BF16),
        name="na_ctx",
    )(main_c, main_c, main_c, main_c)


GLA_TILE = 128


def _bwd_chunk(n, nc_c, nc_l):
    return jnp.where(n < nc_c, nc_c - 1 - n, nc_c + nc_l - 1 - (n - nc_c))


def _gla_kernel(lc, ll,
                qc_ref, kc_ref, vc_ref, rc_ref, zc_ref, ql_ref, kl_ref, vl_ref, rl_ref, zl_ref,
                cos_ref, sin_ref, w2f_ref, w2b_ref, b2_ref, nw_ref, oc_ref, ol_ref,
                q_s, k_s, v_s, bf_s, bb_s, qdf_s, kdf_s, qdb_s, kdb_s, glf_s, glb_s, o_s, st_s):
    c = C_CHUNK
    nc_c, nc_l = lc // c, ll // c
    per_tile = GLA_TILE // c
    scale = C_DK ** -0.5

    ti = _iota2((GLA_TILE, GLA_TILE), 0)
    tj = _iota2((GLA_TILE, GLA_TILE), 1)
    same = (ti // c) == (tj // c)
    tri_l = jnp.where(same & (tj <= ti), 1.0, 0.0).astype(F32)
    tri_u = jnp.where(same & (tj >= ti), 1.0, 0.0).astype(F32)
    ones_b = jnp.where(same, 1.0, 0.0).astype(F32)
    rot = (jnp.where((tj % 16 < 8) & (ti == tj + 8), -1.0, 0.0)
           + jnp.where((tj % 16 >= 8) & (ti == tj - 8), 1.0, 0.0)).astype(BF16)
    expand = jnp.where(_iota2((C_QW, C_VW), 0) // C_DK == _iota2((C_QW, C_VW), 1) // C_DV,
                       1.0, 0.0).astype(BF16)
    st_mask = jnp.where(_iota2((C_VW, C_QW), 0) // C_DV == _iota2((C_VW, C_QW), 1) // C_DK,
                        1.0, 0.0).astype(F32)
    head_ones = jnp.where(_iota2((C_VW, C_VW), 0) // C_DV == _iota2((C_VW, C_VW), 1) // C_DV,
                          1.0, 0.0).astype(BF16)

    def log_sigmoid(x):
        return jnp.minimum(x, 0.0) - jnp.log1p(jnp.exp(-jnp.abs(x)))

    def prologue(q_ref, k_ref, v_ref, r_ref, base, n_tiles, rope):
        def body(t, carry):
            off = pl.multiple_of(t * GLA_TILE, GLA_TILE)
            uoff = pl.multiple_of(base + t * GLA_TILE, GLA_TILE)
            qb = q_ref[0, pl.ds(off, GLA_TILE), :]
            kb = k_ref[0, pl.ds(off, GLA_TILE), :]
            q = qb.astype(F32)
            k = kb.astype(F32)
            if rope:
                cs = cos_ref[pl.ds(off, GLA_TILE), :]
                sn = sin_ref[pl.ds(off, GLA_TILE), :]
                q = q * cs + _mm(qb, rot) * sn
                k = k * cs + _mm(kb, rot) * sn
            q = q * scale
            r = r_ref[0, pl.ds(off, GLA_TILE), :]
            gkf = log_sigmoid(_mm(r, w2f_ref[...], HI) + b2_ref[0:1, :]) / C_GATE_NORM
            gkb = log_sigmoid(_mm(r, w2b_ref[...], HI) + b2_ref[1:2, :]) / C_GATE_NORM
            bf = _mm(tri_l, gkf, HI)
            tf = _mm(ones_b, gkf, HI)
            bb = _mm(tri_u, gkb, HI)
            tb = _mm(ones_b, gkb, HI)
            sl = pl.ds(uoff, GLA_TILE)
            q_s[sl, :] = q.astype(BF16)
            k_s[sl, :] = k.astype(BF16)
            v_s[sl, :] = v_ref[0, pl.ds(off, GLA_TILE), :]
            bf_s[sl, :] = bf
            bb_s[sl, :] = bb
            qdf_s[sl, :] = (q * jnp.exp(bf)).astype(BF16)
            kdf_s[sl, :] = (k * jnp.exp(tf - bf)).astype(BF16)
            qdb_s[sl, :] = (q * jnp.exp(bb)).astype(BF16)
            kdb_s[sl, :] = (k * jnp.exp(tb - bb)).astype(BF16)
            goff = pl.multiple_of((base // c + t * per_tile) * 8, 8)
            glf_s[pl.ds(goff, per_tile * 8), :] = jnp.exp(
                jnp.concatenate([tf[i * c:i * c + 8] for i in range(per_tile)], axis=0))
            glb_s[pl.ds(goff, per_tile * 8), :] = jnp.exp(
                jnp.concatenate([tb[i * c:i * c + 8] for i in range(per_tile)], axis=0))
            return carry
        lax.fori_loop(0, n_tiles, body, 0)

    prologue(qc_ref, kc_ref, vc_ref, rc_ref, 0, lc // GLA_TILE, False)
    prologue(ql_ref, kl_ref, vl_ref, rl_ref, lc, ll // GLA_TILE, True)

    jj = _iota2((c, C_QW), 0)

    def intra(ci, carry):
        off = pl.multiple_of(ci * c, c)
        q = q_s[pl.ds(off, c), :].astype(F32)
        k = k_s[pl.ds(off, c), :].astype(F32)
        v = v_s[pl.ds(off, c), :].astype(F32)
        bf = bf_s[pl.ds(off, c), :]
        bb = bb_s[pl.ds(off, c), :]
        rows = []
        for i in range(c):
            ef = jnp.where(jj <= i, jnp.exp(jnp.minimum(bf[i:i + 1] - bf, 0.0)), 0.0)
            eb = jnp.where(jj >= i, jnp.exp(jnp.minimum(bb[i:i + 1] - bb, 0.0)), 0.0)
            rows.append(((ef + eb) * q[i:i + 1] * k).astype(BF16))
        a = _mm(jnp.concatenate(rows, axis=0), expand)
        o_s[pl.ds(off, c), :] = jnp.sum(a.reshape(c, c, C_VW) * v[None], axis=1)
        return carry
    lax.fori_loop(0, nc_c + nc_l, intra, 0)

    st_s[...] = jnp.zeros_like(st_s)

    def scan(n, carry):
        for d, (qd_s, kd_s, gl_s) in enumerate(((qdf_s, kdf_s, glf_s), (qdb_s, kdb_s, glb_s))):
            ci = n if d == 0 else _bwd_chunk(n, nc_c, nc_l)
            off = pl.multiple_of(ci * c, c)
            st = st_s[d]
            o_s[pl.ds(off, c), :] += _nt(qd_s[pl.ds(off, c), :], st.astype(BF16))
            upd = _tn(v_s[pl.ds(off, c), :], kd_s[pl.ds(off, c), :])
            gl = gl_s[pl.ds(pl.multiple_of(ci * 8, 8), 8), :][0:1]
            st_s[d] = st * gl + upd * st_mask
        return carry
    lax.fori_loop(0, nc_c + nc_l, scan, 0)

    def epilogue(z_ref, o_ref, base, n_tiles):
        def body(t, carry):
            off = pl.multiple_of(t * GLA_TILE, GLA_TILE)
            o = o_s[pl.ds(pl.multiple_of(base + t * GLA_TILE, GLA_TILE), GLA_TILE), :]
            sq = o * o
            hi = sq.astype(BF16)
            lo = (sq - hi.astype(F32)).astype(BF16)
            ms = (_mm(hi, head_ones) + _mm(lo, head_ones)) * (1.0 / C_DV)
            o = o * lax.rsqrt(ms + 1e-6) * nw_ref[...]
            z = z_ref[0, pl.ds(off, GLA_TILE), :].astype(F32)
            o_ref[0, pl.ds(off, GLA_TILE), :] = (o * _silu(z)).astype(BF16)
            return carry
        lax.fori_loop(0, n_tiles, body, 0)

    epilogue(zc_ref, oc_ref, 0, lc // GLA_TILE)
    epilogue(zl_ref, ol_ref, lc, ll // GLA_TILE)


def _rope_tables(s):
    t = np.arange(s)
    nf = C_DK // 4
    inv = ROPE_THETA ** (-np.arange(nf, dtype=np.float64) / nf)
    lane = np.arange(C_DK)
    pos = np.where((lane // (C_DK // 2))[None, :] == 0, (t // GRID_W)[:, None], (t % GRID_W)[:, None])
    ang = pos.astype(np.float32) * inv.astype(np.float32)[lane % nf][None, :]
    cos = np.tile(np.cos(ang).astype(np.float32), (1, C_HEADS))
    sin = np.tile(np.sin(ang).astype(np.float32), (1, C_HEADS))
    return jnp.asarray(cos), jnp.asarray(sin)


def _gla(main_c, small_c, main_l, small_l, w2, b2, norm_w):
    b_, lc, _ = main_c.shape
    ll = main_l.shape[1]
    lt = lc + ll
    assert lc % GLA_TILE == 0 and ll % GLA_TILE == 0
    cos, sin = _rope_tables(ll)
    w2f = jnp.zeros((SMALL_W, C_QW), F32).at[16:16 + C_RANK].set(w2[0].astype(F32))
    w2b = jnp.zeros((SMALL_W, C_QW), F32).at[16 + C_RANK:16 + 2 * C_RANK].set(w2[1].astype(F32))
    nw = jnp.tile(norm_w.astype(F32), C_HEADS).reshape(1, C_VW)
    nchunks = lt // C_CHUNK

    def seg(l, w, idx):
        return pl.BlockSpec((1, l, w), lambda b: (b, 0, idx))

    def full(shape):
        return pl.BlockSpec(shape, lambda b: (0,) * len(shape))

    return pl.pallas_call(
        functools.partial(_gla_kernel, lc, ll),
        grid=(b_,),
        in_specs=[seg(lc, C_QW, 24), seg(lc, C_QW, 25), seg(lc, C_VW, 13), seg(lc, SMALL_W, 0), seg(lc, C_VW, 14),
                  seg(ll, C_QW, 24), seg(ll, C_QW, 25), seg(ll, C_VW, 13), seg(ll, SMALL_W, 0), seg(ll, C_VW, 14),
                  full((ll, C_QW)), full((ll, C_QW)), full((SMALL_W, C_QW)), full((SMALL_W, C_QW)),
                  full((2, C_QW)), full((1, C_VW))],
        out_specs=[seg(lc, C_VW, 0), seg(ll, C_VW, 0)],
        out_shape=[jax.ShapeDtypeStruct((b_, lc, C_VW), BF16), jax.ShapeDtypeStruct((b_, ll, C_VW), BF16)],
        scratch_shapes=[pltpu.VMEM((lt, C_QW), BF16), pltpu.VMEM((lt, C_QW), BF16), pltpu.VMEM((lt, C_VW), BF16),
                        pltpu.VMEM((lt, C_QW), F32), pltpu.VMEM((lt, C_QW), F32),
                        pltpu.VMEM((lt, C_QW), BF16), pltpu.VMEM((lt, C_QW), BF16),
                        pltpu.VMEM((lt, C_QW), BF16), pltpu.VMEM((lt, C_QW), BF16),
                        pltpu.VMEM((nchunks * 8, C_QW), F32), pltpu.VMEM((nchunks * 8, C_QW), F32),
                        pltpu.VMEM((lt, C_VW), F32), pltpu.VMEM((2, C_VW, C_QW), F32)],
        compiler_params=pltpu.CompilerParams(vmem_limit_bytes=VMEM_LIMIT),
        name="gla",
    )(main_c, main_c, main_c, small_c, main_c, main_l, main_l, main_l, small_l, main_l,
      cos, sin, w2f, w2b, b2.astype(F32), nw)


GDN_CHUNK = 128
GDN_BASE = 16
GDN_LEVELS = 3
M_INCL, M_STRICT, M_BASE, M_LEVEL, M_EYE = 0, 2, 4, 5, 8
N_MASKS = 9
GDN_PRE_CHUNKS = 2
CONV_HALO = 8


def _gdn_kernel(lc, ll,
                xqc_ref, xkc_ref, xvc_ref, sc_ref, zc_ref, xql_ref, xkl_ref, xvl_ref, sl_ref, zl_ref,
                cwq_ref, cwk_ref, cwv_ref, alog_ref, dtb_ref, nw_ref, oc_ref, ol_ref,
                pad_s, q_s, k_s, v_s, gf_s, gb_s, bef_s, beb_s, u_s, w_s, qd_s, kdt_s, qk_s, gl_s, msk_s, o_s, st_s):
    c = GDN_CHUNK
    nc_c, nc_l = lc // c, ll // c
    h = pl.program_id(1)

    def conv_seg(x_ref, w_ref, dst, base, l, mode):
        t = min(256, l)
        pad_s[0:CONV_HALO, :] = jnp.zeros((CONV_HALO, LANES), F32)
        pad_s[CONV_HALO + l:2 * CONV_HALO + l, :] = jnp.zeros((CONV_HALO, LANES), F32)

        def fill(i, carry):
            off = pl.multiple_of(i * t, t)
            pad_s[pl.ds(pl.multiple_of(CONV_HALO + i * t, 8), t), :] = x_ref[0, pl.ds(off, t), :].astype(F32)
            return carry
        lax.fori_loop(0, l // t, fill, 0)
        w = w_ref[...]

        def body(i, carry):
            win = pad_s[pl.ds(pl.multiple_of(i * t, t), t + 2 * CONV_HALO), :]
            acc = jnp.zeros((t, LANES), F32)
            for j in range(CONV_K):
                shift = (CONV_K // 2 - j) % (t + 2 * CONV_HALO)
                sh = win if shift == 0 else pltpu.roll(win, shift, 0)
                acc = acc + sh[CONV_HALO:CONV_HALO + t] * w[j:j + 1]
            y = _silu(acc)
            if mode != "v":
                y = y * lax.rsqrt(jnp.sum(y * y, -1, keepdims=True) + 1e-6)
            if mode == "q":
                y = y * (A_DK ** -0.5)
            dst[pl.ds(pl.multiple_of(base + i * t, t), t), :] = y
            return carry
        lax.fori_loop(0, l // t, body, 0)

    for x_ref, w_ref, dst, mode in ((xqc_ref, cwq_ref, q_s, "q"), (xkc_ref, cwk_ref, k_s, "k"),
                                    (xvc_ref, cwv_ref, v_s, "v")):
        conv_seg(x_ref, w_ref, dst, 0, lc, mode)
    for x_ref, w_ref, dst, mode in ((xql_ref, cwq_ref, q_s, "q"), (xkl_ref, cwk_ref, k_s, "k"),
                                    (xvl_ref, cwv_ref, v_s, "v")):
        conv_seg(x_ref, w_ref, dst, lc, ll, mode)

    ti = _iota2((c, c), 0)
    tj = _iota2((c, c), 1)
    as_f32 = lambda m: jnp.where(m, 1.0, 0.0).astype(F32)
    tri_l = as_f32(tj <= ti)
    tri_u = as_f32(tj >= ti)
    msk_s[M_INCL] = tri_l
    msk_s[M_INCL + 1] = tri_u
    msk_s[M_STRICT] = as_f32(tj < ti)
    msk_s[M_STRICT + 1] = as_f32(tj > ti)
    msk_s[M_BASE] = as_f32(ti // GDN_BASE == tj // GDN_BASE)
    for lv in range(GDN_LEVELS):
        s = GDN_BASE << lv
        msk_s[M_LEVEL + lv] = as_f32((ti // (2 * s) == tj // (2 * s)) & (ti // s != tj // s))
    msk_s[M_EYE] = as_f32(ti == tj)
    neg_a = -jnp.exp(alog_ref[...])
    lane1 = _iota2((c, LANES), 1)

    def lane_bcast(g, idx):
        col = jnp.sum(jnp.where(lane1 == idx, g, 0.0), axis=-1, keepdims=True)
        return jnp.broadcast_to(col, (c, LANES))

    def gates(s_ref, base, n_tiles):
        def body(t, carry):
            off = pl.multiple_of(t * c, c)
            x = s_ref[0, pl.ds(off, c), :]
            g = jnp.where(lane1 < 2 * A_HEADS, neg_a * _softplus(x + dtb_ref[...]), _sigmoid(x))
            sl = pl.ds(pl.multiple_of(base + t * c, c), c)
            gf_s[sl, :] = _mm(tri_l, lane_bcast(g, h), HI)
            gb_s[sl, :] = _mm(tri_u, lane_bcast(g, h + A_HEADS), HI)
            bef_s[sl, :] = lane_bcast(g, h + 2 * A_HEADS)
            beb_s[sl, :] = lane_bcast(g, h + 3 * A_HEADS)
            return carry
        lax.fori_loop(0, n_tiles, body, 0)

    gates(sc_ref, 0, nc_c)
    gates(sl_ref, lc, nc_l)

    def prepass(it, carry):
        chains = []
        for j in range(GDN_PRE_CHUNKS):
            ci = it * GDN_PRE_CHUNKS + j
            sl = pl.ds(pl.multiple_of(ci * c, c), c)
            q = q_s[sl, :]
            k = k_s[sl, :]
            kbf = k.astype(BF16)
            gram = _nt(kbf, kbf)
            qk_raw = _nt(q.astype(BF16), kbf)
            for d, (g_s, be_s) in enumerate(((gf_s, bef_s), (gb_s, beb_s))):
                chains.append(dict(ci=ci, sl=sl, d=d, q=q, k=k, gram=gram, qk_raw=qk_raw,
                                   gc=g_s[sl, :], beta=be_s[sl, :]))
        for ch in chains:
            gc = ch["gc"]
            ch["decay"] = jnp.exp(jnp.minimum(gc - gc.T, 0.0)) * msk_s[M_INCL + ch["d"]]
            ch["a"] = ch["gram"] * ch["beta"] * ch["decay"] * msk_s[M_STRICT + ch["d"]]
            n0 = -(ch["a"] * msk_s[M_BASE])
            ch["inv"] = msk_s[M_EYE] + n0
            ch["pw"] = n0.astype(BF16)
        n_sq = GDN_BASE.bit_length() - 2
        for ch in chains:
            ch["pw"] = _mm(ch["pw"], ch["pw"]).astype(BF16)
        for m in range(n_sq):
            for ch in chains:
                ch["inv_next"] = ch["inv"] + _mm(ch["inv"].astype(BF16), ch["pw"])
            if m + 1 < n_sq:
                for ch in chains:
                    ch["pw"] = _mm(ch["pw"], ch["pw"]).astype(BF16)
            for ch in chains:
                ch["inv"] = ch["inv_next"]
        for lv in range(GDN_LEVELS):
            for ch in chains:
                ch["invb"] = ch["inv"].astype(BF16)
                ch["t"] = _mm((ch["a"] * msk_s[M_LEVEL + lv]).astype(BF16), ch["invb"]).astype(BF16)
            for ch in chains:
                ch["inv"] = ch["inv"] - _mm(ch["invb"], ch["t"])
        for ch in chains:
            gc, beta, q, k, d, sl = ch["gc"], ch["beta"], ch["q"], ch["k"], ch["d"], ch["sl"]
            eg = jnp.exp(gc)
            rhs = jnp.concatenate([v_s[sl, :] * beta, k * beta * eg], axis=-1).astype(BF16)
            sol = _mm(ch["inv"].astype(BF16), rhs)
            g_last = gc[c - 1:c] if d == 0 else gc[0:1]
            u_s[d, sl, :] = sol[:, :A_DV].astype(BF16)
            w_s[d, sl, :] = sol[:, A_DV:].astype(BF16)
            qk_s[d, sl, :] = (ch["qk_raw"] * ch["decay"]).astype(BF16)
            qd_s[d, sl, :] = (q * eg).astype(BF16)
            kdt_s[d, sl, :] = (k * jnp.exp(g_last - gc)).T.astype(BF16)
            gl_s[d, pl.ds(pl.multiple_of(ch["ci"] * 8, 8), 8), :] = jnp.exp(jnp.broadcast_to(g_last, (8, LANES)))
        return carry
    lax.fori_loop(0, (nc_c + nc_l) // GDN_PRE_CHUNKS, prepass, 0)

    o_s[...] = jnp.zeros_like(o_s)
    st_s[...] = jnp.zeros_like(st_s)

    def scan(n, carry):
        sls = [pl.ds(pl.multiple_of(ci * c, c), c) for ci in (n, _bwd_chunk(n, nc_c, nc_l))]
        cis = (n, _bwd_chunk(n, nc_c, nc_l))
        sts = [st_s[d] for d in range(2)]
        stbs = [st.astype(BF16) for st in sts]
        v_new = [u_s[d, sls[d], :].astype(F32) - _mm(w_s[d, sls[d], :], stbs[d]) for d in range(2)]
        o_st = [_mm(qd_s[d, sls[d], :], stbs[d]) for d in range(2)]
        vnb = [vn.astype(BF16) for vn in v_new]
        for d in range(2):
            gl = gl_s[d, pl.ds(pl.multiple_of(cis[d] * 8, 8), 8), :][0:1]
            st_s[d] = sts[d] * gl + _mm(kdt_s[d, sls[d], :], vnb[d])
        for d in range(2):
            o_s[sls[d], :] += o_st[d] + _mm(qk_s[d, sls[d], :], vnb[d])
        return carry
    lax.fori_loop(0, nc_c + nc_l, scan, 0)

    def epilogue(z_ref, o_ref, base, n_tiles):
        def body(t, carry):
            off = pl.multiple_of(t * GDN_CHUNK, GDN_CHUNK)
            o = o_s[pl.ds(pl.multiple_of(base + t * GDN_CHUNK, GDN_CHUNK), GDN_CHUNK), :]
            o = o * lax.rsqrt(jnp.mean(o * o, -1, keepdims=True) + 1e-6) * nw_ref[...]
            z = z_ref[0, pl.ds(off, GDN_CHUNK), :].astype(F32)
            o_ref[0, pl.ds(off, GDN_CHUNK), :] = (o * _silu(z)).astype(BF16)
            return carry
        lax.fori_loop(0, n_tiles, body, 0)

    epilogue(zc_ref, oc_ref, 0, lc // GDN_CHUNK)
    epilogue(zl_ref, ol_ref, lc, ll // GDN_CHUNK)


def _gdn(main_c, small_c, main_l, small_l, conv_w, a_log, dt_bias, norm_w):
    b_, lc, _ = main_c.shape
    ll = main_l.shape[1]
    lt = lc + ll
    assert lc % 256 == 0 and ll % 256 == 0 and (lt // GDN_CHUNK) % GDN_PRE_CHUNKS == 0
    pad8 = lambda p: jnp.zeros((1, LANES), F32).at[0, :2 * A_HEADS].set(p.astype(F32).reshape(-1))
    cw = conv_w.astype(F32)

    def seg(l, w, idx_fn):
        return pl.BlockSpec((1, l, w), lambda b, h: (b, 0, idx_fn(h)))

    def cw_spec(part):
        return pl.BlockSpec((CONV_K, A_DK), lambda b, h: (0, part * A_HEADS + h))

    const = lambda shape: pl.BlockSpec(shape, lambda b, h: (0,) * len(shape))
    head_col = lambda part: (lambda h: part * A_HEADS + h)
    in_specs = []
    for l in (lc, ll):
        in_specs += [seg(l, A_DK, head_col(0)), seg(l, A_DK, head_col(1)), seg(l, A_DK, head_col(2)),
                     seg(l, SMALL_W, lambda h: 0), seg(l, A_DV, head_col(3))]
    in_specs += [cw_spec(0), cw_spec(1), cw_spec(2), const((1, LANES)), const((1, LANES)), const((1, A_DV))]
    tok = lambda: pltpu.VMEM((lt, LANES), F32)
    per_dir = lambda: pltpu.VMEM((2, lt, LANES), BF16)
    return pl.pallas_call(
        functools.partial(_gdn_kernel, lc, ll),
        grid=(b_, A_HEADS),
        in_specs=in_specs,
        out_specs=[pl.BlockSpec((1, lc, A_DV), lambda b, h: (b, 0, h)),
                   pl.BlockSpec((1, ll, A_DV), lambda b, h: (b, 0, h))],
        out_shape=[jax.ShapeDtypeStruct((b_, lc, A_W), BF16), jax.ShapeDtypeStruct((b_, ll, A_W), BF16)],
        scratch_shapes=[pltpu.VMEM((max(lc, ll) + 2 * CONV_HALO, LANES), F32),
                        tok(), tok(), tok(), tok(), tok(), tok(), tok(),
                        per_dir(), per_dir(), per_dir(), per_dir(), per_dir(),
                        pltpu.VMEM((2, (lt // GDN_CHUNK) * 8, LANES), F32),
                        pltpu.VMEM((N_MASKS, GDN_CHUNK, GDN_CHUNK), F32),
                        tok(), pltpu.VMEM((2, A_DK, A_DV), F32)],
        compiler_params=pltpu.CompilerParams(vmem_limit_bytes=VMEM_LIMIT),
        name="gdn",
    )(main_c, main_c, main_c, small_c, main_c, main_l, main_l, main_l, small_l, main_l,
      cw, cw, cw, pad8(a_log), pad8(dt_bias), norm_w.astype(F32).reshape(1, A_DV))


def _permute_w_in(w):
    sizes = (3 * A_W, 4 * A_HEADS, A_W, B_W, B_W, B_W, B_W, C_QW, C_QW, C_VW, 2 * C_RANK, C_VW)
    offs = np.concatenate([[0], np.cumsum(sizes)])
    part = lambda i: w[:, offs[i]:offs[i + 1]]
    pad = jnp.zeros((w.shape[0], SMALL_W - 4 * A_HEADS - 2 * C_RANK), w.dtype)
    order = [part(0), part(2), part(3), part(4), part(5), part(6), part(7), part(8), part(9), part(11),
             part(1), part(10), pad]
    return jnp.concatenate(order, axis=1).astype(BF16)


def kernel(x, c, ctx, c_ctx, w_mod, b_mod, w_in, conv_w, a_log, dt_bias, gdn_norm, rpb,
           gla_w2, gla_b2, gla_norm, w_out, ln_g, ln_b):
    depth = w_mod.shape[0]
    b_ = x.shape[0]
    alpha = (2 * depth) ** 0.25
    n_mod = -(-(b_ + 1) // 8) * 8
    c_all = jnp.zeros((n_mod, D_MODEL), F32).at[:b_].set(c).at[b_].set(c_ctx)
    xl, xc = x, ctx
    for i in range(depth):
        ctx_out = i < depth - 1
        mod = _modulation(c_all, w_mod[i], b_mod[i]).reshape(n_mod, 3, 1, D_MODEL)
        sh_l, sc_l, gt_l = mod[:b_, 0], mod[:b_, 1], mod[:b_, 2]
        sh_c, sc_c, gt_c = mod[b_:b_ + 1, 0], mod[b_:b_ + 1, 1], mod[b_:b_ + 1, 2]
        w_perm = _permute_w_in(w_in[i])
        main_l, small_l = _project(xl, sh_l, sc_l, w_perm, True)
        main_c, small_c = _project(xc, sh_c, sc_c, w_perm, False)
        ya_c, ya_l = _gdn(main_c, small_c, main_l, small_l, conv_w[i], a_log[i], dt_bias[i], gdn_norm[i])
        yb_l = _na_latent(main_l, main_c, _na_bias_table(rpb[i]))
        yc_c, yc_l = _gla(main_c, small_c, main_l, small_l, gla_w2[i], gla_b2[i], gla_norm[i])
        w_o = w_out[i].astype(BF16)
        if ctx_out:
            yb_c = _na_ctx(main_c)
            xc = _out_project(xc, ya_c, yb_c, yc_c, w_o, gt_c, ln_g[i], ln_b[i], False, alpha)
        xl = _out_project(xl, ya_l, yb_l, yc_l, w_o, gt_l, ln_g[i], ln_b[i], True, alpha)
    return xl
```

```python
import functools
import math

import numpy as np
import jax
import jax.numpy as jnp
from jax import lax
from jax.experimental import pallas as pl
from jax.experimental.pallas import tpu as pltpu

F32 = jnp.float32
BF16 = jnp.bfloat16
HI = lax.Precision.HIGHEST

D_MODEL = 1024
GRID_W = 64
A_HEADS, A_DK, A_DV, CONV_K, A_CHUNK = 4, 128, 128, 5, 64
B_HEADS, B_DH, WIN_R, WIN_C = 4, 64, 8, 16
C_HEADS, C_DK, C_DV, C_RANK, C_CHUNK = 4, 32, 64, 16, 16
C_GATE_NORM = 16.0
ROPE_THETA = 10000.0
LN_EPS = 1e-6
NEG_INF = -1e30

A_W = A_HEADS * A_DK
B_W = B_HEADS * B_DH
C_QW = C_HEADS * C_DK
C_VW = C_HEADS * C_DV
MAIN_W = 3 * A_W + A_W + 4 * B_W + 2 * C_QW + 2 * C_VW
SMALL_W = 128
PROJ_W = MAIN_W + SMALL_W
LANES = 128
VMEM_LIMIT = 56 * 1024 * 1024


def _nt(a, b, precision=None):
    return lax.dot_general(a, b, (((1,), (1,)), ((), ())), precision=precision,
                           preferred_element_type=F32)


def _tn(a, b):
    return lax.dot_general(a, b, (((0,), (0,)), ((), ())), preferred_element_type=F32)


def _mm(a, b, precision=None):
    return jnp.dot(a, b, precision=precision, preferred_element_type=F32)


def _mm_01(m01, x):
    hi = x.astype(BF16)
    lo = (x - hi.astype(F32)).astype(BF16)
    return _mm(m01, hi) + _mm(m01, lo)


def _sigmoid(x):
    return 1.0 / (1.0 + jnp.exp(-x))


def _silu(x):
    return x * _sigmoid(x)


def _softplus(x):
    return jnp.maximum(x, 0.0) + jnp.log1p(jnp.exp(-jnp.abs(x)))


def _iota2(shape, axis):
    return lax.broadcasted_iota(jnp.int32, shape, axis)


def _mod_kernel(c_ref, w_ref, b_ref, o_ref):
    o_ref[...] = _mm(_silu(c_ref[...]), w_ref[...], HI) + b_ref[...]


def _modulation(c_all, w_mod, b_mod):
    n = c_all.shape[0]
    tn = 768
    return pl.pallas_call(
        _mod_kernel,
        grid=(3 * D_MODEL // tn,),
        in_specs=[pl.BlockSpec((n, D_MODEL), lambda j: (0, 0)),
                  pl.BlockSpec((D_MODEL, tn), lambda j: (0, j)),
                  pl.BlockSpec((1, tn), lambda j: (0, j))],
        out_specs=pl.BlockSpec((n, tn), lambda j: (0, j)),
        out_shape=jax.ShapeDtypeStruct((n, 3 * D_MODEL), F32),
        name="adaln_mod",
    )(c_all, w_mod, b_mod.reshape(1, -1))


def _layer_norm(x):
    mu = jnp.mean(x, -1, keepdims=True)
    xc = x - mu
    var = jnp.mean(xc * xc, -1, keepdims=True)
    return xc * lax.rsqrt(var + LN_EPS)


def _proj_kernel(x_ref, sh_ref, sc_ref, w_ref, o_ref, s_ref):
    m = (_layer_norm(x_ref[0]) * (1.0 + sc_ref[0]) + sh_ref[0]).astype(BF16)
    step = 640
    for c0 in range(0, MAIN_W, step):
        o_ref[0, :, c0:c0 + step] = _mm(m, w_ref[:, c0:c0 + step]).astype(BF16)
    s_ref[0] = _mm(m, w_ref[:, MAIN_W:])


def _project(x, sh, sc, w_perm, per_batch):
    b_, l, _ = x.shape
    t = min(512, l)
    mod_map = (lambda b, i: (b, 0, 0)) if per_batch else (lambda b, i: (0, 0, 0))
    return pl.pallas_call(
        _proj_kernel,
        grid=(b_, l // t),
        in_specs=[pl.BlockSpec((1, t, D_MODEL), lambda b, i: (b, i, 0)),
                  pl.BlockSpec((1, 1, D_MODEL), mod_map),
                  pl.BlockSpec((1, 1, D_MODEL), mod_map),
                  pl.BlockSpec((D_MODEL, PROJ_W), lambda b, i: (0, 0))],
        out_specs=[pl.BlockSpec((1, t, MAIN_W), lambda b, i: (b, i, 0)),
                   pl.BlockSpec((1, t, SMALL_W), lambda b, i: (b, i, 0))],
        out_shape=[jax.ShapeDtypeStruct((b_, l, MAIN_W), BF16),
                   jax.ShapeDtypeStruct((b_, l, SMALL_W), F32)],
        compiler_params=pltpu.CompilerParams(vmem_limit_bytes=VMEM_LIMIT),
        name="ln_mod_proj",
    )(x, sh, sc, w_perm)


def _out_kernel(alpha, x_ref, ya_ref, yb_ref, yc_ref, w_ref, gt_ref, g_ref, b_ref, o_ref):
    y = jnp.concatenate([ya_ref[0], yb_ref[0], yc_ref[0]], axis=-1)
    h = alpha * x_ref[0] + gt_ref[0] * _mm(y, w_ref[...])
    o_ref[0] = _layer_norm(h) * g_ref[...] + b_ref[...]


def _out_project(x, ya, yb, yc, w_out, gt, ln_g, ln_b, per_batch, alpha):
    b_, l, _ = x.shape
    t = min(512, l)
    mod_map = (lambda b, i: (b, 0, 0)) if per_batch else (lambda b, i: (0, 0, 0))
    tok = lambda w: pl.BlockSpec((1, t, w), lambda b, i: (b, i, 0))
    return pl.pallas_call(
        functools.partial(_out_kernel, alpha),
        grid=(b_, l // t),
        in_specs=[tok(D_MODEL), tok(A_W), tok(B_W), tok(C_VW),
                  pl.BlockSpec((D_MODEL, D_MODEL), lambda b, i: (0, 0)),
                  pl.BlockSpec((1, 1, D_MODEL), mod_map),
                  pl.BlockSpec((1, D_MODEL), lambda b, i: (0, 0)),
                  pl.BlockSpec((1, D_MODEL), lambda b, i: (0, 0))],
        out_specs=tok(D_MODEL),
        out_shape=jax.ShapeDtypeStruct((b_, l, D_MODEL), F32),
        compiler_params=pltpu.CompilerParams(vmem_limit_bytes=VMEM_LIMIT),
        name="out_proj_postnorm",
    )(x, ya, yb, yc, w_out, gt, ln_g.reshape(1, -1), ln_b.reshape(1, -1))


NA_ROWS_PER_STEP = 4


def _stack_heads(q):
    head = _iota2(q.shape, 1) // B_DH
    return jnp.concatenate([jnp.where(head == h, q, jnp.zeros_like(q)) for h in range(B_HEADS)], axis=0)


def _unstack_heads(o, n):
    head = _iota2((n, B_W), 1) // B_DH
    out = jnp.zeros((n, B_W), F32)
    for h in range(B_HEADS):
        out = jnp.where(head == h, o[h * n:(h + 1) * n], out)
    return out


def _na_kernel(rows, q_ref, k_ref, v_ref, kc_ref, vc_ref, z_ref, bias_ref, o_ref):
    kc = kc_ref[0]
    vc = vc_ref[0]
    n_win = WIN_R * GRID_W
    for rr in range(NA_ROWS_PER_STEP):
        r = pl.program_id(1) * NA_ROWS_PER_STEP + rr
        r0 = jnp.clip(r - WIN_R // 2, 0, rows - WIN_R)
        start = pl.multiple_of(r0 * GRID_W, GRID_W)
        kw = k_ref[0, pl.ds(start, n_win), :]
        vw = v_ref[0, pl.ds(start, n_win), :]
        tok = slice(rr * GRID_W, (rr + 1) * GRID_W)
        qs = _stack_heads(q_ref[0, tok, :] * (B_DH ** -0.5))
        s_win = _nt(qs, kw) + bias_ref[r - r0]
        s_ctx = _nt(qs, kc)
        m = jnp.maximum(jnp.max(s_win, -1, keepdims=True), jnp.max(s_ctx, -1, keepdims=True))
        p_win = jnp.exp(s_win - m)
        p_ctx = jnp.exp(s_ctx - m)
        den = jnp.sum(p_win, -1, keepdims=True) + jnp.sum(p_ctx, -1, keepdims=True)
        o = (_mm(p_win.astype(BF16), vw) + _mm(p_ctx.astype(BF16), vc)) / den
        o = _unstack_heads(o, GRID_W)
        o_ref[0, tok, :] = (o * _silu(z_ref[0, tok, :].astype(F32))).astype(BF16)


def _na_bias_table(rpb):
    cq = np.arange(GRID_W)
    c0 = np.clip(cq - WIN_C // 2, 0, GRID_W - WIN_C)
    col_ok = (cq[None, :] >= c0[:, None]) & (cq[None, :] < c0[:, None] + WIN_C)
    dj = np.clip(cq[None, :] - cq[:, None] + (WIN_C - 1), 0, 2 * WIN_C - 2)
    onehot = (dj[None] == np.arange(2 * WIN_C - 1)[:, None, None]).astype(np.float32)
    t = jnp.einsum('hdk,kqc->hdqc', rpb.astype(F32), jnp.asarray(onehot), precision=HI)
    t = jnp.where(col_ok[None, None], t, NEG_INF)
    slabs = [jnp.transpose(t[:, WIN_R - 1 - v:2 * WIN_R - 1 - v], (0, 2, 1, 3)) for v in range(WIN_R)]
    return jnp.stack(slabs).reshape(WIN_R, B_HEADS * GRID_W, WIN_R * GRID_W)


def _na_latent(main_l, main_c, bias):
    b_, s, _ = main_l.shape
    lc = main_c.shape[1]
    rows = s // GRID_W
    assert rows >= WIN_R and rows % NA_ROWS_PER_STEP == 0
    tq = NA_ROWS_PER_STEP * GRID_W
    col = lambda idx: (lambda b, i: (b, 0, idx))
    return pl.pallas_call(
        functools.partial(_na_kernel, rows),
        grid=(b_, rows // NA_ROWS_PER_STEP),
        in_specs=[pl.BlockSpec((1, tq, B_W), lambda b, i: (b, i, 8)),
                  pl.BlockSpec((1, s, B_W), col(9)),
                  pl.BlockSpec((1, s, B_W), col(10)),
                  pl.BlockSpec((1, lc, B_W), col(9)),
                  pl.BlockSpec((1, lc, B_W), col(10)),
                  pl.BlockSpec((1, tq, B_W), lambda b, i: (b, i, 11)),
                  pl.BlockSpec(bias.shape, lambda b, i: (0, 0, 0))],
        out_specs=pl.BlockSpec((1, tq, B_W), lambda b, i: (b, i, 0)),
        out_shape=jax.ShapeDtypeStruct((b_, s, B_W), BF16),
        compiler_params=pltpu.CompilerParams(vmem_limit_bytes=VMEM_LIMIT),
        name="na_latent",
    )(main_l, main_l, main_l, main_c, main_c, main_l, bias)


def _na_ctx_kernel(q_ref, k_ref, v_ref, z_ref, o_ref):
    lc = q_ref.shape[1]
    qs = _stack_heads(q_ref[0] * (B_DH ** -0.5))
    s = _nt(qs, k_ref[0])
    p = jnp.exp(s - jnp.max(s, -1, keepdims=True))
    o = _mm(p.astype(BF16), v_ref[0]) / jnp.sum(p, -1, keepdims=True)
    o = _unstack_heads(o, lc)
    o_ref[0] = (o * _silu(z_ref[0].astype(F32))).astype(BF16)


def _na_ctx(main_c):
    b_, lc, _ = main_c.shape
    col = lambda idx: pl.BlockSpec((1, lc, B_W), lambda b: (b, 0, idx))
    return pl.pallas_call(
        _na_ctx_kernel,
        grid=(b_,),
        in_specs=[col(8), col(9), col(10), col(11)],
        out_specs=pl.BlockSpec((1, lc, B_W), lambda b: (b, 0, 0)),
        out_shape=jax.ShapeDtypeStruct((b_, lc, B_W), BF16),
        name="na_ctx",
    )(main_c, main_c, main_c, main_c)


GLA_TILE = 128
GLA_INTRA_CHUNKS = 2
GLA_SCAN_STEPS = 4


def _bwd_chunk(n, nc_c, nc_l):
    return jnp.where(n < nc_c, nc_c - 1 - n, nc_c + nc_l - 1 - (n - nc_c))


def _gla_kernel(lc, ll,
                qc_ref, kc_ref, vc_ref, rc_ref, zc_ref, ql_ref, kl_ref, vl_ref, rl_ref, zl_ref,
                cos_ref, sin_ref, w2f_ref, w2b_ref, b2_ref, nw_ref, oc_ref, ol_ref,
                q_s, k_s, v_s, bf_s, bb_s, qdf_s, kdf_s, qdb_s, kdb_s, glf_s, glb_s, o_s, st_s):
    c = C_CHUNK
    nc_c, nc_l = lc // c, ll // c
    per_tile = GLA_TILE // c
    scale = C_DK ** -0.5

    ti = _iota2((GLA_TILE, GLA_TILE), 0)
    tj = _iota2((GLA_TILE, GLA_TILE), 1)
    same = (ti // c) == (tj // c)
    tri_l = jnp.where(same & (tj <= ti), 1.0, 0.0).astype(BF16)
    tri_u = jnp.where(same & (tj >= ti), 1.0, 0.0).astype(BF16)
    rot = (jnp.where((tj % 16 < 8) & (ti == tj + 8), -1.0, 0.0)
           + jnp.where((tj % 16 >= 8) & (ti == tj - 8), 1.0, 0.0)).astype(BF16)
    expand = jnp.where(_iota2((C_QW, C_VW), 0) // C_DK == _iota2((C_QW, C_VW), 1) // C_DV,
                       1.0, 0.0).astype(BF16)
    st_mask = jnp.where(_iota2((C_VW, C_QW), 0) // C_DV == _iota2((C_VW, C_QW), 1) // C_DK,
                        1.0, 0.0).astype(F32)
    head_ones = jnp.where(_iota2((C_VW, C_VW), 0) // C_DV == _iota2((C_VW, C_VW), 1) // C_DV,
                          1.0, 0.0).astype(BF16)

    def log_sigmoid(x):
        return jnp.minimum(x, 0.0) - jnp.log1p(jnp.exp(-jnp.abs(x)))

    def prologue(q_ref, k_ref, v_ref, r_ref, base, n_tiles, rope):
        def body(t, carry):
            off = pl.multiple_of(t * GLA_TILE, GLA_TILE)
            uoff = pl.multiple_of(base + t * GLA_TILE, GLA_TILE)
            qb = q_ref[0, pl.ds(off, GLA_TILE), :]
            kb = k_ref[0, pl.ds(off, GLA_TILE), :]
            q = qb.astype(F32)
            k = kb.astype(F32)
            if rope:
                cs = cos_ref[pl.ds(off, GLA_TILE), :]
                sn = sin_ref[pl.ds(off, GLA_TILE), :]
                q = q * cs + _mm(qb, rot) * sn
                k = k * cs + _mm(kb, rot) * sn
            q = q * scale
            r = r_ref[0, pl.ds(off, GLA_TILE), :].astype(BF16)
            gkf = log_sigmoid(_mm(r, w2f_ref[...]) + b2_ref[0:1, :]) / C_GATE_NORM
            gkb = log_sigmoid(_mm(r, w2b_ref[...]) + b2_ref[1:2, :]) / C_GATE_NORM
            bf = _mm_01(tri_l, gkf)
            bb = _mm_01(tri_u, gkb)
            tf = jnp.concatenate([jnp.broadcast_to(bf[i * c + c - 1:i * c + c], (c, C_QW)) for i in range(per_tile)], axis=0)
            tb = jnp.concatenate([jnp.broadcast_to(bb[i * c:i * c + 1], (c, C_QW)) for i in range(per_tile)], axis=0)
            sl = pl.ds(uoff, GLA_TILE)
            q_s[sl, :] = q.astype(BF16)
            k_s[sl, :] = k.astype(BF16)
            v_s[sl, :] = v_ref[0, pl.ds(off, GLA_TILE), :]
            bf_s[sl, :] = bf
            bb_s[sl, :] = bb
            qdf_s[sl, :] = (q * jnp.exp(bf)).astype(BF16)
            kdf_s[sl, :] = (k * jnp.exp(tf - bf)).astype(BF16)
            qdb_s[sl, :] = (q * jnp.exp(bb)).astype(BF16)
            kdb_s[sl, :] = (k * jnp.exp(tb - bb)).astype(BF16)
            goff = pl.multiple_of((base // c + t * per_tile) * 8, 8)
            glf_s[pl.ds(goff, per_tile * 8), :] = jnp.exp(
                jnp.concatenate([tf[i * c:i * c + 8] for i in range(per_tile)], axis=0))
            glb_s[pl.ds(goff, per_tile * 8), :] = jnp.exp(
                jnp.concatenate([tb[i * c:i * c + 8] for i in range(per_tile)], axis=0))
            return carry
        lax.fori_loop(0, n_tiles, body, 0, unroll=2)

    prologue(qc_ref, kc_ref, vc_ref, rc_ref, 0, lc // GLA_TILE, False)
    prologue(ql_ref, kl_ref, vl_ref, rl_ref, lc, ll // GLA_TILE, True)

    jj = _iota2((c, C_QW), 0)

    def intra(it, carry):
        offs = [pl.multiple_of((it * GLA_INTRA_CHUNKS + j) * c, c) for j in range(GLA_INTRA_CHUNKS)]
        ps = []
        for off in offs:
            q = q_s[pl.ds(off, c), :].astype(F32)
            k = k_s[pl.ds(off, c), :].astype(F32)
            bf = bf_s[pl.ds(off, c), :]
            bb = bb_s[pl.ds(off, c), :]
            rows = []
            for i in range(c):
                e = jnp.exp(jnp.where(jj < i, bf[i:i + 1] - bf, bb[i:i + 1] - bb))
                rows.append((jnp.where(jj == i, 2.0, e) * (q[i:i + 1] * k)).astype(BF16))
            ps.append(jnp.concatenate(rows, axis=0))
        accs = [_mm(p, expand) for p in ps]
        for off, a in zip(offs, accs):
            v = v_s[pl.ds(off, c), :].astype(F32)
            o_s[pl.ds(off, c), :] = jnp.sum(a.reshape(c, c, C_VW) * v[None], axis=1)
        return carry
    lax.fori_loop(0, (nc_c + nc_l) // GLA_INTRA_CHUNKS, intra, 0)

    st_s[...] = jnp.zeros_like(st_s)
    dirs = ((qdf_s, kdf_s, glf_s), (qdb_s, kdb_s, glb_s))

    def scan(it, carry):
        steps = [it * GLA_SCAN_STEPS + s for s in range(GLA_SCAN_STEPS)]
        chunk = [steps, [_bwd_chunk(n, nc_c, nc_l) for n in steps]]
        upd = [[_tn(v_s[pl.ds(pl.multiple_of(ci * c, c), c), :], dirs[d][1][pl.ds(pl.multiple_of(ci * c, c), c), :])
                for ci in chunk[d]] for d in range(2)]
        st = [st_s[0], st_s[1]]
        for s in range(GLA_SCAN_STEPS):
            for d in range(2):
                ci = chunk[d][s]
                sl = pl.ds(pl.multiple_of(ci * c, c), c)
                o_s[sl, :] += _nt(dirs[d][0][sl, :], st[d].astype(BF16))
                gl = dirs[d][2][pl.ds(pl.multiple_of(ci * 8, 8), 8), :][0:1]
                st[d] = st[d] * gl + upd[d][s] * st_mask
        st_s[0] = st[0]
        st_s[1] = st[1]
        return carry
    lax.fori_loop(0, (nc_c + nc_l) // GLA_SCAN_STEPS, scan, 0)

    def epilogue(z_ref, o_ref, base, n_tiles):
        def body(t, carry):
            off = pl.multiple_of(t * GLA_TILE, GLA_TILE)
            o = o_s[pl.ds(pl.multiple_of(base + t * GLA_TILE, GLA_TILE), GLA_TILE), :]
            sq = o * o
            hi = sq.astype(BF16)
            lo = (sq - hi.astype(F32)).astype(BF16)
            ms = (_mm(hi, head_ones) + _mm(lo, head_ones)) * (1.0 / C_DV)
            o = o * lax.rsqrt(ms + 1e-6) * nw_ref[...]
            z = z_ref[0, pl.ds(off, GLA_TILE), :].astype(F32)
            o_ref[0, pl.ds(off, GLA_TILE), :] = (o * _silu(z)).astype(BF16)
            return carry
        lax.fori_loop(0, n_tiles, body, 0, unroll=2)

    epilogue(zc_ref, oc_ref, 0, lc // GLA_TILE)
    epilogue(zl_ref, ol_ref, lc, ll // GLA_TILE)


def _rope_tables(s):
    t = np.arange(s)
    nf = C_DK // 4
    inv = ROPE_THETA ** (-np.arange(nf, dtype=np.float64) / nf)
    lane = np.arange(C_DK)
    pos = np.where((lane // (C_DK // 2))[None, :] == 0, (t // GRID_W)[:, None], (t % GRID_W)[:, None])
    ang = pos.astype(np.float32) * inv.astype(np.float32)[lane % nf][None, :]
    cos = np.tile(np.cos(ang).astype(np.float32), (1, C_HEADS))
    sin = np.tile(np.sin(ang).astype(np.float32), (1, C_HEADS))
    return jnp.asarray(cos), jnp.asarray(sin)


def _gla(main_c, small_c, main_l, small_l, w2, b2, norm_w):
    b_, lc, _ = main_c.shape
    ll = main_l.shape[1]
    lt = lc + ll
    assert lc % GLA_TILE == 0 and ll % GLA_TILE == 0
    cos, sin = _rope_tables(ll)
    w2f = jnp.zeros((SMALL_W, C_QW), BF16).at[16:16 + C_RANK].set(w2[0].astype(BF16))
    w2b = jnp.zeros((SMALL_W, C_QW), BF16).at[16 + C_RANK:16 + 2 * C_RANK].set(w2[1].astype(BF16))
    nw = jnp.tile(norm_w.astype(F32), C_HEADS).reshape(1, C_VW)
    nchunks = lt // C_CHUNK

    def seg(l, w, idx):
        return pl.BlockSpec((1, l, w), lambda b: (b, 0, idx))

    def full(shape):
        return pl.BlockSpec(shape, lambda b: (0,) * len(shape))

    return pl.pallas_call(
        functools.partial(_gla_kernel, lc, ll),
        grid=(b_,),
        in_specs=[seg(lc, C_QW, 24), seg(lc, C_QW, 25), seg(lc, C_VW, 13), seg(lc, SMALL_W, 0), seg(lc, C_VW, 14),
                  seg(ll, C_QW, 24), seg(ll, C_QW, 25), seg(ll, C_VW, 13), seg(ll, SMALL_W, 0), seg(ll, C_VW, 14),
                  full((ll, C_QW)), full((ll, C_QW)), full((SMALL_W, C_QW)), full((SMALL_W, C_QW)),
                  full((2, C_QW)), full((1, C_VW))],
        out_specs=[seg(lc, C_VW, 0), seg(ll, C_VW, 0)],
        out_shape=[jax.ShapeDtypeStruct((b_, lc, C_VW), BF16), jax.ShapeDtypeStruct((b_, ll, C_VW), BF16)],
        scratch_shapes=[pltpu.VMEM((lt, C_QW), BF16), pltpu.VMEM((lt, C_QW), BF16), pltpu.VMEM((lt, C_VW), BF16),
                        pltpu.VMEM((lt, C_QW), F32), pltpu.VMEM((lt, C_QW), F32),
                        pltpu.VMEM((lt, C_QW), BF16), pltpu.VMEM((lt, C_QW), BF16),
                        pltpu.VMEM((lt, C_QW), BF16), pltpu.VMEM((lt, C_QW), BF16),
                        pltpu.VMEM((nchunks * 8, C_QW), F32), pltpu.VMEM((nchunks * 8, C_QW), F32),
                        pltpu.VMEM((lt, C_VW), F32), pltpu.VMEM((2, C_VW, C_QW), F32)],
        compiler_params=pltpu.CompilerParams(vmem_limit_bytes=VMEM_LIMIT),
        name="gla",
    )(main_c, main_c, main_c, small_c, main_c, main_l, main_l, main_l, small_l, main_l,
      cos, sin, w2f, w2b, b2.astype(F32), nw)


GDN_CHUNK = 128
GDN_BASE = 16
GDN_LEVELS = 3
M_INCL, M_STRICT, M_BASE, M_LEVEL, M_EYE = 0, 2, 4, 5, 8
N_MASKS = 9
GDN_PRE_CHUNKS = 4
CONV_HALO = 8


def _gdn_kernel(lc, ll,
                xqc_ref, xkc_ref, xvc_ref, sc_ref, zc_ref, xql_ref, xkl_ref, xvl_ref, sl_ref, zl_ref,
                cwq_ref, cwk_ref, cwv_ref, alog_ref, dtb_ref, nw_ref, oc_ref, ol_ref,
                pad_s, q_s, k_s, v_s, gf_s, gb_s, bef_s, beb_s, u_s, w_s, qd_s, kdt_s, qk_s, gl_s, msk_s, o_s, st_s):
    c = GDN_CHUNK
    nc_c, nc_l = lc // c, ll // c
    h = pl.program_id(1)

    def conv_seg(x_ref, w_ref, dst, base, l, mode):
        t = min(256, l)
        pad_s[0:CONV_HALO, :] = jnp.zeros((CONV_HALO, LANES), F32)
        pad_s[CONV_HALO + l:2 * CONV_HALO + l, :] = jnp.zeros((CONV_HALO, LANES), F32)

        def fill(i, carry):
            off = pl.multiple_of(i * t, t)
            pad_s[pl.ds(pl.multiple_of(CONV_HALO + i * t, 8), t), :] = x_ref[0, pl.ds(off, t), :].astype(F32)
            return carry
        lax.fori_loop(0, l // t, fill, 0)
        w = w_ref[...]

        def body(i, carry):
            acc = jnp.zeros((t, LANES), F32)
            for j in range(CONV_K):
                tap = pad_s[pl.ds(i * t + (CONV_HALO - CONV_K // 2 + j), t), :]
                acc = acc + tap * w[j:j + 1]
            y = _silu(acc)
            if mode != "v":
                y = y * lax.rsqrt(jnp.sum(y * y, -1, keepdims=True) + 1e-6)
            if mode == "q":
                y = y * (A_DK ** -0.5)
            dst[pl.ds(pl.multiple_of(base + i * t, t), t), :] = y
            return carry
        lax.fori_loop(0, l // t, body, 0)

    for x_ref, w_ref, dst, mode in ((xqc_ref, cwq_ref, q_s, "q"), (xkc_ref, cwk_ref, k_s, "k"),
                                    (xvc_ref, cwv_ref, v_s, "v")):
        conv_seg(x_ref, w_ref, dst, 0, lc, mode)
    for x_ref, w_ref, dst, mode in ((xql_ref, cwq_ref, q_s, "q"), (xkl_ref, cwk_ref, k_s, "k"),
                                    (xvl_ref, cwv_ref, v_s, "v")):
        conv_seg(x_ref, w_ref, dst, lc, ll, mode)

    ti = _iota2((c, c), 0)
    tj = _iota2((c, c), 1)
    as_f32 = lambda m: jnp.where(m, 1.0, 0.0).astype(F32)
    tri_l = as_f32(tj <= ti)
    tri_u = as_f32(tj >= ti)
    msk_s[M_INCL] = tri_l
    msk_s[M_INCL + 1] = tri_u
    msk_s[M_STRICT] = as_f32(tj < ti)
    msk_s[M_STRICT + 1] = as_f32(tj > ti)
    msk_s[M_BASE] = as_f32(ti // GDN_BASE == tj // GDN_BASE)
    for lv in range(GDN_LEVELS):
        s = GDN_BASE << lv
        msk_s[M_LEVEL + lv] = as_f32((ti // (2 * s) == tj // (2 * s)) & (ti // s != tj // s))
    msk_s[M_EYE] = as_f32(ti == tj)
    neg_a = -jnp.exp(alog_ref[...])
    lane1 = _iota2((c, LANES), 1)

    def lane_bcast(g, idx):
        col = jnp.sum(jnp.where(lane1 == idx, g, 0.0), axis=-1, keepdims=True)
        return jnp.broadcast_to(col, (c, LANES))

    def gates(s_ref, base, n_tiles):
        def body(t, carry):
            off = pl.multiple_of(t * c, c)
            x = s_ref[0, pl.ds(off, c), :]
            g = jnp.where(lane1 < 2 * A_HEADS, neg_a * _softplus(x + dtb_ref[...]), _sigmoid(x))
            sl = pl.ds(pl.multiple_of(base + t * c, c), c)
            gf_s[sl, :] = _mm_01(tri_l.astype(BF16), lane_bcast(g, h))
            gb_s[sl, :] = _mm_01(tri_u.astype(BF16), lane_bcast(g, h + A_HEADS))
            bef_s[sl, :] = lane_bcast(g, h + 2 * A_HEADS)
            beb_s[sl, :] = lane_bcast(g, h + 3 * A_HEADS)
            return carry
        lax.fori_loop(0, n_tiles, body, 0, unroll=2)

    gates(sc_ref, 0, nc_c)
    gates(sl_ref, lc, nc_l)

    def prepass(first_chunk, per_step, it, carry):
        chains = []
        for j in range(per_step):
            ci = first_chunk + it * per_step + j
            sl = pl.ds(pl.multiple_of(ci * c, c), c)
            q = q_s[sl, :]
            k = k_s[sl, :]
            kbf = k.astype(BF16)
            gram = _nt(kbf, kbf)
            qk_raw = _nt(q.astype(BF16), kbf)
            for d, (g_s, be_s) in enumerate(((gf_s, bef_s), (gb_s, beb_s))):
                chains.append(dict(ci=ci, sl=sl, d=d, q=q, k=k, gram=gram, qk_raw=qk_raw,
                                   gc=g_s[sl, :], beta=be_s[sl, :]))
        for ch in chains:
            gc = ch["gc"]
            ch["decay"] = jnp.exp(jnp.minimum(gc - gc.T, 0.0)) * msk_s[M_INCL + ch["d"]]
            ch["a"] = ch["gram"] * ch["beta"] * ch["decay"] * msk_s[M_STRICT + ch["d"]]
            n0 = -(ch["a"] * msk_s[M_BASE])
            ch["inv"] = msk_s[M_EYE] + n0
            ch["pw"] = n0.astype(BF16)
        n_sq = GDN_BASE.bit_length() - 2
        for ch in chains:
            ch["pw"] = _mm(ch["pw"], ch["pw"]).astype(BF16)
        for m in range(n_sq):
            for ch in chains:
                ch["inv_next"] = ch["inv"] + _mm(ch["inv"].astype(BF16), ch["pw"])
            if m + 1 < n_sq:
                for ch in chains:
                    ch["pw"] = _mm(ch["pw"], ch["pw"]).astype(BF16)
            for ch in chains:
                ch["inv"] = ch["inv_next"]
        for lv in range(GDN_LEVELS):
            for ch in chains:
                ch["invb"] = ch["inv"].astype(BF16)
                ch["t"] = _mm((ch["a"] * msk_s[M_LEVEL + lv]).astype(BF16), ch["invb"]).astype(BF16)
            for ch in chains:
                ch["inv"] = ch["inv"] - _mm(ch["invb"], ch["t"])
        for ch in chains:
            gc, beta, q, k, d, sl = ch["gc"], ch["beta"], ch["q"], ch["k"], ch["d"], ch["sl"]
            eg = jnp.exp(gc)
            rhs = jnp.concatenate([v_s[sl, :] * beta, k * beta * eg], axis=-1).astype(BF16)
            sol = _mm(ch["inv"].astype(BF16), rhs)
            g_last = gc[c - 1:c] if d == 0 else gc[0:1]
            u_s[d, sl, :] = sol[:, :A_DV].astype(BF16)
            w_s[d, sl, :] = sol[:, A_DV:].astype(BF16)
            qk_s[d, sl, :] = (ch["qk_raw"] * ch["decay"]).astype(BF16)
            qd_s[d, sl, :] = (q * eg).astype(BF16)
            kdt_s[d, sl, :] = (k * jnp.exp(g_last - gc)).T.astype(BF16)
            gl_s[d, pl.ds(pl.multiple_of(ch["ci"] * 8, 8), 8), :] = jnp.exp(jnp.broadcast_to(g_last, (8, LANES)))
        return carry
    for first, count in ((0, nc_c), (nc_c, nc_l)):
        per_step = math.gcd(count, GDN_PRE_CHUNKS)
        lax.fori_loop(0, count // per_step, functools.partial(prepass, first, per_step), 0)

    o_s[...] = jnp.zeros_like(o_s)
    st_s[...] = jnp.zeros_like(st_s)

    def scan(n, carry):
        sls = [pl.ds(pl.multiple_of(ci * c, c), c) for ci in (n, _bwd_chunk(n, nc_c, nc_l))]
        cis = (n, _bwd_chunk(n, nc_c, nc_l))
        sts = [st_s[d] for d in range(2)]
        stbs = [st.astype(BF16) for st in sts]
        v_new = [u_s[d, sls[d], :].astype(F32) - _mm(w_s[d, sls[d], :], stbs[d]) for d in range(2)]
        o_st = [_mm(qd_s[d, sls[d], :], stbs[d]) for d in range(2)]
        vnb = [vn.astype(BF16) for vn in v_new]
        for d in range(2):
            gl = gl_s[d, pl.ds(pl.multiple_of(cis[d] * 8, 8), 8), :][0:1]
            st_s[d] = sts[d] * gl + _mm(kdt_s[d, sls[d], :], vnb[d])
        for d in range(2):
            o_s[sls[d], :] += o_st[d] + _mm(qk_s[d, sls[d], :], vnb[d])
        return carry
    lax.fori_loop(0, nc_c + nc_l, scan, 0)

    def epilogue(z_ref, o_ref, base, n_tiles):
        def body(t, carry):
            off = pl.multiple_of(t * GDN_CHUNK, GDN_CHUNK)
            o = o_s[pl.ds(pl.multiple_of(base + t * GDN_CHUNK, GDN_CHUNK), GDN_CHUNK), :]
            o = o * lax.rsqrt(jnp.mean(o * o, -1, keepdims=True) + 1e-6) * nw_ref[...]
            z = z_ref[0, pl.ds(off, GDN_CHUNK), :].astype(F32)
            o_ref[0, pl.ds(off, GDN_CHUNK), :] = (o * _silu(z)).astype(BF16)
            return carry
        lax.fori_loop(0, n_tiles, body, 0, unroll=2)

    epilogue(zc_ref, oc_ref, 0, lc // GDN_CHUNK)
    epilogue(zl_ref, ol_ref, lc, ll // GDN_CHUNK)


def _gdn(main_c, small_c, main_l, small_l, conv_w, a_log, dt_bias, norm_w):
    b_, lc, _ = main_c.shape
    ll = main_l.shape[1]
    lt = lc + ll
    assert lc % 256 == 0 and ll % 256 == 0
    pad8 = lambda p: jnp.zeros((1, LANES), F32).at[0, :2 * A_HEADS].set(p.astype(F32).reshape(-1))
    cw = conv_w.astype(F32)

    def seg(l, w, idx_fn):
        return pl.BlockSpec((1, l, w), lambda b, h: (b, 0, idx_fn(h)))

    def cw_spec(part):
        return pl.BlockSpec((CONV_K, A_DK), lambda b, h: (0, part * A_HEADS + h))

    const = lambda shape: pl.BlockSpec(shape, lambda b, h: (0,) * len(shape))
    head_col = lambda part: (lambda h: part * A_HEADS + h)
    in_specs = []
    for l in (lc, ll):
        in_specs += [seg(l, A_DK, head_col(0)), seg(l, A_DK, head_col(1)), seg(l, A_DK, head_col(2)),
                     seg(l, SMALL_W, lambda h: 0), seg(l, A_DV, head_col(3))]
    in_specs += [cw_spec(0), cw_spec(1), cw_spec(2), const((1, LANES)), const((1, LANES)), const((1, A_DV))]
    tok = lambda: pltpu.VMEM((lt, LANES), F32)
    per_dir = lambda: pltpu.VMEM((2, lt, LANES), BF16)
    return pl.pallas_call(
        functools.partial(_gdn_kernel, lc, ll),
        grid=(b_, A_HEADS),
        in_specs=in_specs,
        out_specs=[pl.BlockSpec((1, lc, A_DV), lambda b, h: (b, 0, h)),
                   pl.BlockSpec((1, ll, A_DV), lambda b, h: (b, 0, h))],
        out_shape=[jax.ShapeDtypeStruct((b_, lc, A_W), BF16), jax.ShapeDtypeStruct((b_, ll, A_W), BF16)],
        scratch_shapes=[pltpu.VMEM((max(lc, ll) + 2 * CONV_HALO, LANES), F32),
                        tok(), tok(), tok(), tok(), tok(), tok(), tok(),
                        per_dir(), per_dir(), per_dir(), per_dir(), per_dir(),
                        pltpu.VMEM((2, (lt // GDN_CHUNK) * 8, LANES), F32),
                        pltpu.VMEM((N_MASKS, GDN_CHUNK, GDN_CHUNK), F32),
                        tok(), pltpu.VMEM((2, A_DK, A_DV), F32)],
        compiler_params=pltpu.CompilerParams(vmem_limit_bytes=VMEM_LIMIT),
        name="gdn",
    )(main_c, main_c, main_c, small_c, main_c, main_l, main_l, main_l, small_l, main_l,
      cw, cw, cw, pad8(a_log), pad8(dt_bias), norm_w.astype(F32).reshape(1, A_DV))


def _permute_w_in(w):
    sizes = (3 * A_W, 4 * A_HEADS, A_W, B_W, B_W, B_W, B_W, C_QW, C_QW, C_VW, 2 * C_RANK, C_VW)
    offs = np.concatenate([[0], np.cumsum(sizes)])
    part = lambda i: w[:, offs[i]:offs[i + 1]]
    pad = jnp.zeros((w.shape[0], SMALL_W - 4 * A_HEADS - 2 * C_RANK), w.dtype)
    order = [part(0), part(2), part(3), part(4), part(5), part(6), part(7), part(8), part(9), part(11),
             part(1), part(10), pad]
    return jnp.concatenate(order, axis=1).astype(BF16)


def kernel(x, c, ctx, c_ctx, w_mod, b_mod, w_in, conv_w, a_log, dt_bias, gdn_norm, rpb,
           gla_w2, gla_b2, gla_norm, w_out, ln_g, ln_b):
    depth = w_mod.shape[0]
    b_ = x.shape[0]
    alpha = (2 * depth) ** 0.25
    n_mod = -(-(b_ + 1) // 8) * 8
    c_all = jnp.zeros((n_mod, D_MODEL), F32).at[:b_].set(c).at[b_].set(c_ctx)
    xl, xc = x, ctx
    for i in range(depth):
        ctx_out = i < depth - 1
        mod = _modulation(c_all, w_mod[i], b_mod[i]).reshape(n_mod, 3, 1, D_MODEL)
        sh_l, sc_l, gt_l = mod[:b_, 0], mod[:b_, 1], mod[:b_, 2]
        sh_c, sc_c, gt_c = mod[b_:b_ + 1, 0], mod[b_:b_ + 1, 1], mod[b_:b_ + 1, 2]
        w_perm = _permute_w_in(w_in[i])
        main_l, small_l = _project(xl, sh_l, sc_l, w_perm, True)
        main_c, small_c = _project(xc, sh_c, sc_c, w_perm, False)
        ya_c, ya_l = _gdn(main_c, small_c, main_l, small_l, conv_w[i], a_log[i], dt_bias[i], gdn_norm[i])
        yb_l = _na_latent(main_l, main_c, _na_bias_table(rpb[i]))
        yc_c, yc_l = _gla(main_c, small_c, main_l, small_l, gla_w2[i], gla_b2[i], gla_norm[i])
        w_o = w_out[i].astype(BF16)
        if ctx_out:
            yb_c = _na_ctx(main_c)
            xc = _out_project(xc, ya_c, yb_c, yc_c, w_o, gt_c, ln_g[i], ln_b[i], False, alpha)
        xl = _out_project(xl, ya_l, yb_l, yc_l, w_o, gt_l, ln_g[i], ln_b[i], True, alpha)
    return xl
```

```python
import functools
import math

import numpy as np
import jax
import jax.numpy as jnp
from jax import lax
from jax.experimental import pallas as pl
from jax.experimental.pallas import tpu as pltpu

F32 = jnp.float32
BF16 = jnp.bfloat16
HI = lax.Precision.HIGHEST

D_MODEL = 1024
GRID_W = 64
A_HEADS, A_DK, A_DV, CONV_K, A_CHUNK = 4, 128, 128, 5, 64
B_HEADS, B_DH, WIN_R, WIN_C = 4, 64, 8, 16
C_HEADS, C_DK, C_DV, C_RANK, C_CHUNK = 4, 32, 64, 16, 16
C_GATE_NORM = 16.0
ROPE_THETA = 10000.0
LN_EPS = 1e-6
NEG_INF = -1e30

A_W = A_HEADS * A_DK
B_W = B_HEADS * B_DH
C_QW = C_HEADS * C_DK
C_VW = C_HEADS * C_DV
MAIN_W = 3 * A_W + A_W + 4 * B_W + 2 * C_QW + 2 * C_VW
SMALL_W = 128
PROJ_W = MAIN_W + SMALL_W
LANES = 128
VMEM_LIMIT = 56 * 1024 * 1024
GDN_VMEM_LIMIT = 60 * 1024 * 1024


def _nt(a, b, precision=None):
    return lax.dot_general(a, b, (((1,), (1,)), ((), ())), precision=precision,
                           preferred_element_type=F32)


def _tn(a, b):
    return lax.dot_general(a, b, (((0,), (0,)), ((), ())), preferred_element_type=F32)


def _mm(a, b, precision=None):
    return jnp.dot(a, b, precision=precision, preferred_element_type=F32)


def _mm_01(m01, x):
    hi = x.astype(BF16)
    lo = (x - hi.astype(F32)).astype(BF16)
    return _mm(m01, hi) + _mm(m01, lo)


def _sigmoid(x):
    return 1.0 / (1.0 + jnp.exp(-x))


def _silu(x):
    return x * _sigmoid(x)


def _softplus(x):
    return jnp.maximum(x, 0.0) + jnp.log1p(jnp.exp(-jnp.abs(x)))


def _iota2(shape, axis):
    return lax.broadcasted_iota(jnp.int32, shape, axis)


def _mod_kernel(c_ref, w_ref, b_ref, o_ref):
    o_ref[...] = _mm(_silu(c_ref[...]), w_ref[...], HI) + b_ref[...]


def _modulation(c_all, w_mod, b_mod):
    n = c_all.shape[0]
    tn = 768
    return pl.pallas_call(
        _mod_kernel,
        grid=(3 * D_MODEL // tn,),
        in_specs=[pl.BlockSpec((n, D_MODEL), lambda j: (0, 0)),
                  pl.BlockSpec((D_MODEL, tn), lambda j: (0, j)),
                  pl.BlockSpec((1, tn), lambda j: (0, j))],
        out_specs=pl.BlockSpec((n, tn), lambda j: (0, j)),
        out_shape=jax.ShapeDtypeStruct((n, 3 * D_MODEL), F32),
        name="adaln_mod",
    )(c_all, w_mod, b_mod.reshape(1, -1))


def _layer_norm(x):
    mu = jnp.mean(x, -1, keepdims=True)
    xc = x - mu
    var = jnp.mean(xc * xc, -1, keepdims=True)
    return xc * lax.rsqrt(var + LN_EPS)


def _proj_kernel(x_ref, sh_ref, sc_ref, w_ref, o_ref, s_ref):
    m = (_layer_norm(x_ref[0]) * (1.0 + sc_ref[0]) + sh_ref[0]).astype(BF16)
    step = 640
    for c0 in range(0, MAIN_W, step):
        o_ref[0, :, c0:c0 + step] = _mm(m, w_ref[:, c0:c0 + step]).astype(BF16)
    s_ref[0] = _mm(m, w_ref[:, MAIN_W:])


def _project(x, sh, sc, w_perm, per_batch):
    b_, l, _ = x.shape
    t = min(512, l)
    mod_map = (lambda b, i: (b, 0, 0)) if per_batch else (lambda b, i: (0, 0, 0))
    return pl.pallas_call(
        _proj_kernel,
        grid=(b_, l // t),
        in_specs=[pl.BlockSpec((1, t, D_MODEL), lambda b, i: (b, i, 0)),
                  pl.BlockSpec((1, 1, D_MODEL), mod_map),
                  pl.BlockSpec((1, 1, D_MODEL), mod_map),
                  pl.BlockSpec((D_MODEL, PROJ_W), lambda b, i: (0, 0))],
        out_specs=[pl.BlockSpec((1, t, MAIN_W), lambda b, i: (b, i, 0)),
                   pl.BlockSpec((1, t, SMALL_W), lambda b, i: (b, i, 0))],
        out_shape=[jax.ShapeDtypeStruct((b_, l, MAIN_W), BF16),
                   jax.ShapeDtypeStruct((b_, l, SMALL_W), F32)],
        compiler_params=pltpu.CompilerParams(vmem_limit_bytes=VMEM_LIMIT),
        name="ln_mod_proj",
    )(x, sh, sc, w_perm)


def _out_kernel(alpha, x_ref, ya_ref, yb_ref, yc_ref, w_ref, gt_ref, g_ref, b_ref, o_ref):
    y = jnp.concatenate([ya_ref[0], yb_ref[0], yc_ref[0]], axis=-1)
    h = alpha * x_ref[0] + gt_ref[0] * _mm(y, w_ref[...])
    o_ref[0] = _layer_norm(h) * g_ref[...] + b_ref[...]


def _out_project(x, ya, yb, yc, w_out, gt, ln_g, ln_b, per_batch, alpha):
    b_, l, _ = x.shape
    t = min(512, l)
    mod_map = (lambda b, i: (b, 0, 0)) if per_batch else (lambda b, i: (0, 0, 0))
    tok = lambda w: pl.BlockSpec((1, t, w), lambda b, i: (b, i, 0))
    return pl.pallas_call(
        functools.partial(_out_kernel, alpha),
        grid=(b_, l // t),
        in_specs=[tok(D_MODEL), tok(A_W), tok(B_W), tok(C_VW),
                  pl.BlockSpec((D_MODEL, D_MODEL), lambda b, i: (0, 0)),
                  pl.BlockSpec((1, 1, D_MODEL), mod_map),
                  pl.BlockSpec((1, D_MODEL), lambda b, i: (0, 0)),
                  pl.BlockSpec((1, D_MODEL), lambda b, i: (0, 0))],
        out_specs=tok(D_MODEL),
        out_shape=jax.ShapeDtypeStruct((b_, l, D_MODEL), F32),
        compiler_params=pltpu.CompilerParams(vmem_limit_bytes=VMEM_LIMIT),
        name="out_proj_postnorm",
    )(x, ya, yb, yc, w_out, gt, ln_g.reshape(1, -1), ln_b.reshape(1, -1))


NA_ROWS_PER_STEP = 4


def _stack_heads(q):
    head = _iota2(q.shape, 1) // B_DH
    return jnp.concatenate([jnp.where(head == h, q, jnp.zeros_like(q)) for h in range(B_HEADS)], axis=0)


def _unstack_heads(o, n):
    head = _iota2((n, B_W), 1) // B_DH
    out = jnp.zeros((n, B_W), F32)
    for h in range(B_HEADS):
        out = jnp.where(head == h, o[h * n:(h + 1) * n], out)
    return out


def _na_kernel(rows, q_ref, k_ref, v_ref, kc_ref, vc_ref, z_ref, bias_ref, o_ref):
    kc = kc_ref[0]
    vc = vc_ref[0]
    n_win = WIN_R * GRID_W
    for rr in range(NA_ROWS_PER_STEP):
        r = pl.program_id(1) * NA_ROWS_PER_STEP + rr
        r0 = jnp.clip(r - WIN_R // 2, 0, rows - WIN_R)
        start = pl.multiple_of(r0 * GRID_W, GRID_W)
        kw = k_ref[0, pl.ds(start, n_win), :]
        vw = v_ref[0, pl.ds(start, n_win), :]
        tok = slice(rr * GRID_W, (rr + 1) * GRID_W)
        qs = _stack_heads(q_ref[0, tok, :] * (B_DH ** -0.5))
        s_win = _nt(qs, kw) + bias_ref[r - r0]
        s_ctx = _nt(qs, kc)
        m = jnp.maximum(jnp.max(s_win, -1, keepdims=True), jnp.max(s_ctx, -1, keepdims=True))
        p_win = jnp.exp(s_win - m)
        p_ctx = jnp.exp(s_ctx - m)
        den = jnp.sum(p_win, -1, keepdims=True) + jnp.sum(p_ctx, -1, keepdims=True)
        o = (_mm(p_win.astype(BF16), vw) + _mm(p_ctx.astype(BF16), vc)) / den
        o = _unstack_heads(o, GRID_W)
        o_ref[0, tok, :] = (o * _silu(z_ref[0, tok, :].astype(F32))).astype(BF16)


def _na_bias_table(rpb):
    cq = np.arange(GRID_W)
    c0 = np.clip(cq - WIN_C // 2, 0, GRID_W - WIN_C)
    col_ok = (cq[None, :] >= c0[:, None]) & (cq[None, :] < c0[:, None] + WIN_C)
    dj = np.clip(cq[None, :] - cq[:, None] + (WIN_C - 1), 0, 2 * WIN_C - 2)
    onehot = (dj[None] == np.arange(2 * WIN_C - 1)[:, None, None]).astype(np.float32)
    t = jnp.einsum('hdk,kqc->hdqc', rpb.astype(F32), jnp.asarray(onehot), precision=HI)
    t = jnp.where(col_ok[None, None], t, NEG_INF)
    slabs = [jnp.transpose(t[:, WIN_R - 1 - v:2 * WIN_R - 1 - v], (0, 2, 1, 3)) for v in range(WIN_R)]
    return jnp.stack(slabs).reshape(WIN_R, B_HEADS * GRID_W, WIN_R * GRID_W)


def _na_latent(main_l, main_c, bias):
    b_, s, _ = main_l.shape
    lc = main_c.shape[1]
    rows = s // GRID_W
    assert rows >= WIN_R and rows % NA_ROWS_PER_STEP == 0
    tq = NA_ROWS_PER_STEP * GRID_W
    col = lambda idx: (lambda b, i: (b, 0, idx))
    return pl.pallas_call(
        functools.partial(_na_kernel, rows),
        grid=(b_, rows // NA_ROWS_PER_STEP),
        in_specs=[pl.BlockSpec((1, tq, B_W), lambda b, i: (b, i, 8)),
                  pl.BlockSpec((1, s, B_W), col(9)),
                  pl.BlockSpec((1, s, B_W), col(10)),
                  pl.BlockSpec((1, lc, B_W), col(9)),
                  pl.BlockSpec((1, lc, B_W), col(10)),
                  pl.BlockSpec((1, tq, B_W), lambda b, i: (b, i, 11)),
                  pl.BlockSpec(bias.shape, lambda b, i: (0, 0, 0))],
        out_specs=pl.BlockSpec((1, tq, B_W), lambda b, i: (b, i, 0)),
        out_shape=jax.ShapeDtypeStruct((b_, s, B_W), BF16),
        compiler_params=pltpu.CompilerParams(vmem_limit_bytes=VMEM_LIMIT),
        name="na_latent",
    )(main_l, main_l, main_l, main_c, main_c, main_l, bias)


def _na_ctx_kernel(q_ref, k_ref, v_ref, z_ref, o_ref):
    lc = q_ref.shape[1]
    qs = _stack_heads(q_ref[0] * (B_DH ** -0.5))
    s = _nt(qs, k_ref[0])
    p = jnp.exp(s - jnp.max(s, -1, keepdims=True))
    o = _mm(p.astype(BF16), v_ref[0]) / jnp.sum(p, -1, keepdims=True)
    o = _unstack_heads(o, lc)
    o_ref[0] = (o * _silu(z_ref[0].astype(F32))).astype(BF16)


def _na_ctx(main_c):
    b_, lc, _ = main_c.shape
    col = lambda idx: pl.BlockSpec((1, lc, B_W), lambda b: (b, 0, idx))
    return pl.pallas_call(
        _na_ctx_kernel,
        grid=(b_,),
        in_specs=[col(8), col(9), col(10), col(11)],
        out_specs=pl.BlockSpec((1, lc, B_W), lambda b: (b, 0, 0)),
        out_shape=jax.ShapeDtypeStruct((b_, lc, B_W), BF16),
        name="na_ctx",
    )(main_c, main_c, main_c, main_c)


GLA_TILE = 128
GLA_INTRA_CHUNKS = 2
GLA_SCAN_STEPS = 8


def _bwd_chunk(n, nc_c, nc_l):
    return jnp.where(n < nc_c, nc_c - 1 - n, nc_c + nc_l - 1 - (n - nc_c))


def _gla_kernel(lc, ll,
                qc_ref, kc_ref, vc_ref, rc_ref, zc_ref, ql_ref, kl_ref, vl_ref, rl_ref, zl_ref,
                cos_ref, sin_ref, w2f_ref, w2b_ref, b2_ref, nw_ref, oc_ref, ol_ref,
                q_s, k_s, v_s, bf_s, bb_s, qdf_s, kdf_s, qdb_s, kdb_s, glf_s, glb_s, o_s, st_s):
    c = C_CHUNK
    nc_c, nc_l = lc // c, ll // c
    per_tile = GLA_TILE // c
    scale = C_DK ** -0.5

    ti = _iota2((GLA_TILE, GLA_TILE), 0)
    tj = _iota2((GLA_TILE, GLA_TILE), 1)
    same = (ti // c) == (tj // c)
    tri_l = jnp.where(same & (tj <= ti), 1.0, 0.0).astype(BF16)
    tri_u = jnp.where(same & (tj >= ti), 1.0, 0.0).astype(BF16)
    rot = (jnp.where((tj % 16 < 8) & (ti == tj + 8), -1.0, 0.0)
           + jnp.where((tj % 16 >= 8) & (ti == tj - 8), 1.0, 0.0)).astype(BF16)
    expand = jnp.where(_iota2((C_QW, C_VW), 0) // C_DK == _iota2((C_QW, C_VW), 1) // C_DV,
                       1.0, 0.0).astype(BF16)
    st_mask = jnp.where(_iota2((C_VW, C_QW), 0) // C_DV == _iota2((C_VW, C_QW), 1) // C_DK,
                        1.0, 0.0).astype(F32)
    head_ones = jnp.where(_iota2((C_VW, C_VW), 0) // C_DV == _iota2((C_VW, C_VW), 1) // C_DV,
                          1.0, 0.0).astype(BF16)

    def log_sigmoid(x):
        return jnp.minimum(x, 0.0) - jnp.log1p(jnp.exp(-jnp.abs(x)))

    def prologue(q_ref, k_ref, v_ref, r_ref, base, n_tiles, rope):
        def body(t, carry):
            off = pl.multiple_of(t * GLA_TILE, GLA_TILE)
            uoff = pl.multiple_of(base + t * GLA_TILE, GLA_TILE)
            qb = q_ref[0, pl.ds(off, GLA_TILE), :]
            kb = k_ref[0, pl.ds(off, GLA_TILE), :]
            q = qb.astype(F32)
            k = kb.astype(F32)
            if rope:
                cs = cos_ref[pl.ds(off, GLA_TILE), :]
                sn = sin_ref[pl.ds(off, GLA_TILE), :]
                q = q * cs + _mm(qb, rot) * sn
                k = k * cs + _mm(kb, rot) * sn
            q = q * scale
            r = r_ref[0, pl.ds(off, GLA_TILE), :].astype(BF16)
            gkf = log_sigmoid(_mm(r, w2f_ref[...]) + b2_ref[0:1, :]) / C_GATE_NORM
            gkb = log_sigmoid(_mm(r, w2b_ref[...]) + b2_ref[1:2, :]) / C_GATE_NORM
            bf = _mm_01(tri_l, gkf)
            bb = _mm_01(tri_u, gkb)
            tf = jnp.concatenate([jnp.broadcast_to(bf[i * c + c - 1:i * c + c], (c, C_QW)) for i in range(per_tile)], axis=0)
            tb = jnp.concatenate([jnp.broadcast_to(bb[i * c:i * c + 1], (c, C_QW)) for i in range(per_tile)], axis=0)
            sl = pl.ds(uoff, GLA_TILE)
            q_s[sl, :] = q.astype(BF16)
            k_s[sl, :] = k.astype(BF16)
            v_s[sl, :] = v_ref[0, pl.ds(off, GLA_TILE), :]
            bf_s[sl, :] = bf
            bb_s[sl, :] = bb
            qdf_s[sl, :] = (q * jnp.exp(bf)).astype(BF16)
            kdf_s[sl, :] = (k * jnp.exp(tf - bf)).astype(BF16)
            qdb_s[sl, :] = (q * jnp.exp(bb)).astype(BF16)
            kdb_s[sl, :] = (k * jnp.exp(tb - bb)).astype(BF16)
            goff = pl.multiple_of((base // c + t * per_tile) * 8, 8)
            glf_s[pl.ds(goff, per_tile * 8), :] = jnp.exp(
                jnp.concatenate([tf[i * c:i * c + 8] for i in range(per_tile)], axis=0))
            glb_s[pl.ds(goff, per_tile * 8), :] = jnp.exp(
                jnp.concatenate([tb[i * c:i * c + 8] for i in range(per_tile)], axis=0))
            return carry
        lax.fori_loop(0, n_tiles, body, 0, unroll=2)

    prologue(qc_ref, kc_ref, vc_ref, rc_ref, 0, lc // GLA_TILE, False)
    prologue(ql_ref, kl_ref, vl_ref, rl_ref, lc, ll // GLA_TILE, True)

    jj = _iota2((c, C_QW), 0)

    def intra(first_chunk):
        offs = [pl.multiple_of((first_chunk + j) * c, c) for j in range(GLA_INTRA_CHUNKS)]
        ps = []
        for off in offs:
            q = q_s[pl.ds(off, c), :].astype(F32)
            k = k_s[pl.ds(off, c), :].astype(F32)
            bf = bf_s[pl.ds(off, c), :]
            bb = bb_s[pl.ds(off, c), :]
            rows = []
            for i in range(c):
                e = jnp.exp(jnp.where(jj < i, bf[i:i + 1] - bf, bb[i:i + 1] - bb))
                rows.append((jnp.where(jj == i, 2.0, e) * (q[i:i + 1] * k)).astype(BF16))
            ps.append(jnp.concatenate(rows, axis=0))
        accs = [_mm(p, expand) for p in ps]
        for off, a in zip(offs, accs):
            v = v_s[pl.ds(off, c), :].astype(F32)
            o_s[pl.ds(off, c), :] += jnp.sum(a.reshape(c, c, C_VW) * v[None], axis=1)

    o_s[...] = jnp.zeros_like(o_s)
    st_s[...] = jnp.zeros_like(st_s)
    dirs = ((qdf_s, kdf_s, glf_s), (qdb_s, kdb_s, glb_s))

    def scan(it, carry):
        steps = [it * GLA_SCAN_STEPS + s for s in range(GLA_SCAN_STEPS)]
        chunk = [steps, [_bwd_chunk(n, nc_c, nc_l) for n in steps]]
        upd = [[_tn(v_s[pl.ds(pl.multiple_of(ci * c, c), c), :], dirs[d][1][pl.ds(pl.multiple_of(ci * c, c), c), :])
                for ci in chunk[d]] for d in range(2)]
        st = [st_s[0], st_s[1]]
        for s in range(GLA_SCAN_STEPS):
            if s % GLA_INTRA_CHUNKS == 0:
                intra(steps[s])
            for d in range(2):
                ci = chunk[d][s]
                sl = pl.ds(pl.multiple_of(ci * c, c), c)
                o_s[sl, :] += _nt(dirs[d][0][sl, :], st[d].astype(BF16))
                gl = dirs[d][2][pl.ds(pl.multiple_of(ci * 8, 8), 8), :][0:1]
                st[d] = st[d] * gl + upd[d][s] * st_mask
        st_s[0] = st[0]
        st_s[1] = st[1]
        return carry
    lax.fori_loop(0, (nc_c + nc_l) // GLA_SCAN_STEPS, scan, 0)

    def epilogue(z_ref, o_ref, base, n_tiles):
        def body(t, carry):
            off = pl.multiple_of(t * GLA_TILE, GLA_TILE)
            o = o_s[pl.ds(pl.multiple_of(base + t * GLA_TILE, GLA_TILE), GLA_TILE), :]
            sq = o * o
            hi = sq.astype(BF16)
            lo = (sq - hi.astype(F32)).astype(BF16)
            ms = (_mm(hi, head_ones) + _mm(lo, head_ones)) * (1.0 / C_DV)
            o = o * lax.rsqrt(ms + 1e-6) * nw_ref[...]
            z = z_ref[0, pl.ds(off, GLA_TILE), :].astype(F32)
            o_ref[0, pl.ds(off, GLA_TILE), :] = (o * _silu(z)).astype(BF16)
            return carry
        lax.fori_loop(0, n_tiles, body, 0, unroll=2)

    epilogue(zc_ref, oc_ref, 0, lc // GLA_TILE)
    epilogue(zl_ref, ol_ref, lc, ll // GLA_TILE)


def _rope_tables(s):
    t = np.arange(s)
    nf = C_DK // 4
    inv = ROPE_THETA ** (-np.arange(nf, dtype=np.float64) / nf)
    lane = np.arange(C_DK)
    pos = np.where((lane // (C_DK // 2))[None, :] == 0, (t // GRID_W)[:, None], (t % GRID_W)[:, None])
    ang = pos.astype(np.float32) * inv.astype(np.float32)[lane % nf][None, :]
    cos = np.tile(np.cos(ang).astype(np.float32), (1, C_HEADS))
    sin = np.tile(np.sin(ang).astype(np.float32), (1, C_HEADS))
    return jnp.asarray(cos), jnp.asarray(sin)


def _gla(main_c, small_c, main_l, small_l, w2, b2, norm_w):
    b_, lc, _ = main_c.shape
    ll = main_l.shape[1]
    lt = lc + ll
    assert lc % GLA_TILE == 0 and ll % GLA_TILE == 0
    cos, sin = _rope_tables(ll)
    w2f = jnp.zeros((SMALL_W, C_QW), BF16).at[16:16 + C_RANK].set(w2[0].astype(BF16))
    w2b = jnp.zeros((SMALL_W, C_QW), BF16).at[16 + C_RANK:16 + 2 * C_RANK].set(w2[1].astype(BF16))
    nw = jnp.tile(norm_w.astype(F32), C_HEADS).reshape(1, C_VW)
    nchunks = lt // C_CHUNK

    def seg(l, w, idx):
        return pl.BlockSpec((1, l, w), lambda b: (b, 0, idx))

    def full(shape):
        return pl.BlockSpec(shape, lambda b: (0,) * len(shape))

    return pl.pallas_call(
        functools.partial(_gla_kernel, lc, ll),
        grid=(b_,),
        in_specs=[seg(lc, C_QW, 24), seg(lc, C_QW, 25), seg(lc, C_VW, 13), seg(lc, SMALL_W, 0), seg(lc, C_VW, 14),
                  seg(ll, C_QW, 24), seg(ll, C_QW, 25), seg(ll, C_VW, 13), seg(ll, SMALL_W, 0), seg(ll, C_VW, 14),
                  full((ll, C_QW)), full((ll, C_QW)), full((SMALL_W, C_QW)), full((SMALL_W, C_QW)),
                  full((2, C_QW)), full((1, C_VW))],
        out_specs=[seg(lc, C_VW, 0), seg(ll, C_VW, 0)],
        out_shape=[jax.ShapeDtypeStruct((b_, lc, C_VW), BF16), jax.ShapeDtypeStruct((b_, ll, C_VW), BF16)],
        scratch_shapes=[pltpu.VMEM((lt, C_QW), BF16), pltpu.VMEM((lt, C_QW), BF16), pltpu.VMEM((lt, C_VW), BF16),
                        pltpu.VMEM((lt, C_QW), F32), pltpu.VMEM((lt, C_QW), F32),
                        pltpu.VMEM((lt, C_QW), BF16), pltpu.VMEM((lt, C_QW), BF16),
                        pltpu.VMEM((lt, C_QW), BF16), pltpu.VMEM((lt, C_QW), BF16),
                        pltpu.VMEM((nchunks * 8, C_QW), F32), pltpu.VMEM((nchunks * 8, C_QW), F32),
                        pltpu.VMEM((lt, C_VW), F32), pltpu.VMEM((2, C_VW, C_QW), F32)],
        compiler_params=pltpu.CompilerParams(vmem_limit_bytes=VMEM_LIMIT),
        name="gla",
    )(main_c, main_c, main_c, small_c, main_c, main_l, main_l, main_l, small_l, main_l,
      cos, sin, w2f, w2b, b2.astype(F32), nw)


GDN_CHUNK = 128
GDN_BASE = 16
GDN_LEVELS = 3
M_INCL, M_STRICT, M_BASE, M_LEVEL, M_EYE = 0, 2, 4, 5, 8
N_MASKS = 9
GDN_PRE_CHUNKS = 4
GDN_REC_FIRST_STAGE = 2
CONV_HALO = 8


def _gdn_kernel(lc, ll, n_seq,
                xqc_ref, xkc_ref, xvc_ref, sc_ref, zc_ref, xql_ref, xkl_ref, xvl_ref, sl_ref, zl_ref,
                cwq_ref, cwk_ref, cwv_ref, alog_ref, dtb_ref, nw_ref, oc_ref, ol_ref,
                pad_s, q_s, k_s, v_s, gf_s, gb_s, bef_s, beb_s, u_s, w_s, qd_s, kdt_s, qk_s, gl_s, msk_s, o_s, st_s):
    c = GDN_CHUNK
    nc_c, nc_l = lc // c, ll // c
    step = pl.program_id(0)
    h = jnp.minimum(step, n_seq - 1) % A_HEADS
    cur = step % 2
    prev = 1 - cur

    @pl.when(step == 0)
    def _():
        for buf in (u_s, w_s, qd_s, kdt_s, qk_s, gl_s):
            buf[1] = jnp.zeros(buf.shape[1:], buf.dtype)

    def conv_seg(x_ref, w_ref, dst, base, l, mode):
        t = min(256, l)
        pad_s[0:CONV_HALO, :] = jnp.zeros((CONV_HALO, LANES), F32)
        pad_s[CONV_HALO + l:2 * CONV_HALO + l, :] = jnp.zeros((CONV_HALO, LANES), F32)

        def fill(i, carry):
            off = pl.multiple_of(i * t, t)
            pad_s[pl.ds(pl.multiple_of(CONV_HALO + i * t, 8), t), :] = x_ref[0, pl.ds(off, t), :].astype(F32)
            return carry
        lax.fori_loop(0, l // t, fill, 0)
        w = w_ref[...]

        def body(i, carry):
            acc = jnp.zeros((t, LANES), F32)
            for j in range(CONV_K):
                tap = pad_s[pl.ds(i * t + (CONV_HALO - CONV_K // 2 + j), t), :]
                acc = acc + tap * w[j:j + 1]
            y = _silu(acc)
            if mode != "v":
                y = y * lax.rsqrt(jnp.sum(y * y, -1, keepdims=True) + 1e-6)
            if mode == "q":
                y = y * (A_DK ** -0.5)
            dst[pl.ds(pl.multiple_of(base + i * t, t), t), :] = y.astype(dst.dtype)
            return carry
        lax.fori_loop(0, l // t, body, 0)

    for x_ref, w_ref, dst, mode in ((xqc_ref, cwq_ref, q_s, "q"), (xkc_ref, cwk_ref, k_s, "k"),
                                    (xvc_ref, cwv_ref, v_s, "v")):
        conv_seg(x_ref, w_ref, dst, 0, lc, mode)
    for x_ref, w_ref, dst, mode in ((xql_ref, cwq_ref, q_s, "q"), (xkl_ref, cwk_ref, k_s, "k"),
                                    (xvl_ref, cwv_ref, v_s, "v")):
        conv_seg(x_ref, w_ref, dst, lc, ll, mode)

    ti = _iota2((c, c), 0)
    tj = _iota2((c, c), 1)
    as_f32 = lambda m: jnp.where(m, 1.0, 0.0).astype(F32)
    tri_l = as_f32(tj <= ti)
    tri_u = as_f32(tj >= ti)
    msk_s[M_INCL] = tri_l
    msk_s[M_INCL + 1] = tri_u
    msk_s[M_STRICT] = as_f32(tj < ti)
    msk_s[M_STRICT + 1] = as_f32(tj > ti)
    msk_s[M_BASE] = as_f32(ti // GDN_BASE == tj // GDN_BASE)
    for lv in range(GDN_LEVELS):
        s = GDN_BASE << lv
        msk_s[M_LEVEL + lv] = as_f32((ti // (2 * s) == tj // (2 * s)) & (ti // s != tj // s))
    msk_s[M_EYE] = as_f32(ti == tj)
    neg_a = -jnp.exp(alog_ref[...])
    lane1 = _iota2((c, LANES), 1)

    def lane_bcast(g, idx):
        col = jnp.sum(jnp.where(lane1 == idx, g, 0.0), axis=-1, keepdims=True)
        return jnp.broadcast_to(col, (c, LANES))

    def gates(s_ref, base, n_tiles):
        def body(t, carry):
            off = pl.multiple_of(t * c, c)
            x = s_ref[0, pl.ds(off, c), :]
            g = jnp.where(lane1 < 2 * A_HEADS, neg_a * _softplus(x + dtb_ref[...]), _sigmoid(x))
            sl = pl.ds(pl.multiple_of(base + t * c, c), c)
            gf_s[sl, :] = _mm_01(tri_l.astype(BF16), lane_bcast(g, h))
            gb_s[sl, :] = _mm_01(tri_u.astype(BF16), lane_bcast(g, h + A_HEADS))
            bef_s[sl, :] = lane_bcast(g, h + 2 * A_HEADS).astype(BF16)
            beb_s[sl, :] = lane_bcast(g, h + 3 * A_HEADS).astype(BF16)
            return carry
        lax.fori_loop(0, n_tiles, body, 0, unroll=2)

    gates(sc_ref, 0, nc_c)
    gates(sl_ref, lc, nc_l)

    def prepass(first_chunk, per_step, it):
        chains = []
        for j in range(per_step):
            ci = first_chunk + it * per_step + j
            sl = pl.ds(pl.multiple_of(ci * c, c), c)
            kbf = k_s[sl, :]
            q = q_s[sl, :].astype(F32)
            k = kbf.astype(F32)
            gram = _nt(kbf, kbf)
            qk_raw = _nt(q_s[sl, :], kbf)
            for d, (g_s, be_s) in enumerate(((gf_s, bef_s), (gb_s, beb_s))):
                chains.append(dict(ci=ci, sl=sl, d=d, q=q, k=k, gram=gram, qk_raw=qk_raw,
                                   gc=g_s[sl, :], beta=be_s[sl, :].astype(F32)))
        yield
        for ch in chains:
            gc = ch["gc"]
            ch["decay"] = jnp.exp(jnp.minimum(gc - gc.T, 0.0)) * msk_s[M_INCL + ch["d"]]
            ch["a"] = ch["gram"] * ch["beta"] * ch["decay"] * msk_s[M_STRICT + ch["d"]]
            n0 = -(ch["a"] * msk_s[M_BASE])
            ch["inv"] = msk_s[M_EYE] + n0
            ch["pw"] = n0.astype(BF16)
        yield
        n_sq = GDN_BASE.bit_length() - 2
        for ch in chains:
            ch["pw"] = _mm(ch["pw"], ch["pw"]).astype(BF16)
        yield
        for m in range(n_sq):
            for ch in chains:
                ch["inv_next"] = ch["inv"] + _mm(ch["inv"].astype(BF16), ch["pw"])
            if m + 1 < n_sq:
                for ch in chains:
                    ch["pw"] = _mm(ch["pw"], ch["pw"]).astype(BF16)
            for ch in chains:
                ch["inv"] = ch["inv_next"]
            yield
        for lv in range(GDN_LEVELS):
            for ch in chains:
                ch["invb"] = ch["inv"].astype(BF16)
                ch["t"] = _mm((ch["a"] * msk_s[M_LEVEL + lv]).astype(BF16), ch["invb"]).astype(BF16)
            yield
            for ch in chains:
                ch["inv"] = ch["inv"] - _mm(ch["invb"], ch["t"])
            yield
        for ch in chains:
            gc, beta, q, k, d, sl = ch["gc"], ch["beta"], ch["q"], ch["k"], ch["d"], ch["sl"]
            eg = jnp.exp(gc)
            rhs = jnp.concatenate([v_s[sl, :].astype(F32) * beta, k * beta * eg], axis=-1).astype(BF16)
            sol = _mm(ch["inv"].astype(BF16), rhs)
            g_last = gc[c - 1:c] if d == 0 else gc[0:1]
            u_s[cur, d, sl, :] = sol[:, :A_DV].astype(BF16)
            w_s[cur, d, sl, :] = sol[:, A_DV:].astype(BF16)
            qk_s[cur, d, sl, :] = (ch["qk_raw"] * ch["decay"]).astype(BF16)
            qd_s[cur, d, sl, :] = (q * eg).astype(BF16)
            kdt_s[cur, d, sl, :] = (k * jnp.exp(g_last - gc)).T.astype(BF16)
            gl_s[cur, d, pl.ds(pl.multiple_of(ch["ci"] * 8, 8), 8), :] = jnp.exp(
                jnp.broadcast_to(g_last, (8, LANES)))

    def recurrence(first_step, n_steps, it):
        st = [st_s[0], st_s[1]]
        for j in range(n_steps):
            n = first_step + it * n_steps + j
            cis = (n, _bwd_chunk(n, nc_c, nc_l))
            sls = [pl.ds(pl.multiple_of(ci * c, c), c) for ci in cis]
            stb = [x.astype(BF16) for x in st]
            w_st = [_mm(w_s[prev, d, sls[d], :], stb[d]) for d in range(2)]
            o_st = [_mm(qd_s[prev, d, sls[d], :], stb[d]) for d in range(2)]
            yield
            vnb = [(u_s[prev, d, sls[d], :].astype(F32) - w_st[d]).astype(BF16) for d in range(2)]
            for d in range(2):
                gl = gl_s[prev, d, pl.ds(pl.multiple_of(cis[d] * 8, 8), 8), :][0:1]
                st[d] = st[d] * gl + _mm(kdt_s[prev, d, sls[d], :], vnb[d])
            for d in range(2):
                o_s[sls[d], :] += o_st[d] + _mm(qk_s[prev, d, sls[d], :], vnb[d])
            yield
        st_s[0] = st[0]
        st_s[1] = st[1]

    o_s[...] = jnp.zeros_like(o_s)
    st_s[...] = jnp.zeros_like(st_s)

    def merged(first, per_step, it, carry):
        pre = prepass(first, per_step, it)
        rec = recurrence(first, per_step, it)
        for i, _ in enumerate(pre):
            if i >= GDN_REC_FIRST_STAGE:
                next(rec, None)
        for _ in rec:
            pass
        return carry
    for first, count in ((0, nc_c), (nc_c, nc_l)):
        per_step = math.gcd(count, GDN_PRE_CHUNKS)
        lax.fori_loop(0, count // per_step, functools.partial(merged, first, per_step), 0)

    def epilogue(z_ref, o_ref, base, n_tiles):
        def body(t, carry):
            off = pl.multiple_of(t * GDN_CHUNK, GDN_CHUNK)
            o = o_s[pl.ds(pl.multiple_of(base + t * GDN_CHUNK, GDN_CHUNK), GDN_CHUNK), :]
            o = o * lax.rsqrt(jnp.mean(o * o, -1, keepdims=True) + 1e-6) * nw_ref[...]
            z = z_ref[0, pl.ds(off, GDN_CHUNK), :].astype(F32)
            o_ref[0, pl.ds(off, GDN_CHUNK), :] = (o * _silu(z)).astype(BF16)
            return carry
        lax.fori_loop(0, n_tiles, body, 0, unroll=2)

    epilogue(zc_ref, oc_ref, 0, lc // GDN_CHUNK)
    epilogue(zl_ref, ol_ref, lc, ll // GDN_CHUNK)


def _gdn(main_c, small_c, main_l, small_l, conv_w, a_log, dt_bias, norm_w):
    b_, lc, _ = main_c.shape
    ll = main_l.shape[1]
    lt = lc + ll
    assert lc % 256 == 0 and ll % 256 == 0
    pad8 = lambda p: jnp.zeros((1, LANES), F32).at[0, :2 * A_HEADS].set(p.astype(F32).reshape(-1))
    cw = conv_w.astype(F32)

    n_seq = b_ * A_HEADS
    pre_seq = lambda s: jnp.minimum(s, n_seq - 1)
    rec_seq = lambda s: jnp.maximum(s - 1, 0)

    def seg(l, w, part, seq):
        return pl.BlockSpec((1, l, w), lambda s: (seq(s) // A_HEADS, 0, part * A_HEADS + seq(s) % A_HEADS))

    def cw_spec(part):
        return pl.BlockSpec((CONV_K, A_DK), lambda s: (0, part * A_HEADS + pre_seq(s) % A_HEADS))

    const = lambda shape: pl.BlockSpec(shape, lambda s: (0,) * len(shape))
    in_specs = []
    for l in (lc, ll):
        in_specs += [seg(l, A_DK, 0, pre_seq), seg(l, A_DK, 1, pre_seq), seg(l, A_DK, 2, pre_seq),
                     pl.BlockSpec((1, l, SMALL_W), lambda s: (pre_seq(s) // A_HEADS, 0, 0)),
                     seg(l, A_DV, 3, rec_seq)]
    in_specs += [cw_spec(0), cw_spec(1), cw_spec(2), const((1, LANES)), const((1, LANES)), const((1, A_DV))]
    tok = lambda dt: pltpu.VMEM((lt, LANES), dt)
    halves = lambda: pltpu.VMEM((2, 2, lt, LANES), BF16)
    return pl.pallas_call(
        functools.partial(_gdn_kernel, lc, ll, n_seq),
        grid=(n_seq + 1,),
        in_specs=in_specs,
        out_specs=[seg(lc, A_DV, 0, rec_seq), seg(ll, A_DV, 0, rec_seq)],
        out_shape=[jax.ShapeDtypeStruct((b_, lc, A_W), BF16), jax.ShapeDtypeStruct((b_, ll, A_W), BF16)],
        scratch_shapes=[pltpu.VMEM((max(lc, ll) + 2 * CONV_HALO, LANES), F32),
                        tok(BF16), tok(BF16), tok(BF16), tok(F32), tok(F32), tok(BF16), tok(BF16),
                        halves(), halves(), halves(), halves(), halves(),
                        pltpu.VMEM((2, 2, (lt // GDN_CHUNK) * 8, LANES), F32),
                        pltpu.VMEM((N_MASKS, GDN_CHUNK, GDN_CHUNK), F32),
                        tok(F32), pltpu.VMEM((2, A_DK, A_DV), F32)],
        compiler_params=pltpu.CompilerParams(vmem_limit_bytes=GDN_VMEM_LIMIT,
                                             dimension_semantics=("arbitrary",)),
        name="gdn",
    )(main_c, main_c, main_c, small_c, main_c, main_l, main_l, main_l, small_l, main_l,
      cw, cw, cw, pad8(a_log), pad8(dt_bias), norm_w.astype(F32).reshape(1, A_DV))


def _permute_w_in(w):
    sizes = (3 * A_W, 4 * A_HEADS, A_W, B_W, B_W, B_W, B_W, C_QW, C_QW, C_VW, 2 * C_RANK, C_VW)
    offs = np.concatenate([[0], np.cumsum(sizes)])
    part = lambda i: w[:, offs[i]:offs[i + 1]]
    pad = jnp.zeros((w.shape[0], SMALL_W - 4 * A_HEADS - 2 * C_RANK), w.dtype)
    order = [part(0), part(2), part(3), part(4), part(5), part(6), part(7), part(8), part(9), part(11),
             part(1), part(10), pad]
    return jnp.concatenate(order, axis=1).astype(BF16)


def kernel(x, c, ctx, c_ctx, w_mod, b_mod, w_in, conv_w, a_log, dt_bias, gdn_norm, rpb,
           gla_w2, gla_b2, gla_norm, w_out, ln_g, ln_b):
    depth = w_mod.shape[0]
    b_ = x.shape[0]
    alpha = (2 * depth) ** 0.25
    n_mod = -(-(b_ + 1) // 8) * 8
    c_all = jnp.zeros((n_mod, D_MODEL), F32).at[:b_].set(c).at[b_].set(c_ctx)
    xl, xc = x, ctx
    for i in range(depth):
        ctx_out = i < depth - 1
        mod = _modulation(c_all, w_mod[i], b_mod[i]).reshape(n_mod, 3, 1, D_MODEL)
        sh_l, sc_l, gt_l = mod[:b_, 0], mod[:b_, 1], mod[:b_, 2]
        sh_c, sc_c, gt_c = mod[b_:b_ + 1, 0], mod[b_:b_ + 1, 1], mod[b_:b_ + 1, 2]
        w_perm = _permute_w_in(w_in[i])
        main_l, small_l = _project(xl, sh_l, sc_l, w_perm, True)
        main_c, small_c = _project(xc, sh_c, sc_c, w_perm, False)
        ya_c, ya_l = _gdn(main_c, small_c, main_l, small_l, conv_w[i], a_log[i], dt_bias[i], gdn_norm[i])
        yb_l = _na_latent(main_l, main_c, _na_bias_table(rpb[i]))
        yc_c, yc_l = _gla(main_c, small_c, main_l, small_l, gla_w2[i], gla_b2[i], gla_norm[i])
        w_o = w_out[i].astype(BF16)
        if ctx_out:
            yb_c = _na_ctx(main_c)
            xc = _out_project(xc, ya_c, yb_c, yc_c, w_o, gt_c, ln_g[i], ln_b[i], False, alpha)
        xl = _out_project(xl, ya_l, yb_l, yc_l, w_o, gt_l, ln_g[i], ln_b[i], True, alpha)
    return xl
```

```python
import functools
import math

import numpy as np
import jax
import jax.numpy as jnp
from jax import lax
from jax.experimental import pallas as pl
from jax.experimental.pallas import tpu as pltpu

F32 = jnp.float32
BF16 = jnp.bfloat16
HI = lax.Precision.HIGHEST

D_MODEL = 1024
GRID_W = 64
A_HEADS, A_DK, A_DV, CONV_K, A_CHUNK = 4, 128, 128, 5, 64
B_HEADS, B_DH, WIN_R, WIN_C = 4, 64, 8, 16
C_HEADS, C_DK, C_DV, C_RANK, C_CHUNK = 4, 32, 64, 16, 16
C_GATE_NORM = 16.0
ROPE_THETA = 10000.0
LN_EPS = 1e-6
NEG_INF = -1e30

A_W = A_HEADS * A_DK
B_W = B_HEADS * B_DH
C_QW = C_HEADS * C_DK
C_VW = C_HEADS * C_DV
MAIN_W = 3 * A_W + A_W + 4 * B_W + 2 * C_QW + 2 * C_VW
SMALL_W = 128
PROJ_W = MAIN_W + SMALL_W
LANES = 128
VMEM_LIMIT = 56 * 1024 * 1024
GDN_VMEM_LIMIT = 60 * 1024 * 1024


def _nt(a, b, precision=None):
    return lax.dot_general(a, b, (((1,), (1,)), ((), ())), precision=precision,
                           preferred_element_type=F32)


def _tn(a, b):
    return lax.dot_general(a, b, (((0,), (0,)), ((), ())), preferred_element_type=F32)


def _mm(a, b, precision=None):
    return jnp.dot(a, b, precision=precision, preferred_element_type=F32)


def _mm_01(m01, x):
    hi = x.astype(BF16)
    lo = (x - hi.astype(F32)).astype(BF16)
    return _mm(m01, hi) + _mm(m01, lo)


def _sigmoid(x):
    return 1.0 / (1.0 + jnp.exp(-x))


def _silu(x):
    return x * _sigmoid(x)


def _softplus(x):
    return jnp.maximum(x, 0.0) + jnp.log1p(jnp.exp(-jnp.abs(x)))


def _iota2(shape, axis):
    return lax.broadcasted_iota(jnp.int32, shape, axis)


def _aligned(x, m):
    return x if isinstance(x, int) else pl.multiple_of(x, m)


def _mod_kernel(c_ref, w_ref, b_ref, o_ref):
    o_ref[...] = _mm(_silu(c_ref[...]), w_ref[...], HI) + b_ref[...]


def _modulation(c_all, w_mod, b_mod):
    n = c_all.shape[0]
    tn = 768
    return pl.pallas_call(
        _mod_kernel,
        grid=(3 * D_MODEL // tn,),
        in_specs=[pl.BlockSpec((n, D_MODEL), lambda j: (0, 0)),
                  pl.BlockSpec((D_MODEL, tn), lambda j: (0, j)),
                  pl.BlockSpec((1, tn), lambda j: (0, j))],
        out_specs=pl.BlockSpec((n, tn), lambda j: (0, j)),
        out_shape=jax.ShapeDtypeStruct((n, 3 * D_MODEL), F32),
        name="adaln_mod",
    )(c_all, w_mod, b_mod.reshape(1, -1))


def _layer_norm(x):
    mu = jnp.mean(x, -1, keepdims=True)
    xc = x - mu
    var = jnp.mean(xc * xc, -1, keepdims=True)
    return xc * lax.rsqrt(var + LN_EPS)


def _proj_kernel(x_ref, sh_ref, sc_ref, w_ref, o_ref, s_ref):
    m = (_layer_norm(x_ref[0]) * (1.0 + sc_ref[0]) + sh_ref[0]).astype(BF16)
    step = 640
    for c0 in range(0, MAIN_W, step):
        o_ref[0, :, c0:c0 + step] = _mm(m, w_ref[:, c0:c0 + step]).astype(BF16)
    s_ref[0] = _mm(m, w_ref[:, MAIN_W:])


def _project(x, sh, sc, w_perm, per_batch):
    b_, l, _ = x.shape
    t = min(512, l)
    mod_map = (lambda b, i: (b, 0, 0)) if per_batch else (lambda b, i: (0, 0, 0))
    return pl.pallas_call(
        _proj_kernel,
        grid=(b_, l // t),
        in_specs=[pl.BlockSpec((1, t, D_MODEL), lambda b, i: (b, i, 0)),
                  pl.BlockSpec((1, 1, D_MODEL), mod_map),
                  pl.BlockSpec((1, 1, D_MODEL), mod_map),
                  pl.BlockSpec((D_MODEL, PROJ_W), lambda b, i: (0, 0))],
        out_specs=[pl.BlockSpec((1, t, MAIN_W), lambda b, i: (b, i, 0)),
                   pl.BlockSpec((1, t, SMALL_W), lambda b, i: (b, i, 0))],
        out_shape=[jax.ShapeDtypeStruct((b_, l, MAIN_W), BF16),
                   jax.ShapeDtypeStruct((b_, l, SMALL_W), F32)],
        compiler_params=pltpu.CompilerParams(vmem_limit_bytes=VMEM_LIMIT),
        name="ln_mod_proj",
    )(x, sh, sc, w_perm)


def _out_kernel(alpha, x_ref, ya_ref, yb_ref, yc_ref, w_ref, gt_ref, g_ref, b_ref, o_ref):
    y = jnp.concatenate([ya_ref[0], yb_ref[0], yc_ref[0]], axis=-1)
    h = alpha * x_ref[0] + gt_ref[0] * _mm(y, w_ref[...])
    o_ref[0] = _layer_norm(h) * g_ref[...] + b_ref[...]


def _out_project(x, ya, yb, yc, w_out, gt, ln_g, ln_b, per_batch, alpha):
    b_, l, _ = x.shape
    t = min(512, l)
    mod_map = (lambda b, i: (b, 0, 0)) if per_batch else (lambda b, i: (0, 0, 0))
    tok = lambda w: pl.BlockSpec((1, t, w), lambda b, i: (b, i, 0))
    return pl.pallas_call(
        functools.partial(_out_kernel, alpha),
        grid=(b_, l // t),
        in_specs=[tok(D_MODEL), tok(A_W), tok(B_W), tok(C_VW),
                  pl.BlockSpec((D_MODEL, D_MODEL), lambda b, i: (0, 0)),
                  pl.BlockSpec((1, 1, D_MODEL), mod_map),
                  pl.BlockSpec((1, D_MODEL), lambda b, i: (0, 0)),
                  pl.BlockSpec((1, D_MODEL), lambda b, i: (0, 0))],
        out_specs=tok(D_MODEL),
        out_shape=jax.ShapeDtypeStruct((b_, l, D_MODEL), F32),
        compiler_params=pltpu.CompilerParams(vmem_limit_bytes=VMEM_LIMIT),
        name="out_proj_postnorm",
    )(x, ya, yb, yc, w_out, gt, ln_g.reshape(1, -1), ln_b.reshape(1, -1))


NA_ROWS_PER_STEP = 4


def _stack_heads(q):
    head = _iota2(q.shape, 1) // B_DH
    return jnp.concatenate([jnp.where(head == h, q, jnp.zeros_like(q)) for h in range(B_HEADS)], axis=0)


def _unstack_heads(o, n):
    head = _iota2((n, B_W), 1) // B_DH
    out = jnp.zeros((n, B_W), F32)
    for h in range(B_HEADS):
        out = jnp.where(head == h, o[h * n:(h + 1) * n], out)
    return out


def _na_kernel(rows, q_ref, k_ref, v_ref, kc_ref, vc_ref, z_ref, bias_ref, o_ref):
    kc = kc_ref[0]
    vc = vc_ref[0]
    n_win = WIN_R * GRID_W
    for rr in range(NA_ROWS_PER_STEP):
        r = pl.program_id(1) * NA_ROWS_PER_STEP + rr
        r0 = jnp.clip(r - WIN_R // 2, 0, rows - WIN_R)
        start = pl.multiple_of(r0 * GRID_W, GRID_W)
        kw = k_ref[0, pl.ds(start, n_win), :]
        vw = v_ref[0, pl.ds(start, n_win), :]
        tok = slice(rr * GRID_W, (rr + 1) * GRID_W)
        qs = _stack_heads(q_ref[0, tok, :] * (B_DH ** -0.5))
        s_win = _nt(qs, kw) + bias_ref[r - r0]
        s_ctx = _nt(qs, kc)
        m = jnp.maximum(jnp.max(s_win, -1, keepdims=True), jnp.max(s_ctx, -1, keepdims=True))
        p_win = jnp.exp(s_win - m)
        p_ctx = jnp.exp(s_ctx - m)
        den = jnp.sum(p_win, -1, keepdims=True) + jnp.sum(p_ctx, -1, keepdims=True)
        o = (_mm(p_win.astype(BF16), vw) + _mm(p_ctx.astype(BF16), vc)) / den
        o = _unstack_heads(o, GRID_W)
        o_ref[0, tok, :] = (o * _silu(z_ref[0, tok, :].astype(F32))).astype(BF16)


def _na_bias_table(rpb):
    cq = np.arange(GRID_W)
    c0 = np.clip(cq - WIN_C // 2, 0, GRID_W - WIN_C)
    col_ok = (cq[None, :] >= c0[:, None]) & (cq[None, :] < c0[:, None] + WIN_C)
    dj = np.clip(cq[None, :] - cq[:, None] + (WIN_C - 1), 0, 2 * WIN_C - 2)
    onehot = (dj[None] == np.arange(2 * WIN_C - 1)[:, None, None]).astype(np.float32)
    t = jnp.einsum('hdk,kqc->hdqc', rpb.astype(F32), jnp.asarray(onehot), precision=HI)
    t = jnp.where(col_ok[None, None], t, NEG_INF)
    slabs = [jnp.transpose(t[:, WIN_R - 1 - v:2 * WIN_R - 1 - v], (0, 2, 1, 3)) for v in range(WIN_R)]
    return jnp.stack(slabs).reshape(WIN_R, B_HEADS * GRID_W, WIN_R * GRID_W)


def _na_latent(main_l, main_c, bias):
    b_, s, _ = main_l.shape
    lc = main_c.shape[1]
    rows = s // GRID_W
    assert rows >= WIN_R and rows % NA_ROWS_PER_STEP == 0
    tq = NA_ROWS_PER_STEP * GRID_W
    col = lambda idx: (lambda b, i: (b, 0, idx))
    return pl.pallas_call(
        functools.partial(_na_kernel, rows),
        grid=(b_, rows // NA_ROWS_PER_STEP),
        in_specs=[pl.BlockSpec((1, tq, B_W), lambda b, i: (b, i, 8)),
                  pl.BlockSpec((1, s, B_W), col(9)),
                  pl.BlockSpec((1, s, B_W), col(10)),
                  pl.BlockSpec((1, lc, B_W), col(9)),
                  pl.BlockSpec((1, lc, B_W), col(10)),
                  pl.BlockSpec((1, tq, B_W), lambda b, i: (b, i, 11)),
                  pl.BlockSpec(bias.shape, lambda b, i: (0, 0, 0))],
        out_specs=pl.BlockSpec((1, tq, B_W), lambda b, i: (b, i, 0)),
        out_shape=jax.ShapeDtypeStruct((b_, s, B_W), BF16),
        compiler_params=pltpu.CompilerParams(vmem_limit_bytes=VMEM_LIMIT),
        name="na_latent",
    )(main_l, main_l, main_l, main_c, main_c, main_l, bias)


def _na_ctx_kernel(q_ref, k_ref, v_ref, z_ref, o_ref):
    lc = q_ref.shape[1]
    qs = _stack_heads(q_ref[0] * (B_DH ** -0.5))
    s = _nt(qs, k_ref[0])
    p = jnp.exp(s - jnp.max(s, -1, keepdims=True))
    o = _mm(p.astype(BF16), v_ref[0]) / jnp.sum(p, -1, keepdims=True)
    o = _unstack_heads(o, lc)
    o_ref[0] = (o * _silu(z_ref[0].astype(F32))).astype(BF16)


def _na_ctx(main_c):
    b_, lc, _ = main_c.shape
    col = lambda idx: pl.BlockSpec((1, lc, B_W), lambda b: (b, 0, idx))
    return pl.pallas_call(
        _na_ctx_kernel,
        grid=(b_,),
        in_specs=[col(8), col(9), col(10), col(11)],
        out_specs=pl.BlockSpec((1, lc, B_W), lambda b: (b, 0, 0)),
        out_shape=jax.ShapeDtypeStruct((b_, lc, B_W), BF16),
        name="na_ctx",
    )(main_c, main_c, main_c, main_c)


GLA_TILE = 128
GLA_INTRA_CHUNKS = 2
GLA_SCAN_STEPS = 8


def _bwd_chunk(n, nc_c, nc_l):
    return jnp.where(n < nc_c, nc_c - 1 - n, nc_c + nc_l - 1 - (n - nc_c))


def _gla_kernel(lc, ll,
                qc_ref, kc_ref, vc_ref, rc_ref, zc_ref, ql_ref, kl_ref, vl_ref, rl_ref, zl_ref,
                cos_ref, sin_ref, w2f_ref, w2b_ref, b2_ref, nw_ref, oc_ref, ol_ref,
                q_s, k_s, v_s, bf_s, bb_s, qdf_s, kdf_s, qdb_s, kdb_s, glf_s, glb_s, o_s, st_s):
    c = C_CHUNK
    nc_c, nc_l = lc // c, ll // c
    per_tile = GLA_TILE // c
    scale = C_DK ** -0.5

    ti = _iota2((GLA_TILE, GLA_TILE), 0)
    tj = _iota2((GLA_TILE, GLA_TILE), 1)
    same = (ti // c) == (tj // c)
    tri_l = jnp.where(same & (tj <= ti), 1.0, 0.0).astype(BF16)
    tri_u = jnp.where(same & (tj >= ti), 1.0, 0.0).astype(BF16)
    rot = (jnp.where((tj % 16 < 8) & (ti == tj + 8), -1.0, 0.0)
           + jnp.where((tj % 16 >= 8) & (ti == tj - 8), 1.0, 0.0)).astype(BF16)
    expand = jnp.where(_iota2((C_QW, C_VW), 0) // C_DK == _iota2((C_QW, C_VW), 1) // C_DV,
                       1.0, 0.0).astype(BF16)
    st_mask = jnp.where(_iota2((C_VW, C_QW), 0) // C_DV == _iota2((C_VW, C_QW), 1) // C_DK,
                        1.0, 0.0).astype(F32)
    head_ones = jnp.where(_iota2((C_VW, C_VW), 0) // C_DV == _iota2((C_VW, C_VW), 1) // C_DV,
                          1.0, 0.0).astype(BF16)

    def log_sigmoid(x):
        return jnp.minimum(x, 0.0) - jnp.log1p(jnp.exp(-jnp.abs(x)))

    def prologue(q_ref, k_ref, v_ref, r_ref, base, n_tiles, rope):
        def body(t, carry):
            off = pl.multiple_of(t * GLA_TILE, GLA_TILE)
            uoff = pl.multiple_of(base + t * GLA_TILE, GLA_TILE)
            qb = q_ref[0, pl.ds(off, GLA_TILE), :]
            kb = k_ref[0, pl.ds(off, GLA_TILE), :]
            q = qb.astype(F32)
            k = kb.astype(F32)
            if rope:
                cs = cos_ref[pl.ds(off, GLA_TILE), :]
                sn = sin_ref[pl.ds(off, GLA_TILE), :]
                q = q * cs + _mm(qb, rot) * sn
                k = k * cs + _mm(kb, rot) * sn
            q = q * scale
            r = r_ref[0, pl.ds(off, GLA_TILE), :].astype(BF16)
            gkf = log_sigmoid(_mm(r, w2f_ref[...]) + b2_ref[0:1, :]) / C_GATE_NORM
            gkb = log_sigmoid(_mm(r, w2b_ref[...]) + b2_ref[1:2, :]) / C_GATE_NORM
            bf = _mm_01(tri_l, gkf)
            bb = _mm_01(tri_u, gkb)
            tf = jnp.concatenate([jnp.broadcast_to(bf[i * c + c - 1:i * c + c], (c, C_QW)) for i in range(per_tile)], axis=0)
            tb = jnp.concatenate([jnp.broadcast_to(bb[i * c:i * c + 1], (c, C_QW)) for i in range(per_tile)], axis=0)
            sl = pl.ds(uoff, GLA_TILE)
            q_s[sl, :] = q.astype(BF16)
            k_s[sl, :] = k.astype(BF16)
            v_s[sl, :] = v_ref[0, pl.ds(off, GLA_TILE), :]
            bf_s[sl, :] = bf
            bb_s[sl, :] = bb
            qdf_s[sl, :] = (q * jnp.exp(bf)).astype(BF16)
            kdf_s[sl, :] = (k * jnp.exp(tf - bf)).astype(BF16)
            qdb_s[sl, :] = (q * jnp.exp(bb)).astype(BF16)
            kdb_s[sl, :] = (k * jnp.exp(tb - bb)).astype(BF16)
            goff = pl.multiple_of((base // c + t * per_tile) * 8, 8)
            glf_s[pl.ds(goff, per_tile * 8), :] = jnp.exp(
                jnp.concatenate([tf[i * c:i * c + 8] for i in range(per_tile)], axis=0))
            glb_s[pl.ds(goff, per_tile * 8), :] = jnp.exp(
                jnp.concatenate([tb[i * c:i * c + 8] for i in range(per_tile)], axis=0))
            return carry
        lax.fori_loop(0, n_tiles, body, 0, unroll=2)

    prologue(qc_ref, kc_ref, vc_ref, rc_ref, 0, lc // GLA_TILE, False)
    prologue(ql_ref, kl_ref, vl_ref, rl_ref, lc, ll // GLA_TILE, True)

    jj = _iota2((c, C_QW), 0)

    def intra(first_chunk):
        offs = [pl.multiple_of((first_chunk + j) * c, c) for j in range(GLA_INTRA_CHUNKS)]
        ps = []
        for off in offs:
            q = q_s[pl.ds(off, c), :].astype(F32)
            k = k_s[pl.ds(off, c), :].astype(F32)
            bf = bf_s[pl.ds(off, c), :]
            bb = bb_s[pl.ds(off, c), :]
            rows = []
            for i in range(c):
                e = jnp.exp(jnp.where(jj < i, bf[i:i + 1] - bf, bb[i:i + 1] - bb))
                rows.append((jnp.where(jj == i, 2.0, e) * (q[i:i + 1] * k)).astype(BF16))
            ps.append(jnp.concatenate(rows, axis=0))
        accs = [_mm(p, expand) for p in ps]
        for off, a in zip(offs, accs):
            v = v_s[pl.ds(off, c), :].astype(F32)
            o_s[pl.ds(off, c), :] += jnp.sum(a.reshape(c, c, C_VW) * v[None], axis=1)

    o_s[...] = jnp.zeros_like(o_s)
    st_s[...] = jnp.zeros_like(st_s)
    dirs = ((qdf_s, kdf_s, glf_s), (qdb_s, kdb_s, glb_s))

    def scan(it, carry):
        steps = [it * GLA_SCAN_STEPS + s for s in range(GLA_SCAN_STEPS)]
        chunk = [steps, [_bwd_chunk(n, nc_c, nc_l) for n in steps]]
        upd = [[_tn(v_s[pl.ds(pl.multiple_of(ci * c, c), c), :], dirs[d][1][pl.ds(pl.multiple_of(ci * c, c), c), :])
                for ci in chunk[d]] for d in range(2)]
        st = [st_s[0], st_s[1]]
        for s in range(GLA_SCAN_STEPS):
            if s % GLA_INTRA_CHUNKS == 0:
                intra(steps[s])
            for d in range(2):
                ci = chunk[d][s]
                sl = pl.ds(pl.multiple_of(ci * c, c), c)
                o_s[sl, :] += _nt(dirs[d][0][sl, :], st[d].astype(BF16))
                gl = dirs[d][2][pl.ds(pl.multiple_of(ci * 8, 8), 8), :][0:1]
                st[d] = st[d] * gl + upd[d][s] * st_mask
        st_s[0] = st[0]
        st_s[1] = st[1]
        return carry
    lax.fori_loop(0, (nc_c + nc_l) // GLA_SCAN_STEPS, scan, 0)

    def epilogue(z_ref, o_ref, base, n_tiles):
        def body(t, carry):
            off = pl.multiple_of(t * GLA_TILE, GLA_TILE)
            o = o_s[pl.ds(pl.multiple_of(base + t * GLA_TILE, GLA_TILE), GLA_TILE), :]
            sq = o * o
            hi = sq.astype(BF16)
            lo = (sq - hi.astype(F32)).astype(BF16)
            ms = (_mm(hi, head_ones) + _mm(lo, head_ones)) * (1.0 / C_DV)
            o = o * lax.rsqrt(ms + 1e-6) * nw_ref[...]
            z = z_ref[0, pl.ds(off, GLA_TILE), :].astype(F32)
            o_ref[0, pl.ds(off, GLA_TILE), :] = (o * _silu(z)).astype(BF16)
            return carry
        lax.fori_loop(0, n_tiles, body, 0, unroll=2)

    epilogue(zc_ref, oc_ref, 0, lc // GLA_TILE)
    epilogue(zl_ref, ol_ref, lc, ll // GLA_TILE)


def _rope_tables(s):
    t = np.arange(s)
    nf = C_DK // 4
    inv = ROPE_THETA ** (-np.arange(nf, dtype=np.float64) / nf)
    lane = np.arange(C_DK)
    pos = np.where((lane // (C_DK // 2))[None, :] == 0, (t // GRID_W)[:, None], (t % GRID_W)[:, None])
    ang = pos.astype(np.float32) * inv.astype(np.float32)[lane % nf][None, :]
    cos = np.tile(np.cos(ang).astype(np.float32), (1, C_HEADS))
    sin = np.tile(np.sin(ang).astype(np.float32), (1, C_HEADS))
    return jnp.asarray(cos), jnp.asarray(sin)


def _gla(main_c, small_c, main_l, small_l, w2, b2, norm_w):
    b_, lc, _ = main_c.shape
    ll = main_l.shape[1]
    lt = lc + ll
    assert lc % GLA_TILE == 0 and ll % GLA_TILE == 0
    cos, sin = _rope_tables(ll)
    w2f = jnp.zeros((SMALL_W, C_QW), BF16).at[16:16 + C_RANK].set(w2[0].astype(BF16))
    w2b = jnp.zeros((SMALL_W, C_QW), BF16).at[16 + C_RANK:16 + 2 * C_RANK].set(w2[1].astype(BF16))
    nw = jnp.tile(norm_w.astype(F32), C_HEADS).reshape(1, C_VW)
    nchunks = lt // C_CHUNK

    def seg(l, w, idx):
        return pl.BlockSpec((1, l, w), lambda b: (b, 0, idx))

    def full(shape):
        return pl.BlockSpec(shape, lambda b: (0,) * len(shape))

    return pl.pallas_call(
        functools.partial(_gla_kernel, lc, ll),
        grid=(b_,),
        in_specs=[seg(lc, C_QW, 24), seg(lc, C_QW, 25), seg(lc, C_VW, 13), seg(lc, SMALL_W, 0), seg(lc, C_VW, 14),
                  seg(ll, C_QW, 24), seg(ll, C_QW, 25), seg(ll, C_VW, 13), seg(ll, SMALL_W, 0), seg(ll, C_VW, 14),
                  full((ll, C_QW)), full((ll, C_QW)), full((SMALL_W, C_QW)), full((SMALL_W, C_QW)),
                  full((2, C_QW)), full((1, C_VW))],
        out_specs=[seg(lc, C_VW, 0), seg(ll, C_VW, 0)],
        out_shape=[jax.ShapeDtypeStruct((b_, lc, C_VW), BF16), jax.ShapeDtypeStruct((b_, ll, C_VW), BF16)],
        scratch_shapes=[pltpu.VMEM((lt, C_QW), BF16), pltpu.VMEM((lt, C_QW), BF16), pltpu.VMEM((lt, C_VW), BF16),
                        pltpu.VMEM((lt, C_QW), F32), pltpu.VMEM((lt, C_QW), F32),
                        pltpu.VMEM((lt, C_QW), BF16), pltpu.VMEM((lt, C_QW), BF16),
                        pltpu.VMEM((lt, C_QW), BF16), pltpu.VMEM((lt, C_QW), BF16),
                        pltpu.VMEM((nchunks * 8, C_QW), F32), pltpu.VMEM((nchunks * 8, C_QW), F32),
                        pltpu.VMEM((lt, C_VW), F32), pltpu.VMEM((2, C_VW, C_QW), F32)],
        compiler_params=pltpu.CompilerParams(vmem_limit_bytes=VMEM_LIMIT),
        name="gla",
    )(main_c, main_c, main_c, small_c, main_c, main_l, main_l, main_l, small_l, main_l,
      cos, sin, w2f, w2b, b2.astype(F32), nw)


GDN_CHUNK = 128
GDN_BASE = 16
GDN_LEVELS = 3
M_INCL, M_STRICT, M_BASE, M_LEVEL, M_EYE = 0, 2, 4, 5, 8
N_MASKS = 9
GDN_PRE_CHUNKS = 4
GDN_REC_FIRST_STAGE = 2
CONV_HALO = 16
CONV_TILE = 256


def _gdn_kernel(lc, ll, n_seq,
                xqc_ref, xkc_ref, xvc_ref, sc_ref, zc_ref, xql_ref, xkl_ref, xvl_ref, sl_ref, zl_ref,
                cwq_ref, cwk_ref, cwv_ref, alog_ref, dtb_ref, nw_ref, oc_ref, ol_ref,
                pad_s, q_s, k_s, v_s, gf_s, gb_s, bef_s, beb_s, u_s, w_s, qd_s, kdt_s, qk_s, gl_s, msk_s, o_s, st_s):
    c = GDN_CHUNK
    nc_c, nc_l = lc // c, ll // c
    step = pl.program_id(0)
    h = jnp.minimum(step, n_seq - 1) % A_HEADS
    cur = step % 2
    prev = 1 - cur

    @pl.when(step == 0)
    def _():
        for buf in (u_s, w_s, qd_s, kdt_s, qk_s, gl_s):
            buf[1] = jnp.zeros(buf.shape[1:], buf.dtype)

    def conv_group(parts, seg_base, l, g0, n_tok):
        t = CONV_TILE
        for p, (x_ref, w_ref, dst, mode) in enumerate(parts):
            lo = _aligned(jnp.maximum(g0 - CONV_HALO, 0), CONV_HALO)
            hi = _aligned(jnp.minimum(g0 + n_tok, l - CONV_HALO), CONV_HALO)
            left = x_ref[0, pl.ds(lo, CONV_HALO), :].astype(F32)
            right = x_ref[0, pl.ds(hi, CONV_HALO), :].astype(F32)
            pad_s[p, 0:CONV_HALO, :] = jnp.where(g0 > 0, left, 0.0)
            pad_s[p, CONV_HALO:CONV_HALO + n_tok, :] = x_ref[0, pl.ds(_aligned(g0, n_tok), n_tok), :].astype(F32)
            pad_s[p, CONV_HALO + n_tok:2 * CONV_HALO + n_tok, :] = jnp.where(g0 + n_tok < l, right, 0.0)
            w = w_ref[...]
            for t0 in range(0, n_tok, t):
                acc = jnp.zeros((t, LANES), F32)
                for j in range(CONV_K):
                    r0 = CONV_HALO - CONV_K // 2 + j + t0
                    acc = acc + pad_s[p, r0:r0 + t, :] * w[j:j + 1]
                y = _silu(acc)
                if mode != "v":
                    y = y * lax.rsqrt(jnp.sum(y * y, -1, keepdims=True) + 1e-6)
                if mode == "q":
                    y = y * (A_DK ** -0.5)
                dst[pl.ds(_aligned(seg_base + g0 + t0, t), t), :] = y.astype(dst.dtype)

    parts_c = ((xqc_ref, cwq_ref, q_s, "q"), (xkc_ref, cwk_ref, k_s, "k"), (xvc_ref, cwv_ref, v_s, "v"))
    parts_l = ((xql_ref, cwq_ref, q_s, "q"), (xkl_ref, cwk_ref, k_s, "k"), (xvl_ref, cwv_ref, v_s, "v"))

    ti = _iota2((c, c), 0)
    tj = _iota2((c, c), 1)
    as_f32 = lambda m: jnp.where(m, 1.0, 0.0).astype(F32)
    tri_l = as_f32(tj <= ti)
    tri_u = as_f32(tj >= ti)
    msk_s[M_INCL] = tri_l
    msk_s[M_INCL + 1] = tri_u
    msk_s[M_STRICT] = as_f32(tj < ti)
    msk_s[M_STRICT + 1] = as_f32(tj > ti)
    msk_s[M_BASE] = as_f32(ti // GDN_BASE == tj // GDN_BASE)
    for lv in range(GDN_LEVELS):
        s = GDN_BASE << lv
        msk_s[M_LEVEL + lv] = as_f32((ti // (2 * s) == tj // (2 * s)) & (ti // s != tj // s))
    msk_s[M_EYE] = as_f32(ti == tj)
    neg_a = -jnp.exp(alog_ref[...])
    lane1 = _iota2((c, LANES), 1)

    def lane_bcast(g, idx):
        col = jnp.sum(jnp.where(lane1 == idx, g, 0.0), axis=-1, keepdims=True)
        return jnp.broadcast_to(col, (c, LANES))

    def gates_group(s_ref, seg_base, g0, n_tok):
        for t0 in range(0, n_tok, c):
            x = s_ref[0, pl.ds(_aligned(g0 + t0, c), c), :]
            g = jnp.where(lane1 < 2 * A_HEADS, neg_a * _softplus(x + dtb_ref[...]), _sigmoid(x))
            sl = pl.ds(_aligned(seg_base + g0 + t0, c), c)
            gf_s[sl, :] = _mm_01(tri_l.astype(BF16), lane_bcast(g, h))
            gb_s[sl, :] = _mm_01(tri_u.astype(BF16), lane_bcast(g, h + A_HEADS))
            bef_s[sl, :] = lane_bcast(g, h + 2 * A_HEADS).astype(BF16)
            beb_s[sl, :] = lane_bcast(g, h + 3 * A_HEADS).astype(BF16)

    def front_end(seg, g0, n_tok):
        parts, s_ref, base, l = ((parts_c, sc_ref, 0, lc), (parts_l, sl_ref, lc, ll))[seg]
        conv_group(parts, base, l, g0, n_tok)
        gates_group(s_ref, base, g0, n_tok)

    def prepass(first_chunk, per_step, it):
        chains = []
        for j in range(per_step):
            ci = first_chunk + it * per_step + j
            sl = pl.ds(_aligned(ci * c, c), c)
            kbf = k_s[sl, :]
            q = q_s[sl, :].astype(F32)
            k = kbf.astype(F32)
            gram = _nt(kbf, kbf)
            qk_raw = _nt(q_s[sl, :], kbf)
            for d, (g_s, be_s) in enumerate(((gf_s, bef_s), (gb_s, beb_s))):
                chains.append(dict(ci=ci, sl=sl, d=d, q=q, k=k, v=v_s[sl, :].astype(F32), gram=gram, qk_raw=qk_raw,
                                   gc=g_s[sl, :], beta=be_s[sl, :].astype(F32)))
        yield
        for ch in chains:
            gc = ch["gc"]
            ch["decay"] = jnp.exp(jnp.minimum(gc - gc.T, 0.0)) * msk_s[M_INCL + ch["d"]]
            ch["a"] = ch["gram"] * ch["beta"] * ch["decay"] * msk_s[M_STRICT + ch["d"]]
            n0 = -(ch["a"] * msk_s[M_BASE])
            ch["inv"] = msk_s[M_EYE] + n0
            ch["pw"] = n0.astype(BF16)
        yield
        n_sq = GDN_BASE.bit_length() - 2
        for ch in chains:
            ch["pw"] = _mm(ch["pw"], ch["pw"]).astype(BF16)
        yield
        for m in range(n_sq):
            for ch in chains:
                ch["inv_next"] = ch["inv"] + _mm(ch["inv"].astype(BF16), ch["pw"])
            if m + 1 < n_sq:
                for ch in chains:
                    ch["pw"] = _mm(ch["pw"], ch["pw"]).astype(BF16)
            for ch in chains:
                ch["inv"] = ch["inv_next"]
            yield
        for lv in range(GDN_LEVELS):
            for ch in chains:
                ch["invb"] = ch["inv"].astype(BF16)
                ch["t"] = _mm((ch["a"] * msk_s[M_LEVEL + lv]).astype(BF16), ch["invb"]).astype(BF16)
            yield
            for ch in chains:
                ch["inv"] = ch["inv"] - _mm(ch["invb"], ch["t"])
            yield
        for ch in chains:
            gc, beta, q, k, d, sl = ch["gc"], ch["beta"], ch["q"], ch["k"], ch["d"], ch["sl"]
            eg = jnp.exp(gc)
            rhs = jnp.concatenate([ch["v"] * beta, k * beta * eg], axis=-1).astype(BF16)
            sol = _mm(ch["inv"].astype(BF16), rhs)
            g_last = gc[c - 1:c] if d == 0 else gc[0:1]
            u_s[cur, d, sl, :] = sol[:, :A_DV].astype(BF16)
            w_s[cur, d, sl, :] = sol[:, A_DV:].astype(BF16)
            qk_s[cur, d, sl, :] = (ch["qk_raw"] * ch["decay"]).astype(BF16)
            qd_s[cur, d, sl, :] = (q * eg).astype(BF16)
            kdt_s[cur, d, sl, :] = (k * jnp.exp(g_last - gc)).T.astype(BF16)
            gl_s[cur, d, pl.ds(_aligned(ch["ci"] * 8, 8), 8), :] = jnp.exp(
                jnp.broadcast_to(g_last, (8, LANES)))

    def recurrence(first_step, n_steps, it):
        st = [st_s[0], st_s[1]]
        for j in range(n_steps):
            n = first_step + it * n_steps + j
            cis = (n, _bwd_chunk(n, nc_c, nc_l))
            sls = [pl.ds(_aligned(ci * c, c), c) for ci in cis]
            stb = [x.astype(BF16) for x in st]
            w_st = [_mm(w_s[prev, d, sls[d], :], stb[d]) for d in range(2)]
            o_st = [_mm(qd_s[prev, d, sls[d], :], stb[d]) for d in range(2)]
            yield
            vnb = [(u_s[prev, d, sls[d], :].astype(F32) - w_st[d]).astype(BF16) for d in range(2)]
            for d in range(2):
                gl = gl_s[prev, d, pl.ds(_aligned(cis[d] * 8, 8), 8), :][0:1]
                st[d] = st[d] * gl + _mm(kdt_s[prev, d, sls[d], :], vnb[d])
            for d in range(2):
                o_s[sls[d], :] += o_st[d] + _mm(qk_s[prev, d, sls[d], :], vnb[d])
            yield
        st_s[0] = st[0]
        st_s[1] = st[1]

    o_s[...] = jnp.zeros_like(o_s)
    st_s[...] = jnp.zeros_like(st_s)

    def merged(first, per_step, next_group, it, carry):
        pre = prepass(first, per_step, it)
        rec = recurrence(first, per_step, it)
        for i, _ in enumerate(pre):
            if i == 0 and next_group is not None:
                front_end(*next_group(it))
            if i >= GDN_REC_FIRST_STAGE:
                next(rec, None)
        for _ in rec:
            pass
        return carry

    per_c = math.gcd(nc_c, GDN_PRE_CHUNKS)
    per_l = math.gcd(nc_l, GDN_PRE_CHUNKS)
    it_c, it_l = nc_c // per_c, nc_l // per_l
    front_end(0, 0, per_c * c)
    if it_c > 1:
        lax.fori_loop(0, it_c - 1, functools.partial(
            merged, 0, per_c, lambda it: (0, (it + 1) * (per_c * c), per_c * c)), 0)
    merged(0, per_c, lambda it: (1, 0, per_l * c), it_c - 1, 0)
    if it_l > 1:
        lax.fori_loop(0, it_l - 1, functools.partial(
            merged, nc_c, per_l, lambda it: (1, (it + 1) * (per_l * c), per_l * c)), 0)
    merged(nc_c, per_l, None, it_l - 1, 0)

    def epilogue(z_ref, o_ref, base, n_tiles):
        def body(t, carry):
            off = pl.multiple_of(t * GDN_CHUNK, GDN_CHUNK)
            o = o_s[pl.ds(pl.multiple_of(base + t * GDN_CHUNK, GDN_CHUNK), GDN_CHUNK), :]
            o = o * lax.rsqrt(jnp.mean(o * o, -1, keepdims=True) + 1e-6) * nw_ref[...]
            z = z_ref[0, pl.ds(off, GDN_CHUNK), :].astype(F32)
            o_ref[0, pl.ds(off, GDN_CHUNK), :] = (o * _silu(z)).astype(BF16)
            return carry
        lax.fori_loop(0, n_tiles, body, 0, unroll=2)

    epilogue(zc_ref, oc_ref, 0, lc // GDN_CHUNK)
    epilogue(zl_ref, ol_ref, lc, ll // GDN_CHUNK)


def _gdn(main_c, small_c, main_l, small_l, conv_w, a_log, dt_bias, norm_w):
    b_, lc, _ = main_c.shape
    ll = main_l.shape[1]
    lt = lc + ll
    assert lc % 256 == 0 and ll % 256 == 0
    pad8 = lambda p: jnp.zeros((1, LANES), F32).at[0, :2 * A_HEADS].set(p.astype(F32).reshape(-1))
    cw = conv_w.astype(F32)

    n_seq = b_ * A_HEADS
    pre_seq = lambda s: jnp.minimum(s, n_seq - 1)
    rec_seq = lambda s: jnp.maximum(s - 1, 0)

    def seg(l, w, part, seq):
        return pl.BlockSpec((1, l, w), lambda s: (seq(s) // A_HEADS, 0, part * A_HEADS + seq(s) % A_HEADS))

    def cw_spec(part):
        return pl.BlockSpec((CONV_K, A_DK), lambda s: (0, part * A_HEADS + pre_seq(s) % A_HEADS))

    const = lambda shape: pl.BlockSpec(shape, lambda s: (0,) * len(shape))
    in_specs = []
    for l in (lc, ll):
        in_specs += [seg(l, A_DK, 0, pre_seq), seg(l, A_DK, 1, pre_seq), seg(l, A_DK, 2, pre_seq),
                     pl.BlockSpec((1, l, SMALL_W), lambda s: (pre_seq(s) // A_HEADS, 0, 0)),
                     seg(l, A_DV, 3, rec_seq)]
    in_specs += [cw_spec(0), cw_spec(1), cw_spec(2), const((1, LANES)), const((1, LANES)), const((1, A_DV))]
    tok = lambda dt: pltpu.VMEM((lt, LANES), dt)
    halves = lambda: pltpu.VMEM((2, 2, lt, LANES), BF16)
    return pl.pallas_call(
        functools.partial(_gdn_kernel, lc, ll, n_seq),
        grid=(n_seq + 1,),
        in_specs=in_specs,
        out_specs=[seg(lc, A_DV, 0, rec_seq), seg(ll, A_DV, 0, rec_seq)],
        out_shape=[jax.ShapeDtypeStruct((b_, lc, A_W), BF16), jax.ShapeDtypeStruct((b_, ll, A_W), BF16)],
        scratch_shapes=[pltpu.VMEM((3, min(GDN_PRE_CHUNKS * GDN_CHUNK, max(lc, ll)) + 2 * CONV_HALO, LANES), F32),
                        tok(BF16), tok(BF16), tok(BF16), tok(F32), tok(F32), tok(BF16), tok(BF16),
                        halves(), halves(), halves(), halves(), halves(),
                        pltpu.VMEM((2, 2, (lt // GDN_CHUNK) * 8, LANES), F32),
                        pltpu.VMEM((N_MASKS, GDN_CHUNK, GDN_CHUNK), F32),
                        tok(F32), pltpu.VMEM((2, A_DK, A_DV), F32)],
        compiler_params=pltpu.CompilerParams(vmem_limit_bytes=GDN_VMEM_LIMIT,
                                             dimension_semantics=("arbitrary",)),
        name="gdn",
    )(main_c, main_c, main_c, small_c, main_c, main_l, main_l, main_l, small_l, main_l,
      cw, cw, cw, pad8(a_log), pad8(dt_bias), norm_w.astype(F32).reshape(1, A_DV))


def _permute_w_in(w):
    sizes = (3 * A_W, 4 * A_HEADS, A_W, B_W, B_W, B_W, B_W, C_QW, C_QW, C_VW, 2 * C_RANK, C_VW)
    offs = np.concatenate([[0], np.cumsum(sizes)])
    part = lambda i: w[:, offs[i]:offs[i + 1]]
    pad = jnp.zeros((w.shape[0], SMALL_W - 4 * A_HEADS - 2 * C_RANK), w.dtype)
    order = [part(0), part(2), part(3), part(4), part(5), part(6), part(7), part(8), part(9), part(11),
             part(1), part(10), pad]
    return jnp.concatenate(order, axis=1).astype(BF16)


def kernel(x, c, ctx, c_ctx, w_mod, b_mod, w_in, conv_w, a_log, dt_bias, gdn_norm, rpb,
           gla_w2, gla_b2, gla_norm, w_out, ln_g, ln_b):
    depth = w_mod.shape[0]
    b_ = x.shape[0]
    alpha = (2 * depth) ** 0.25
    n_mod = -(-(b_ + 1) // 8) * 8
    c_all = jnp.zeros((n_mod, D_MODEL), F32).at[:b_].set(c).at[b_].set(c_ctx)
    xl, xc = x, ctx
    for i in range(depth):
        ctx_out = i < depth - 1
        mod = _modulation(c_all, w_mod[i], b_mod[i]).reshape(n_mod, 3, 1, D_MODEL)
        sh_l, sc_l, gt_l = mod[:b_, 0], mod[:b_, 1], mod[:b_, 2]
        sh_c, sc_c, gt_c = mod[b_:b_ + 1, 0], mod[b_:b_ + 1, 1], mod[b_:b_ + 1, 2]
        w_perm = _permute_w_in(w_in[i])
        main_l, small_l = _project(xl, sh_l, sc_l, w_perm, True)
        main_c, small_c = _project(xc, sh_c, sc_c, w_perm, False)
        ya_c, ya_l = _gdn(main_c, small_c, main_l, small_l, conv_w[i], a_log[i], dt_bias[i], gdn_norm[i])
        yb_l = _na_latent(main_l, main_c, _na_bias_table(rpb[i]))
        yc_c, yc_l = _gla(main_c, small_c, main_l, small_l, gla_w2[i], gla_b2[i], gla_norm[i])
        w_o = w_out[i].astype(BF16)
        if ctx_out:
            yb_c = _na_ctx(main_c)
            xc = _out_project(xc, ya_c, yb_c, yc_c, w_o, gt_c, ln_g[i], ln_b[i], False, alpha)
        xl = _out_project(xl, ya_l, yb_l, yc_l, w_o, gt_l, ln_g[i], ln_b[i], True, alpha)
    return xl
```

```python
import functools
import math

import numpy as np
import jax
import jax.numpy as jnp
from jax import lax
from jax.experimental import pallas as pl
from jax.experimental.pallas import tpu as pltpu

F32 = jnp.float32
BF16 = jnp.bfloat16
HI = lax.Precision.HIGHEST

D_MODEL = 1024
GRID_W = 64
A_HEADS, A_DK, A_DV, CONV_K, A_CHUNK = 4, 128, 128, 5, 64
B_HEADS, B_DH, WIN_R, WIN_C = 4, 64, 8, 16
C_HEADS, C_DK, C_DV, C_RANK, C_CHUNK = 4, 32, 64, 16, 16
C_GATE_NORM = 16.0
ROPE_THETA = 10000.0
LN_EPS = 1e-6
NEG_INF = -1e30

A_W = A_HEADS * A_DK
B_W = B_HEADS * B_DH
C_QW = C_HEADS * C_DK
C_VW = C_HEADS * C_DV
MAIN_W = 3 * A_W + A_W + 4 * B_W + 2 * C_QW + 2 * C_VW
SMALL_W = 128
PROJ_W = MAIN_W + SMALL_W
LANES = 128
VMEM_LIMIT = 56 * 1024 * 1024
GDN_VMEM_LIMIT = 60 * 1024 * 1024
PROJ_TILE = 1024


def _nt(a, b, precision=None):
    return lax.dot_general(a, b, (((1,), (1,)), ((), ())), precision=precision,
                           preferred_element_type=F32)


def _tn(a, b):
    return lax.dot_general(a, b, (((0,), (0,)), ((), ())), preferred_element_type=F32)


def _mm(a, b, precision=None):
    return jnp.dot(a, b, precision=precision, preferred_element_type=F32)


def _mm_01(m01, x):
    hi = x.astype(BF16)
    lo = (x - hi.astype(F32)).astype(BF16)
    return _mm(m01, hi) + _mm(m01, lo)


def _sigmoid(x):
    return 1.0 / (1.0 + jnp.exp(-x))


def _silu(x):
    return x * _sigmoid(x)


def _softplus(x):
    return jnp.maximum(x, 0.0) + jnp.log(1.0 + jnp.exp(-jnp.abs(x)))


def _iota2(shape, axis):
    return lax.broadcasted_iota(jnp.int32, shape, axis)


def _aligned(x, m):
    return x if isinstance(x, int) else pl.multiple_of(x, m)


def _mod_kernel(c_ref, w_ref, b_ref, o_ref):
    o_ref[...] = _mm(_silu(c_ref[...]), w_ref[...], HI) + b_ref[...]


def _modulation(c_all, w_mod, b_mod, layer):
    n = c_all.shape[0]
    tn = 768
    return pl.pallas_call(
        _mod_kernel,
        grid=(3 * D_MODEL // tn,),
        in_specs=[pl.BlockSpec((n, D_MODEL), lambda j: (0, 0)),
                  pl.BlockSpec((None, D_MODEL, tn), lambda j: (layer, 0, j)),
                  pl.BlockSpec((None, 1, tn), lambda j: (layer, 0, j))],
        out_specs=pl.BlockSpec((n, tn), lambda j: (0, j)),
        out_shape=jax.ShapeDtypeStruct((n, 3 * D_MODEL), F32),
        name="adaln_mod",
    )(c_all, w_mod, b_mod)


def _layer_norm(x):
    mu = jnp.mean(x, -1, keepdims=True)
    xc = x - mu
    var = jnp.mean(xc * xc, -1, keepdims=True)
    return xc * lax.rsqrt(var + LN_EPS)


def _proj_kernel(x_ref, sh_ref, sc_ref, w_ref, o_ref, s_ref):
    m = (_layer_norm(x_ref[0]) * (1.0 + sc_ref[0]) + sh_ref[0]).astype(BF16)
    step = 640
    for c0 in range(0, MAIN_W, step):
        o_ref[0, :, c0:c0 + step] = _mm(m, w_ref[:, c0:c0 + step]).astype(BF16)
    s_ref[0] = _mm(m, w_ref[:, MAIN_W:])


def _project(x, sh, sc, w_perm, per_batch):
    b_, l, _ = x.shape
    t = min(PROJ_TILE, l)
    mod_map = (lambda b, i: (b, 0, 0)) if per_batch else (lambda b, i: (0, 0, 0))
    return pl.pallas_call(
        _proj_kernel,
        grid=(b_, l // t),
        in_specs=[pl.BlockSpec((1, t, D_MODEL), lambda b, i: (b, i, 0)),
                  pl.BlockSpec((1, 1, D_MODEL), mod_map),
                  pl.BlockSpec((1, 1, D_MODEL), mod_map),
                  pl.BlockSpec((D_MODEL, PROJ_W), lambda b, i: (0, 0))],
        out_specs=[pl.BlockSpec((1, t, MAIN_W), lambda b, i: (b, i, 0)),
                   pl.BlockSpec((1, t, SMALL_W), lambda b, i: (b, i, 0))],
        out_shape=[jax.ShapeDtypeStruct((b_, l, MAIN_W), BF16),
                   jax.ShapeDtypeStruct((b_, l, SMALL_W), F32)],
        compiler_params=pltpu.CompilerParams(vmem_limit_bytes=VMEM_LIMIT),
        name="ln_mod_proj",
    )(x, sh, sc, w_perm)


def _out_kernel(alpha, x_ref, ya_ref, yb_ref, yc_ref, w_ref, gt_ref, g_ref, b_ref, o_ref):
    y = jnp.concatenate([ya_ref[0], yb_ref[0], yc_ref[0]], axis=-1)
    h = alpha * x_ref[0] + gt_ref[0] * _mm(y, w_ref[...])
    o_ref[0] = _layer_norm(h) * g_ref[...] + b_ref[...]


def _out_project(x, ya, yb, yc, w_out, gt, ln_g, ln_b, per_batch, alpha):
    b_, l, _ = x.shape
    t = min(PROJ_TILE, l)
    mod_map = (lambda b, i: (b, 0, 0)) if per_batch else (lambda b, i: (0, 0, 0))
    tok = lambda w: pl.BlockSpec((1, t, w), lambda b, i: (b, i, 0))
    return pl.pallas_call(
        functools.partial(_out_kernel, alpha),
        grid=(b_, l // t),
        in_specs=[tok(D_MODEL), tok(A_W), tok(B_W), tok(C_VW),
                  pl.BlockSpec((D_MODEL, D_MODEL), lambda b, i: (0, 0)),
                  pl.BlockSpec((1, 1, D_MODEL), mod_map),
                  pl.BlockSpec((1, D_MODEL), lambda b, i: (0, 0)),
                  pl.BlockSpec((1, D_MODEL), lambda b, i: (0, 0))],
        out_specs=tok(D_MODEL),
        out_shape=jax.ShapeDtypeStruct((b_, l, D_MODEL), F32),
        compiler_params=pltpu.CompilerParams(vmem_limit_bytes=VMEM_LIMIT),
        name="out_proj_postnorm",
    )(x, ya, yb, yc, w_out, gt, ln_g.reshape(1, -1), ln_b.reshape(1, -1))


NA_ROWS_PER_STEP = 4


def _stack_heads(q):
    head = _iota2(q.shape, 1) // B_DH
    return jnp.concatenate([jnp.where(head == h, q, jnp.zeros_like(q)) for h in range(B_HEADS)], axis=0)


def _unstack_heads(o, n):
    head = _iota2((n, B_W), 1) // B_DH
    out = jnp.zeros((n, B_W), F32)
    for h in range(B_HEADS):
        out = jnp.where(head == h, o[h * n:(h + 1) * n], out)
    return out


def _na_kernel(rows, q_ref, k_ref, v_ref, kc_ref, vc_ref, z_ref, bias_ref, o_ref):
    kc = kc_ref[0]
    vc = vc_ref[0]
    n_win = WIN_R * GRID_W
    for rr in range(NA_ROWS_PER_STEP):
        r = pl.program_id(1) * NA_ROWS_PER_STEP + rr
        r0 = jnp.clip(r - WIN_R // 2, 0, rows - WIN_R)
        start = pl.multiple_of(r0 * GRID_W, GRID_W)
        kw = k_ref[0, pl.ds(start, n_win), :]
        vw = v_ref[0, pl.ds(start, n_win), :]
        tok = slice(rr * GRID_W, (rr + 1) * GRID_W)
        qs = _stack_heads(q_ref[0, tok, :] * (B_DH ** -0.5))
        s_win = _nt(qs, kw)
        s_ctx = _nt(qs, kc)
        p_win, p_ctx, inv = [], [], []
        for h in range(B_HEADS):
            blk = slice(h * GRID_W, (h + 1) * GRID_W)
            sw = s_win[blk] + bias_ref[r - r0, blk, :]
            sc = s_ctx[blk]
            m = jnp.maximum(jnp.max(sw, -1, keepdims=True), jnp.max(sc, -1, keepdims=True))
            pw = jnp.exp(sw - m)
            pc = jnp.exp(sc - m)
            inv.append(1.0 / (jnp.sum(pw, -1, keepdims=True) + jnp.sum(pc, -1, keepdims=True)))
            p_win.append(pw.astype(BF16))
            p_ctx.append(pc.astype(BF16))
        o = _mm(jnp.concatenate(p_win, axis=0), vw) + _mm(jnp.concatenate(p_ctx, axis=0), vc)
        o = _unstack_heads(o * jnp.concatenate(inv, axis=0), GRID_W)
        o_ref[0, tok, :] = (o * _silu(z_ref[0, tok, :].astype(F32))).astype(BF16)


def _na_bias_table(rpb):
    cq = np.arange(GRID_W)
    c0 = np.clip(cq - WIN_C // 2, 0, GRID_W - WIN_C)
    col_ok = (cq[None, :] >= c0[:, None]) & (cq[None, :] < c0[:, None] + WIN_C)
    dj = np.clip(cq[None, :] - cq[:, None] + (WIN_C - 1), 0, 2 * WIN_C - 2)
    onehot = (dj[None] == np.arange(2 * WIN_C - 1)[:, None, None]).astype(np.float32)
    t = jnp.einsum('hdk,kqc->hdqc', rpb.astype(F32), jnp.asarray(onehot), precision=HI)
    t = jnp.where(col_ok[None, None], t, NEG_INF)
    slabs = [jnp.transpose(t[:, WIN_R - 1 - v:2 * WIN_R - 1 - v], (0, 2, 1, 3)) for v in range(WIN_R)]
    return jnp.stack(slabs).reshape(WIN_R, B_HEADS * GRID_W, WIN_R * GRID_W)


def _na_latent(main_l, main_c, bias):
    b_, s, _ = main_l.shape
    lc = main_c.shape[1]
    rows = s // GRID_W
    assert rows >= WIN_R and rows % NA_ROWS_PER_STEP == 0
    tq = NA_ROWS_PER_STEP * GRID_W
    col = lambda idx: (lambda b, i: (b, 0, idx))
    return pl.pallas_call(
        functools.partial(_na_kernel, rows),
        grid=(b_, rows // NA_ROWS_PER_STEP),
        in_specs=[pl.BlockSpec((1, tq, B_W), lambda b, i: (b, i, 8)),
                  pl.BlockSpec((1, s, B_W), col(9)),
                  pl.BlockSpec((1, s, B_W), col(10)),
                  pl.BlockSpec((1, lc, B_W), col(9)),
                  pl.BlockSpec((1, lc, B_W), col(10)),
                  pl.BlockSpec((1, tq, B_W), lambda b, i: (b, i, 11)),
                  pl.BlockSpec(bias.shape, lambda b, i: (0, 0, 0))],
        out_specs=pl.BlockSpec((1, tq, B_W), lambda b, i: (b, i, 0)),
        out_shape=jax.ShapeDtypeStruct((b_, s, B_W), BF16),
        compiler_params=pltpu.CompilerParams(vmem_limit_bytes=VMEM_LIMIT),
        name="na_latent",
    )(main_l, main_l, main_l, main_c, main_c, main_l, bias)


def _na_ctx_kernel(q_ref, k_ref, v_ref, z_ref, o_ref):
    lc = q_ref.shape[1]
    qs = _stack_heads(q_ref[0] * (B_DH ** -0.5))
    s = _nt(qs, k_ref[0])
    p = jnp.exp(s - jnp.max(s, -1, keepdims=True))
    o = _mm(p.astype(BF16), v_ref[0]) / jnp.sum(p, -1, keepdims=True)
    o = _unstack_heads(o, lc)
    o_ref[0] = (o * _silu(z_ref[0].astype(F32))).astype(BF16)


def _na_ctx(main_c):
    b_, lc, _ = main_c.shape
    col = lambda idx: pl.BlockSpec((1, lc, B_W), lambda b: (b, 0, idx))
    return pl.pallas_call(
        _na_ctx_kernel,
        grid=(b_,),
        in_specs=[col(8), col(9), col(10), col(11)],
        out_specs=pl.BlockSpec((1, lc, B_W), lambda b: (b, 0, 0)),
        out_shape=jax.ShapeDtypeStruct((b_, lc, B_W), BF16),
        name="na_ctx",
    )(main_c, main_c, main_c, main_c)


GLA_TILE = 128
GLA_INTRA_CHUNKS = 2
GLA_SCAN_STEPS = 8


def _bwd_chunk(n, nc_c, nc_l):
    return jnp.where(n < nc_c, nc_c - 1 - n, nc_c + nc_l - 1 - (n - nc_c))


def _gla_kernel(lc, ll,
                qc_ref, kc_ref, vc_ref, rc_ref, zc_ref, ql_ref, kl_ref, vl_ref, rl_ref, zl_ref,
                cos_ref, sin_ref, w2f_ref, w2b_ref, b2_ref, nw_ref, oc_ref, ol_ref,
                q_s, k_s, v_s, bf_s, bb_s, qdf_s, kdf_s, qdb_s, kdb_s, glf_s, glb_s, of_s, ob_s, st_s):
    c = C_CHUNK
    nc_c, nc_l = lc // c, ll // c
    per_tile = GLA_TILE // c
    scale = C_DK ** -0.5

    ti = _iota2((GLA_TILE, GLA_TILE), 0)
    tj = _iota2((GLA_TILE, GLA_TILE), 1)
    same = (ti // c) == (tj // c)
    tri_l = jnp.where(same & (tj <= ti), 1.0, 0.0).astype(BF16)
    tri_u = jnp.where(same & (tj >= ti), 1.0, 0.0).astype(BF16)
    rot = (jnp.where((tj % 16 < 8) & (ti == tj + 8), -1.0, 0.0)
           + jnp.where((tj % 16 >= 8) & (ti == tj - 8), 1.0, 0.0)).astype(BF16)
    expand = jnp.where(_iota2((C_QW, C_VW), 0) // C_DK == _iota2((C_QW, C_VW), 1) // C_DV,
                       1.0, 0.0).astype(BF16)
    st_mask = jnp.where(_iota2((C_VW, C_QW), 0) // C_DV == _iota2((C_VW, C_QW), 1) // C_DK,
                        1.0, 0.0).astype(F32)
    head_ones = jnp.where(_iota2((C_VW, C_VW), 0) // C_DV == _iota2((C_VW, C_VW), 1) // C_DV,
                          1.0, 0.0).astype(BF16)

    def log_sigmoid(x):
        return jnp.minimum(x, 0.0) - jnp.log(1.0 + jnp.exp(-jnp.abs(x)))

    def prologue(q_ref, k_ref, v_ref, r_ref, base, n_tiles, rope):
        def body(t, carry):
            off = pl.multiple_of(t * GLA_TILE, GLA_TILE)
            uoff = pl.multiple_of(base + t * GLA_TILE, GLA_TILE)
            qb = q_ref[0, pl.ds(off, GLA_TILE), :]
            kb = k_ref[0, pl.ds(off, GLA_TILE), :]
            q = qb.astype(F32)
            k = kb.astype(F32)
            if rope:
                cs = cos_ref[pl.ds(off, GLA_TILE), :]
                sn = sin_ref[pl.ds(off, GLA_TILE), :]
                q = q * cs + _mm(qb, rot) * sn
                k = k * cs + _mm(kb, rot) * sn
            q = q * scale
            r = r_ref[0, pl.ds(off, GLA_TILE), :].astype(BF16)
            gkf = log_sigmoid(_mm(r, w2f_ref[...]) + b2_ref[0:1, :]) / C_GATE_NORM
            gkb = log_sigmoid(_mm(r, w2b_ref[...]) + b2_ref[1:2, :]) / C_GATE_NORM
            bf = _mm_01(tri_l, gkf)
            bb = _mm_01(tri_u, gkb)
            tf = jnp.concatenate([jnp.broadcast_to(bf[i * c + c - 1:i * c + c], (c, C_QW)) for i in range(per_tile)], axis=0)
            tb = jnp.concatenate([jnp.broadcast_to(bb[i * c:i * c + 1], (c, C_QW)) for i in range(per_tile)], axis=0)
            sl = pl.ds(uoff, GLA_TILE)
            q_s[sl, :] = q.astype(BF16)
            k_s[sl, :] = k.astype(BF16)
            v_s[sl, :] = v_ref[0, pl.ds(off, GLA_TILE), :]
            bf_s[sl, :] = bf
            bb_s[sl, :] = bb
            qdf_s[sl, :] = (q * jnp.exp(bf)).astype(BF16)
            kdf_s[sl, :] = (k * jnp.exp(tf - bf)).astype(BF16)
            qdb_s[sl, :] = (q * jnp.exp(bb)).astype(BF16)
            kdb_s[sl, :] = (k * jnp.exp(tb - bb)).astype(BF16)
            goff = pl.multiple_of((base // c + t * per_tile) * 8, 8)
            glf_s[pl.ds(goff, per_tile * 8), :] = jnp.exp(
                jnp.concatenate([tf[i * c:i * c + 8] for i in range(per_tile)], axis=0))
            glb_s[pl.ds(goff, per_tile * 8), :] = jnp.exp(
                jnp.concatenate([tb[i * c:i * c + 8] for i in range(per_tile)], axis=0))
            return carry
        lax.fori_loop(0, n_tiles, body, 0, unroll=2)

    prologue(qc_ref, kc_ref, vc_ref, rc_ref, 0, lc // GLA_TILE, False)
    prologue(ql_ref, kl_ref, vl_ref, rl_ref, lc, ll // GLA_TILE, True)

    jj = _iota2((c, C_QW), 0)
    row_sum = jnp.where(_iota2((c, c * c), 1) // c == _iota2((c, c * c), 0), 1.0, 0.0).astype(BF16)

    def intra(first_chunk):
        offs = [pl.multiple_of((first_chunk + j) * c, c) for j in range(GLA_INTRA_CHUNKS)]
        ps = []
        for off in offs:
            q = q_s[pl.ds(off, c), :].astype(F32)
            k = k_s[pl.ds(off, c), :].astype(F32)
            bf = bf_s[pl.ds(off, c), :]
            bb = bb_s[pl.ds(off, c), :]
            rows = []
            for i in range(c):
                e = jnp.exp(jnp.where(jj < i, bf[i:i + 1] - bf, bb[i:i + 1] - bb))
                rows.append((jnp.where(jj == i, 2.0, e) * (q[i:i + 1] * k)).astype(BF16))
            ps.append(jnp.concatenate(rows, axis=0))
        return [(off, _mm(p, expand)) for off, p in zip(offs, ps)]

    def intra_apply(weights):
        outs = []
        for off, a in weights:
            v = v_s[pl.ds(off, c), :].astype(F32)
            av = (a.reshape(c, c, C_VW) * v[None]).reshape(c * c, C_VW).astype(BF16)
            outs.append(_mm(row_sum, av))
        return outs

    st_s[...] = jnp.zeros_like(st_s)
    dirs = ((qdf_s, kdf_s, glf_s), (qdb_s, kdb_s, glb_s))

    def scan(it, carry):
        steps = [it * GLA_SCAN_STEPS + s for s in range(GLA_SCAN_STEPS)]
        chunk = [steps, [_bwd_chunk(n, nc_c, nc_l) for n in steps]]
        sls = [[pl.ds(pl.multiple_of(ci * c, c), c) for ci in chunk[d]] for d in range(2)]
        upd = [[_tn(v_s[sl, :], dirs[d][1][sl, :]) for sl in sls[d]] for d in range(2)]
        weights = []
        for s in range(0, GLA_SCAN_STEPS, GLA_INTRA_CHUNKS):
            weights += intra(steps[s])
        st = [st_s[0], st_s[1]]
        inter = [[], []]
        for s in range(GLA_SCAN_STEPS):
            for d in range(2):
                inter[d].append(_nt(dirs[d][0][sls[d][s], :], st[d].astype(BF16)))
                gl = dirs[d][2][pl.ds(pl.multiple_of(chunk[d][s] * 8, 8), 8), :][0:1]
                st[d] = st[d] * gl + upd[d][s] * st_mask
        st_s[0] = st[0]
        st_s[1] = st[1]
        local = intra_apply(weights)
        for s in range(GLA_SCAN_STEPS):
            of_s[sls[0][s], :] = local[s] + inter[0][s]
            ob_s[sls[1][s], :] = inter[1][s]
        return carry
    lax.fori_loop(0, (nc_c + nc_l) // GLA_SCAN_STEPS, scan, 0)

    def epilogue(z_ref, o_ref, base, n_tiles):
        def body(t, carry):
            off = pl.multiple_of(t * GLA_TILE, GLA_TILE)
            sl = pl.ds(pl.multiple_of(base + t * GLA_TILE, GLA_TILE), GLA_TILE)
            o = of_s[sl, :] + ob_s[sl, :]
            sq = o * o
            hi = sq.astype(BF16)
            lo = (sq - hi.astype(F32)).astype(BF16)
            ms = (_mm(hi, head_ones) + _mm(lo, head_ones)) * (1.0 / C_DV)
            o = o * lax.rsqrt(ms + 1e-6) * nw_ref[...]
            z = z_ref[0, pl.ds(off, GLA_TILE), :].astype(F32)
            o_ref[0, pl.ds(off, GLA_TILE), :] = (o * _silu(z)).astype(BF16)
            return carry
        lax.fori_loop(0, n_tiles, body, 0, unroll=2)

    epilogue(zc_ref, oc_ref, 0, lc // GLA_TILE)
    epilogue(zl_ref, ol_ref, lc, ll // GLA_TILE)


def _rope_tables(s):
    t = np.arange(s)
    nf = C_DK // 4
    inv = ROPE_THETA ** (-np.arange(nf, dtype=np.float64) / nf)
    lane = np.arange(C_DK)
    pos = np.where((lane // (C_DK // 2))[None, :] == 0, (t // GRID_W)[:, None], (t % GRID_W)[:, None])
    ang = pos.astype(np.float32) * inv.astype(np.float32)[lane % nf][None, :]
    cos = np.tile(np.cos(ang).astype(np.float32), (1, C_HEADS))
    sin = np.tile(np.sin(ang).astype(np.float32), (1, C_HEADS))
    return jnp.asarray(cos), jnp.asarray(sin)


def _gla(main_c, small_c, main_l, small_l, w2, b2, norm_w):
    b_, lc, _ = main_c.shape
    ll = main_l.shape[1]
    lt = lc + ll
    assert lc % GLA_TILE == 0 and ll % GLA_TILE == 0
    cos, sin = _rope_tables(ll)
    w2f = jnp.zeros((SMALL_W, C_QW), BF16).at[16:16 + C_RANK].set(w2[0].astype(BF16))
    w2b = jnp.zeros((SMALL_W, C_QW), BF16).at[16 + C_RANK:16 + 2 * C_RANK].set(w2[1].astype(BF16))
    nw = jnp.tile(norm_w.astype(F32), C_HEADS).reshape(1, C_VW)
    nchunks = lt // C_CHUNK

    def seg(l, w, idx):
        return pl.BlockSpec((1, l, w), lambda b: (b, 0, idx))

    def full(shape):
        return pl.BlockSpec(shape, lambda b: (0,) * len(shape))

    return pl.pallas_call(
        functools.partial(_gla_kernel, lc, ll),
        grid=(b_,),
        in_specs=[seg(lc, C_QW, 24), seg(lc, C_QW, 25), seg(lc, C_VW, 13), seg(lc, SMALL_W, 0), seg(lc, C_VW, 14),
                  seg(ll, C_QW, 24), seg(ll, C_QW, 25), seg(ll, C_VW, 13), seg(ll, SMALL_W, 0), seg(ll, C_VW, 14),
                  full((ll, C_QW)), full((ll, C_QW)), full((SMALL_W, C_QW)), full((SMALL_W, C_QW)),
                  full((2, C_QW)), full((1, C_VW))],
        out_specs=[seg(lc, C_VW, 0), seg(ll, C_VW, 0)],
        out_shape=[jax.ShapeDtypeStruct((b_, lc, C_VW), BF16), jax.ShapeDtypeStruct((b_, ll, C_VW), BF16)],
        scratch_shapes=[pltpu.VMEM((lt, C_QW), BF16), pltpu.VMEM((lt, C_QW), BF16), pltpu.VMEM((lt, C_VW), BF16),
                        pltpu.VMEM((lt, C_QW), F32), pltpu.VMEM((lt, C_QW), F32),
                        pltpu.VMEM((lt, C_QW), BF16), pltpu.VMEM((lt, C_QW), BF16),
                        pltpu.VMEM((lt, C_QW), BF16), pltpu.VMEM((lt, C_QW), BF16),
                        pltpu.VMEM((nchunks * 8, C_QW), F32), pltpu.VMEM((nchunks * 8, C_QW), F32),
                        pltpu.VMEM((lt, C_VW), F32), pltpu.VMEM((lt, C_VW), F32),
                        pltpu.VMEM((2, C_VW, C_QW), F32)],
        compiler_params=pltpu.CompilerParams(vmem_limit_bytes=VMEM_LIMIT),
        name="gla",
    )(main_c, main_c, main_c, small_c, main_c, main_l, main_l, main_l, small_l, main_l,
      cos, sin, w2f, w2b, b2.astype(F32), nw)


GDN_CHUNK = 128
GDN_BASE = 16
GDN_LEVELS = 3
M_INCL, M_STRICT, M_BASE, M_LEVEL, M_EYE = 0, 2, 4, 5, 8
N_MASKS = 9
GDN_PRE_CHUNKS = 4
GDN_REC_FIRST_STAGE = 2
CONV_HALO = 16
CONV_TILE = 256


def _gdn_kernel(lc, ll, n_seq,
                xqc_ref, xkc_ref, xvc_ref, sc_ref, zc_ref, xql_ref, xkl_ref, xvl_ref, sl_ref, zl_ref,
                cwq_ref, cwk_ref, cwv_ref, alog_ref, dtb_ref, nw_ref, oc_ref, ol_ref,
                pad_s, q_s, k_s, v_s, gf_s, gb_s, bef_s, beb_s, u_s, w_s, qd_s, kdt_s, qk_s, gl_s, msk_s, o_s, st_s):
    c = GDN_CHUNK
    nc_c, nc_l = lc // c, ll // c
    step = pl.program_id(0)
    h = jnp.minimum(step, n_seq - 1) % A_HEADS
    cur = step % 2
    prev = 1 - cur

    @pl.when(step == 0)
    def _():
        for buf in (u_s, w_s, qd_s, kdt_s, qk_s, gl_s):
            buf[1] = jnp.zeros(buf.shape[1:], buf.dtype)

    def conv_group(parts, seg_base, l, g0, n_tok):
        t = CONV_TILE
        for p, (x_ref, w_ref, dst, mode) in enumerate(parts):
            lo = _aligned(jnp.maximum(g0 - CONV_HALO, 0), CONV_HALO)
            hi = _aligned(jnp.minimum(g0 + n_tok, l - CONV_HALO), CONV_HALO)
            left = x_ref[0, pl.ds(lo, CONV_HALO), :].astype(F32)
            right = x_ref[0, pl.ds(hi, CONV_HALO), :].astype(F32)
            pad_s[p, 0:CONV_HALO, :] = jnp.where(g0 > 0, left, 0.0)
            pad_s[p, CONV_HALO:CONV_HALO + n_tok, :] = x_ref[0, pl.ds(_aligned(g0, n_tok), n_tok), :].astype(F32)
            pad_s[p, CONV_HALO + n_tok:2 * CONV_HALO + n_tok, :] = jnp.where(g0 + n_tok < l, right, 0.0)
            w = w_ref[...]
            for t0 in range(0, n_tok, t):
                acc = jnp.zeros((t, LANES), F32)
                for j in range(CONV_K):
                    r0 = CONV_HALO - CONV_K // 2 + j + t0
                    acc = acc + pad_s[p, r0:r0 + t, :] * w[j:j + 1]
                y = _silu(acc)
                if mode != "v":
                    y = y * lax.rsqrt(jnp.sum(y * y, -1, keepdims=True) + 1e-6)
                if mode == "q":
                    y = y * (A_DK ** -0.5)
                dst[pl.ds(_aligned(seg_base + g0 + t0, t), t), :] = y.astype(dst.dtype)

    parts_c = ((xqc_ref, cwq_ref, q_s, "q"), (xkc_ref, cwk_ref, k_s, "k"), (xvc_ref, cwv_ref, v_s, "v"))
    parts_l = ((xql_ref, cwq_ref, q_s, "q"), (xkl_ref, cwk_ref, k_s, "k"), (xvl_ref, cwv_ref, v_s, "v"))

    ti = _iota2((c, c), 0)
    tj = _iota2((c, c), 1)
    as_f32 = lambda m: jnp.where(m, 1.0, 0.0).astype(F32)
    tri_l = as_f32(tj <= ti)
    tri_u = as_f32(tj >= ti)
    msk_s[M_INCL] = tri_l
    msk_s[M_INCL + 1] = tri_u
    msk_s[M_STRICT] = as_f32(tj < ti)
    msk_s[M_STRICT + 1] = as_f32(tj > ti)
    msk_s[M_BASE] = -as_f32(ti // GDN_BASE == tj // GDN_BASE)
    for lv in range(GDN_LEVELS):
        s = GDN_BASE << lv
        msk_s[M_LEVEL + lv] = as_f32((ti // (2 * s) == tj // (2 * s)) & (ti // s != tj // s))
    msk_s[M_EYE] = as_f32(ti == tj)
    neg_a = -jnp.exp(alog_ref[...])
    lane1 = _iota2((c, LANES), 1)

    def lane_bcast(g, idx):
        col = jnp.sum(jnp.where(lane1 == idx, g, 0.0), axis=-1, keepdims=True)
        return jnp.broadcast_to(col, (c, LANES))

    def gates_group(s_ref, seg_base, g0, n_tok):
        for t0 in range(0, n_tok, c):
            x = s_ref[0, pl.ds(_aligned(g0 + t0, c), c), :]
            g = jnp.where(lane1 < 2 * A_HEADS, neg_a * _softplus(x + dtb_ref[...]), _sigmoid(x))
            sl = pl.ds(_aligned(seg_base + g0 + t0, c), c)
            gf_s[sl, :] = _mm_01(tri_l.astype(BF16), lane_bcast(g, h))
            gb_s[sl, :] = _mm_01(tri_u.astype(BF16), lane_bcast(g, h + A_HEADS))
            bef_s[sl, :] = lane_bcast(g, h + 2 * A_HEADS).astype(BF16)
            beb_s[sl, :] = lane_bcast(g, h + 3 * A_HEADS).astype(BF16)

    def front_end(seg, g0, n_tok):
        parts, s_ref, base, l = ((parts_c, sc_ref, 0, lc), (parts_l, sl_ref, lc, ll))[seg]
        conv_group(parts, base, l, g0, n_tok)
        gates_group(s_ref, base, g0, n_tok)

    def prepass(first_chunk, per_step, it):
        chains = []
        for j in range(per_step):
            ci = first_chunk + it * per_step + j
            sl = pl.ds(_aligned(ci * c, c), c)
            kbf = k_s[sl, :]
            q = q_s[sl, :].astype(F32)
            k = kbf.astype(F32)
            gram = _nt(kbf, kbf)
            qk_raw = _nt(q_s[sl, :], kbf)
            for d, (g_s, be_s) in enumerate(((gf_s, bef_s), (gb_s, beb_s))):
                chains.append(dict(ci=ci, sl=sl, d=d, q=q, k=k, v=v_s[sl, :].astype(F32), gram=gram, qk_raw=qk_raw,
                                   gc=g_s[sl, :], beta=be_s[sl, :].astype(F32)))
        yield
        for ch in chains:
            gc = ch["gc"]
            ch["decay"] = jnp.exp(jnp.minimum(gc - gc.T, 0.0)) * msk_s[M_INCL + ch["d"]]
            ch["a"] = ch["gram"] * ch["beta"] * ch["decay"] * msk_s[M_STRICT + ch["d"]]
            n0 = ch["a"] * msk_s[M_BASE]
            ch["inv"] = msk_s[M_EYE] + n0
            ch["pw"] = n0.astype(BF16)
        yield
        n_sq = GDN_BASE.bit_length() - 2
        for ch in chains:
            ch["pw"] = _mm(ch["pw"], ch["pw"]).astype(BF16)
        yield
        for m in range(n_sq):
            for ch in chains:
                ch["inv_next"] = ch["inv"] + _mm(ch["inv"].astype(BF16), ch["pw"])
            if m + 1 < n_sq:
                for ch in chains:
                    ch["pw"] = _mm(ch["pw"], ch["pw"]).astype(BF16)
            for ch in chains:
                ch["inv"] = ch["inv_next"]
            yield
        for lv in range(GDN_LEVELS):
            for ch in chains:
                ch["invb"] = ch["inv"].astype(BF16)
                ch["t"] = _mm((ch["a"] * msk_s[M_LEVEL + lv]).astype(BF16), ch["invb"]).astype(BF16)
            yield
            for ch in chains:
                ch["inv"] = ch["inv"] - _mm(ch["invb"], ch["t"])
            yield
        for ch in chains:
            gc, beta, q, k, d, sl = ch["gc"], ch["beta"], ch["q"], ch["k"], ch["d"], ch["sl"]
            eg = jnp.exp(gc)
            rhs = jnp.concatenate([ch["v"] * beta, k * beta * eg], axis=-1).astype(BF16)
            sol = _mm(ch["inv"].astype(BF16), rhs)
            g_last = gc[c - 1:c] if d == 0 else gc[0:1]
            u_s[cur, d, sl, :] = sol[:, :A_DV].astype(BF16)
            w_s[cur, d, sl, :] = sol[:, A_DV:].astype(BF16)
            qk_s[cur, d, sl, :] = (ch["qk_raw"] * ch["decay"]).astype(BF16)
            qd_s[cur, d, sl, :] = (q * eg).astype(BF16)
            kdt_s[cur, d, sl, :] = (k * jnp.exp(g_last - gc)).T.astype(BF16)
            gl_s[cur, d, pl.ds(_aligned(ch["ci"] * 8, 8), 8), :] = jnp.exp(
                jnp.broadcast_to(g_last, (8, LANES)))

    def recurrence(first_step, n_steps, it):
        st = [st_s[0], st_s[1]]
        for j in range(n_steps):
            n = first_step + it * n_steps + j
            cis = (n, _bwd_chunk(n, nc_c, nc_l))
            sls = [pl.ds(_aligned(ci * c, c), c) for ci in cis]
            stb = [x.astype(BF16) for x in st]
            w_st = [_mm(w_s[prev, d, sls[d], :], stb[d]) for d in range(2)]
            o_st = [_mm(qd_s[prev, d, sls[d], :], stb[d]) for d in range(2)]
            yield
            vnb = [(u_s[prev, d, sls[d], :].astype(F32) - w_st[d]).astype(BF16) for d in range(2)]
            for d in range(2):
                gl = gl_s[prev, d, pl.ds(_aligned(cis[d] * 8, 8), 8), :][0:1]
                st[d] = st[d] * gl + _mm(kdt_s[prev, d, sls[d], :], vnb[d])
            for d in range(2):
                o_s[sls[d], :] += o_st[d] + _mm(qk_s[prev, d, sls[d], :], vnb[d])
            yield
        st_s[0] = st[0]
        st_s[1] = st[1]

    o_s[...] = jnp.zeros_like(o_s)
    st_s[...] = jnp.zeros_like(st_s)

    def merged(first, per_step, next_group, it, carry):
        pre = prepass(first, per_step, it)
        rec = recurrence(first, per_step, it)
        for i, _ in enumerate(pre):
            if i == 0 and next_group is not None:
                front_end(*next_group(it))
            if i >= GDN_REC_FIRST_STAGE:
                next(rec, None)
        for _ in rec:
            pass
        return carry

    per_c = math.gcd(nc_c, GDN_PRE_CHUNKS)
    per_l = math.gcd(nc_l, GDN_PRE_CHUNKS)
    it_c, it_l = nc_c // per_c, nc_l // per_l
    front_end(0, 0, per_c * c)
    if it_c > 1:
        lax.fori_loop(0, it_c - 1, functools.partial(
            merged, 0, per_c, lambda it: (0, (it + 1) * (per_c * c), per_c * c)), 0)
    merged(0, per_c, lambda it: (1, 0, per_l * c), it_c - 1, 0)
    if it_l > 1:
        lax.fori_loop(0, it_l - 1, functools.partial(
            merged, nc_c, per_l, lambda it: (1, (it + 1) * (per_l * c), per_l * c)), 0)
    merged(nc_c, per_l, None, it_l - 1, 0)

    def epilogue(z_ref, o_ref, base, n_tiles):
        def body(t, carry):
            off = pl.multiple_of(t * GDN_CHUNK, GDN_CHUNK)
            o = o_s[pl.ds(pl.multiple_of(base + t * GDN_CHUNK, GDN_CHUNK), GDN_CHUNK), :]
            o = o * lax.rsqrt(jnp.mean(o * o, -1, keepdims=True) + 1e-6) * nw_ref[...]
            z = z_ref[0, pl.ds(off, GDN_CHUNK), :].astype(F32)
            o_ref[0, pl.ds(off, GDN_CHUNK), :] = (o * _silu(z)).astype(BF16)
            return carry
        lax.fori_loop(0, n_tiles, body, 0, unroll=2)

    epilogue(zc_ref, oc_ref, 0, lc // GDN_CHUNK)
    epilogue(zl_ref, ol_ref, lc, ll // GDN_CHUNK)


def _gdn(main_c, small_c, main_l, small_l, conv_w, a_log, dt_bias, norm_w):
    b_, lc, _ = main_c.shape
    ll = main_l.shape[1]
    lt = lc + ll
    assert lc % 256 == 0 and ll % 256 == 0
    pad8 = lambda p: jnp.zeros((1, LANES), F32).at[0, :2 * A_HEADS].set(p.astype(F32).reshape(-1))
    cw = conv_w.astype(F32)

    n_seq = b_ * A_HEADS
    pre_seq = lambda s: jnp.minimum(s, n_seq - 1)
    rec_seq = lambda s: jnp.maximum(s - 1, 0)

    def seg(l, w, part, seq):
        return pl.BlockSpec((1, l, w), lambda s: (seq(s) // A_HEADS, 0, part * A_HEADS + seq(s) % A_HEADS))

    def cw_spec(part):
        return pl.BlockSpec((CONV_K, A_DK), lambda s: (0, part * A_HEADS + pre_seq(s) % A_HEADS))

    const = lambda shape: pl.BlockSpec(shape, lambda s: (0,) * len(shape))
    in_specs = []
    for l in (lc, ll):
        in_specs += [seg(l, A_DK, 0, pre_seq), seg(l, A_DK, 1, pre_seq), seg(l, A_DK, 2, pre_seq),
                     pl.BlockSpec((1, l, SMALL_W), lambda s: (pre_seq(s) // A_HEADS, 0, 0)),
                     seg(l, A_DV, 3, rec_seq)]
    in_specs += [cw_spec(0), cw_spec(1), cw_spec(2), const((1, LANES)), const((1, LANES)), const((1, A_DV))]
    tok = lambda dt: pltpu.VMEM((lt, LANES), dt)
    halves = lambda: pltpu.VMEM((2, 2, lt, LANES), BF16)
    return pl.pallas_call(
        functools.partial(_gdn_kernel, lc, ll, n_seq),
        grid=(n_seq + 1,),
        in_specs=in_specs,
        out_specs=[seg(lc, A_DV, 0, rec_seq), seg(ll, A_DV, 0, rec_seq)],
        out_shape=[jax.ShapeDtypeStruct((b_, lc, A_W), BF16), jax.ShapeDtypeStruct((b_, ll, A_W), BF16)],
        scratch_shapes=[pltpu.VMEM((3, min(GDN_PRE_CHUNKS * GDN_CHUNK, max(lc, ll)) + 2 * CONV_HALO, LANES), F32),
                        tok(BF16), tok(BF16), tok(BF16), tok(F32), tok(F32), tok(BF16), tok(BF16),
                        halves(), halves(), halves(), halves(), halves(),
                        pltpu.VMEM((2, 2, (lt // GDN_CHUNK) * 8, LANES), F32),
                        pltpu.VMEM((N_MASKS, GDN_CHUNK, GDN_CHUNK), F32),
                        tok(F32), pltpu.VMEM((2, A_DK, A_DV), F32)],
        compiler_params=pltpu.CompilerParams(vmem_limit_bytes=GDN_VMEM_LIMIT,
                                             dimension_semantics=("arbitrary",)),
        name="gdn",
    )(main_c, main_c, main_c, small_c, main_c, main_l, main_l, main_l, small_l, main_l,
      cw, cw, cw, pad8(a_log), pad8(dt_bias), norm_w.astype(F32).reshape(1, A_DV))


def _permute_w_in(w):
    sizes = (3 * A_W, 4 * A_HEADS, A_W, B_W, B_W, B_W, B_W, C_QW, C_QW, C_VW, 2 * C_RANK, C_VW)
    offs = np.concatenate([[0], np.cumsum(sizes)])
    part = lambda i: w[:, offs[i]:offs[i + 1]]
    pad = jnp.zeros((w.shape[0], SMALL_W - 4 * A_HEADS - 2 * C_RANK), w.dtype)
    order = [part(0), part(2), part(3), part(4), part(5), part(6), part(7), part(8), part(9), part(11),
             part(1), part(10), pad]
    return jnp.concatenate(order, axis=1).astype(BF16)


def kernel(x, c, ctx, c_ctx, w_mod, b_mod, w_in, conv_w, a_log, dt_bias, gdn_norm, rpb,
           gla_w2, gla_b2, gla_norm, w_out, ln_g, ln_b):
    depth = w_mod.shape[0]
    b_ = x.shape[0]
    alpha = (2 * depth) ** 0.25
    n_mod = -(-(b_ + 1) // 8) * 8
    c_all = jnp.zeros((n_mod, D_MODEL), F32).at[:b_].set(c).at[b_].set(c_ctx)
    xl, xc = x, ctx
    for i in range(depth):
        ctx_out = i < depth - 1
        mod = _modulation(c_all, w_mod, b_mod.reshape(depth, 1, -1), i).reshape(n_mod, 3, 1, D_MODEL)
        sh_l, sc_l, gt_l = mod[:b_, 0], mod[:b_, 1], mod[:b_, 2]
        sh_c, sc_c, gt_c = mod[b_:b_ + 1, 0], mod[b_:b_ + 1, 1], mod[b_:b_ + 1, 2]
        w_perm = _permute_w_in(w_in[i])
        main_l, small_l = _project(xl, sh_l, sc_l, w_perm, True)
        main_c, small_c = _project(xc, sh_c, sc_c, w_perm, False)
        ya_c, ya_l = _gdn(main_c, small_c, main_l, small_l, conv_w[i], a_log[i], dt_bias[i], gdn_norm[i])
        yb_l = _na_latent(main_l, main_c, _na_bias_table(rpb[i]))
        yc_c, yc_l = _gla(main_c, small_c, main_l, small_l, gla_w2[i], gla_b2[i], gla_norm[i])
        w_o = w_out[i].astype(BF16)
        if ctx_out:
            yb_c = _na_ctx(main_c)
            xc = _out_project(xc, ya_c, yb_c, yc_c, w_o, gt_c, ln_g[i], ln_b[i], False, alpha)
        xl = _out_project(xl, ya_l, yb_l, yc_l, w_o, gt_l, ln_g[i], ln_b[i], True, alpha)
    return xl
```

```python
import functools
import math

import numpy as np
import jax
import jax.numpy as jnp
from jax import lax
from jax.experimental import pallas as pl
from jax.experimental.pallas import tpu as pltpu

F32 = jnp.float32
BF16 = jnp.bfloat16
HI = lax.Precision.HIGHEST

D_MODEL = 1024
GRID_W = 64
A_HEADS, A_DK, A_DV, CONV_K, A_CHUNK = 4, 128, 128, 5, 64
B_HEADS, B_DH, WIN_R, WIN_C = 4, 64, 8, 16
C_HEADS, C_DK, C_DV, C_RANK, C_CHUNK = 4, 32, 64, 16, 16
C_GATE_NORM = 16.0
ROPE_THETA = 10000.0
LN_EPS = 1e-6
NEG_INF = -1e30

A_W = A_HEADS * A_DK
B_W = B_HEADS * B_DH
C_QW = C_HEADS * C_DK
C_VW = C_HEADS * C_DV
MAIN_W = 3 * A_W + A_W + 4 * B_W + 2 * C_QW + 2 * C_VW
SMALL_W = 128
PROJ_W = MAIN_W + SMALL_W
LANES = 128
VMEM_LIMIT = 56 * 1024 * 1024
GDN_VMEM_LIMIT = 60 * 1024 * 1024
PROJ_TILE = 1024


def _nt(a, b, precision=None):
    return lax.dot_general(a, b, (((1,), (1,)), ((), ())), precision=precision,
                           preferred_element_type=F32)


def _tn(a, b):
    return lax.dot_general(a, b, (((0,), (0,)), ((), ())), preferred_element_type=F32)


def _mm(a, b, precision=None):
    return jnp.dot(a, b, precision=precision, preferred_element_type=F32)


def _mm_01(m01, x, right=False):
    hi = x.astype(BF16)
    lo = (x - hi.astype(F32)).astype(BF16)
    return _mm(hi, m01) + _mm(lo, m01) if right else _mm(m01, hi) + _mm(m01, lo)


def _sigmoid(x):
    return 1.0 / (1.0 + jnp.exp(-x))


def _silu(x):
    return x * _sigmoid(x)


def _softplus(x):
    return jnp.maximum(x, 0.0) + jnp.log(1.0 + jnp.exp(-jnp.abs(x)))


def _iota2(shape, axis):
    return lax.broadcasted_iota(jnp.int32, shape, axis)


def _aligned(x, m):
    return x if isinstance(x, int) else pl.multiple_of(x, m)


def _mod_kernel(c_ref, w_ref, b_ref, o_ref):
    o_ref[...] = _mm(_silu(c_ref[...]), w_ref[...], HI) + b_ref[...]


def _modulation(c_all, w_mod, b_mod, layer):
    n = c_all.shape[0]
    tn = 768
    return pl.pallas_call(
        _mod_kernel,
        grid=(3 * D_MODEL // tn,),
        in_specs=[pl.BlockSpec((n, D_MODEL), lambda j: (0, 0)),
                  pl.BlockSpec((None, D_MODEL, tn), lambda j: (layer, 0, j)),
                  pl.BlockSpec((None, 1, tn), lambda j: (layer, 0, j))],
        out_specs=pl.BlockSpec((n, tn), lambda j: (0, j)),
        out_shape=jax.ShapeDtypeStruct((n, 3 * D_MODEL), F32),
        name="adaln_mod",
    )(c_all, w_mod, b_mod)


def _layer_norm(x):
    mu = jnp.mean(x, -1, keepdims=True)
    xc = x - mu
    var = jnp.mean(xc * xc, -1, keepdims=True)
    return xc * lax.rsqrt(var + LN_EPS)


def _proj_kernel(x_ref, sh_ref, sc_ref, w_ref, o_ref, s_ref):
    m = (_layer_norm(x_ref[0]) * (1.0 + sc_ref[0]) + sh_ref[0]).astype(BF16)
    step = 640
    for c0 in range(0, MAIN_W, step):
        o_ref[0, :, c0:c0 + step] = _mm(m, w_ref[:, c0:c0 + step]).astype(BF16)
    s_ref[0] = _mm(m, w_ref[:, MAIN_W:])


def _project(x, sh, sc, w_perm, layer, per_batch):
    b_, l, _ = x.shape
    t = min(PROJ_TILE, l)
    mod_map = (lambda b, i: (b, 0, 0)) if per_batch else (lambda b, i: (0, 0, 0))
    return pl.pallas_call(
        _proj_kernel,
        grid=(b_, l // t),
        in_specs=[pl.BlockSpec((1, t, D_MODEL), lambda b, i: (b, i, 0)),
                  pl.BlockSpec((1, 1, D_MODEL), mod_map),
                  pl.BlockSpec((1, 1, D_MODEL), mod_map),
                  pl.BlockSpec((None, D_MODEL, PROJ_W), lambda b, i: (layer, 0, 0))],
        out_specs=[pl.BlockSpec((1, t, MAIN_W), lambda b, i: (b, i, 0)),
                   pl.BlockSpec((1, t, SMALL_W), lambda b, i: (b, i, 0))],
        out_shape=[jax.ShapeDtypeStruct((b_, l, MAIN_W), BF16),
                   jax.ShapeDtypeStruct((b_, l, SMALL_W), F32)],
        compiler_params=pltpu.CompilerParams(vmem_limit_bytes=VMEM_LIMIT),
        name="ln_mod_proj",
    )(x, sh, sc, w_perm)


def _out_kernel(alpha, x_ref, ya_ref, yb_ref, yc_ref, w_ref, gt_ref, g_ref, b_ref, o_ref):
    y = jnp.concatenate([ya_ref[0], yb_ref[0], yc_ref[0]], axis=-1)
    h = alpha * x_ref[0] + gt_ref[0] * _mm(y, w_ref[...])
    o_ref[0] = _layer_norm(h) * g_ref[...] + b_ref[...]


def _out_project(x, ya, yb, yc, w_out, layer, gt, ln_g, ln_b, per_batch, alpha):
    b_, l, _ = x.shape
    t = min(PROJ_TILE, l)
    mod_map = (lambda b, i: (b, 0, 0)) if per_batch else (lambda b, i: (0, 0, 0))
    tok = lambda w: pl.BlockSpec((1, t, w), lambda b, i: (b, i, 0))
    return pl.pallas_call(
        functools.partial(_out_kernel, alpha),
        grid=(b_, l // t),
        in_specs=[tok(D_MODEL), tok(A_W), tok(B_W), tok(C_VW),
                  pl.BlockSpec((None, D_MODEL, D_MODEL), lambda b, i: (layer, 0, 0)),
                  pl.BlockSpec((1, 1, D_MODEL), mod_map),
                  pl.BlockSpec((1, D_MODEL), lambda b, i: (0, 0)),
                  pl.BlockSpec((1, D_MODEL), lambda b, i: (0, 0))],
        out_specs=tok(D_MODEL),
        out_shape=jax.ShapeDtypeStruct((b_, l, D_MODEL), F32),
        compiler_params=pltpu.CompilerParams(vmem_limit_bytes=VMEM_LIMIT),
        name="out_proj_postnorm",
    )(x, ya, yb, yc, w_out, gt, ln_g.reshape(1, -1), ln_b.reshape(1, -1))


NA_ROWS_PER_STEP = 8


def _stack_heads(q):
    head = _iota2(q.shape, 1) // B_DH
    return jnp.concatenate([jnp.where(head == h, q, jnp.zeros_like(q)) for h in range(B_HEADS)], axis=0)


def _unstack_heads(o, n):
    head = _iota2((n, B_W), 1) // B_DH
    out = jnp.zeros((n, B_W), F32)
    for h in range(B_HEADS):
        out = jnp.where(head == h, o[h * n:(h + 1) * n], out)
    return out


def _na_kernel(rows, q_ref, k_ref, v_ref, kc_ref, vc_ref, z_ref, bias_ref, o_ref):
    kc = kc_ref[0]
    vc = vc_ref[0]
    n_win = WIN_R * GRID_W
    for rr in range(NA_ROWS_PER_STEP):
        r = pl.program_id(1) * NA_ROWS_PER_STEP + rr
        r0 = jnp.clip(r - WIN_R // 2, 0, rows - WIN_R)
        start = pl.multiple_of(r0 * GRID_W, GRID_W)
        kw = k_ref[0, pl.ds(start, n_win), :]
        vw = v_ref[0, pl.ds(start, n_win), :]
        tok = slice(rr * GRID_W, (rr + 1) * GRID_W)
        qs = _stack_heads(q_ref[0, tok, :] * (B_DH ** -0.5))
        s_win = _nt(qs, kw)
        s_ctx = _nt(qs, kc)
        p_win, p_ctx, inv = [], [], []
        for h in range(B_HEADS):
            blk = slice(h * GRID_W, (h + 1) * GRID_W)
            sw = s_win[blk] + bias_ref[r - r0, blk, :]
            sc = s_ctx[blk]
            m = jnp.maximum(jnp.max(sw, -1, keepdims=True), jnp.max(sc, -1, keepdims=True))
            pw = jnp.exp(sw - m)
            pc = jnp.exp(sc - m)
            inv.append(1.0 / (jnp.sum(pw, -1, keepdims=True) + jnp.sum(pc, -1, keepdims=True)))
            p_win.append(pw.astype(BF16))
            p_ctx.append(pc.astype(BF16))
        o = _mm(jnp.concatenate(p_win, axis=0), vw) + _mm(jnp.concatenate(p_ctx, axis=0), vc)
        o = _unstack_heads(o * jnp.concatenate(inv, axis=0), GRID_W)
        o_ref[0, tok, :] = (o * _silu(z_ref[0, tok, :].astype(F32))).astype(BF16)


def _na_bias_table(rpb):
    cq = np.arange(GRID_W)
    c0 = np.clip(cq - WIN_C // 2, 0, GRID_W - WIN_C)
    col_ok = (cq[None, :] >= c0[:, None]) & (cq[None, :] < c0[:, None] + WIN_C)
    dj = np.clip(cq[None, :] - cq[:, None] + (WIN_C - 1), 0, 2 * WIN_C - 2)
    onehot = (dj[None] == np.arange(2 * WIN_C - 1)[:, None, None]).astype(np.float32)
    t = jnp.einsum('lhdk,kqc->lhdqc', rpb.astype(F32), jnp.asarray(onehot), precision=HI)
    t = jnp.where(col_ok[None, None, None], t, NEG_INF)
    slabs = [jnp.transpose(t[:, :, WIN_R - 1 - v:2 * WIN_R - 1 - v], (0, 1, 3, 2, 4)) for v in range(WIN_R)]
    return jnp.stack(slabs, axis=1).reshape(rpb.shape[0], WIN_R, B_HEADS * GRID_W, WIN_R * GRID_W)


def _na_latent(main_l, main_c, bias, layer):
    b_, s, _ = main_l.shape
    lc = main_c.shape[1]
    rows = s // GRID_W
    assert rows >= WIN_R and rows % NA_ROWS_PER_STEP == 0
    tq = NA_ROWS_PER_STEP * GRID_W
    col = lambda idx: (lambda b, i: (b, 0, idx))
    return pl.pallas_call(
        functools.partial(_na_kernel, rows),
        grid=(b_, rows // NA_ROWS_PER_STEP),
        in_specs=[pl.BlockSpec((1, tq, B_W), lambda b, i: (b, i, 8)),
                  pl.BlockSpec((1, s, B_W), col(9)),
                  pl.BlockSpec((1, s, B_W), col(10)),
                  pl.BlockSpec((1, lc, B_W), col(9)),
                  pl.BlockSpec((1, lc, B_W), col(10)),
                  pl.BlockSpec((1, tq, B_W), lambda b, i: (b, i, 11)),
                  pl.BlockSpec((None,) + bias.shape[1:], lambda b, i: (layer, 0, 0, 0))],
        out_specs=pl.BlockSpec((1, tq, B_W), lambda b, i: (b, i, 0)),
        out_shape=jax.ShapeDtypeStruct((b_, s, B_W), BF16),
        compiler_params=pltpu.CompilerParams(vmem_limit_bytes=VMEM_LIMIT),
        name="na_latent",
    )(main_l, main_l, main_l, main_c, main_c, main_l, bias)


def _na_ctx_kernel(q_ref, k_ref, v_ref, z_ref, o_ref):
    lc = q_ref.shape[1]
    qs = _stack_heads(q_ref[0] * (B_DH ** -0.5))
    s = _nt(qs, k_ref[0])
    p = jnp.exp(s - jnp.max(s, -1, keepdims=True))
    o = _mm(p.astype(BF16), v_ref[0]) / jnp.sum(p, -1, keepdims=True)
    o = _unstack_heads(o, lc)
    o_ref[0] = (o * _silu(z_ref[0].astype(F32))).astype(BF16)


def _na_ctx(main_c):
    b_, lc, _ = main_c.shape
    col = lambda idx: pl.BlockSpec((1, lc, B_W), lambda b: (b, 0, idx))
    return pl.pallas_call(
        _na_ctx_kernel,
        grid=(b_,),
        in_specs=[col(8), col(9), col(10), col(11)],
        out_specs=pl.BlockSpec((1, lc, B_W), lambda b: (b, 0, 0)),
        out_shape=jax.ShapeDtypeStruct((b_, lc, B_W), BF16),
        name="na_ctx",
    )(main_c, main_c, main_c, main_c)


GLA_TILE = 128
GLA_INTRA_CHUNKS = 2
GLA_SCAN_STEPS = 8


def _bwd_chunk(n, nc_c, nc_l):
    return jnp.where(n < nc_c, nc_c - 1 - n, nc_c + nc_l - 1 - (n - nc_c))


def _gla_kernel(lc, ll,
                qc_ref, kc_ref, vc_ref, rc_ref, zc_ref, ql_ref, kl_ref, vl_ref, rl_ref, zl_ref,
                cos_ref, sin_ref, w2f_ref, w2b_ref, b2_ref, nw_ref, oc_ref, ol_ref,
                q_s, k_s, v_s, bf_s, bb_s, qdf_s, kdf_s, qdb_s, kdb_s, glf_s, glb_s, of_s, ob_s, st_s):
    c = C_CHUNK
    nc_c, nc_l = lc // c, ll // c
    per_tile = GLA_TILE // c
    scale = C_DK ** -0.5

    ti = _iota2((GLA_TILE, GLA_TILE), 0)
    tj = _iota2((GLA_TILE, GLA_TILE), 1)
    same = (ti // c) == (tj // c)
    tri_l = jnp.where(same & (tj <= ti), 1.0, 0.0).astype(BF16)
    tri_u = jnp.where(same & (tj >= ti), 1.0, 0.0).astype(BF16)
    rot = (jnp.where((tj % 16 < 8) & (ti == tj + 8), -1.0, 0.0)
           + jnp.where((tj % 16 >= 8) & (ti == tj - 8), 1.0, 0.0)).astype(BF16)
    expand = jnp.where(_iota2((C_QW, C_VW), 0) // C_DK == _iota2((C_QW, C_VW), 1) // C_DV,
                       1.0, 0.0).astype(BF16)
    st_mask = jnp.where(_iota2((C_VW, C_QW), 0) // C_DV == _iota2((C_VW, C_QW), 1) // C_DK,
                        1.0, 0.0).astype(F32)
    head_ones = jnp.where(_iota2((C_VW, C_VW), 0) // C_DV == _iota2((C_VW, C_VW), 1) // C_DV,
                          1.0, 0.0).astype(BF16)

    def log_sigmoid(x):
        return jnp.minimum(x, 0.0) - jnp.log(1.0 + jnp.exp(-jnp.abs(x)))

    def prologue(q_ref, k_ref, v_ref, r_ref, base, n_tiles, rope):
        def body(t, carry):
            off = pl.multiple_of(t * GLA_TILE, GLA_TILE)
            uoff = pl.multiple_of(base + t * GLA_TILE, GLA_TILE)
            qb = q_ref[0, pl.ds(off, GLA_TILE), :]
            kb = k_ref[0, pl.ds(off, GLA_TILE), :]
            q = qb.astype(F32)
            k = kb.astype(F32)
            if rope:
                cs = cos_ref[pl.ds(off, GLA_TILE), :]
                sn = sin_ref[pl.ds(off, GLA_TILE), :]
                q = q * cs + _mm(qb, rot) * sn
                k = k * cs + _mm(kb, rot) * sn
            q = q * scale
            r = r_ref[0, pl.ds(off, GLA_TILE), :].astype(BF16)
            gkf = log_sigmoid(_mm(r, w2f_ref[...]) + b2_ref[0:1, :]) / C_GATE_NORM
            gkb = log_sigmoid(_mm(r, w2b_ref[...]) + b2_ref[1:2, :]) / C_GATE_NORM
            bf = _mm_01(tri_l, gkf)
            bb = _mm_01(tri_u, gkb)
            tf = jnp.concatenate([jnp.broadcast_to(bf[i * c + c - 1:i * c + c], (c, C_QW)) for i in range(per_tile)], axis=0)
            tb = jnp.concatenate([jnp.broadcast_to(bb[i * c:i * c + 1], (c, C_QW)) for i in range(per_tile)], axis=0)
            sl = pl.ds(uoff, GLA_TILE)
            q_s[sl, :] = q.astype(BF16)
            k_s[sl, :] = k.astype(BF16)
            v_s[sl, :] = v_ref[0, pl.ds(off, GLA_TILE), :]
            bf_s[sl, :] = bf
            bb_s[sl, :] = bb
            qdf_s[sl, :] = (q * jnp.exp(bf)).astype(BF16)
            kdf_s[sl, :] = (k * jnp.exp(tf - bf)).astype(BF16)
            qdb_s[sl, :] = (q * jnp.exp(bb)).astype(BF16)
            kdb_s[sl, :] = (k * jnp.exp(tb - bb)).astype(BF16)
            goff = pl.multiple_of((base // c + t * per_tile) * 8, 8)
            glf_s[pl.ds(goff, per_tile * 8), :] = jnp.exp(
                jnp.concatenate([tf[i * c:i * c + 8] for i in range(per_tile)], axis=0))
            glb_s[pl.ds(goff, per_tile * 8), :] = jnp.exp(
                jnp.concatenate([tb[i * c:i * c + 8] for i in range(per_tile)], axis=0))
            return carry
        lax.fori_loop(0, n_tiles, body, 0, unroll=min(4, n_tiles))

    prologue(qc_ref, kc_ref, vc_ref, rc_ref, 0, lc // GLA_TILE, False)
    prologue(ql_ref, kl_ref, vl_ref, rl_ref, lc, ll // GLA_TILE, True)

    jj = _iota2((c, C_QW), 0)
    row_sum = jnp.where(_iota2((c, c * c), 1) // c == _iota2((c, c * c), 0), 1.0, 0.0).astype(BF16)

    def intra(first_chunk):
        offs = [pl.multiple_of((first_chunk + j) * c, c) for j in range(GLA_INTRA_CHUNKS)]
        ps = []
        for off in offs:
            q = q_s[pl.ds(off, c), :].astype(F32)
            k = k_s[pl.ds(off, c), :].astype(F32)
            bf = bf_s[pl.ds(off, c), :]
            bb = bb_s[pl.ds(off, c), :]
            rows = []
            for i in range(c):
                e = jnp.exp(jnp.where(jj < i, bf[i:i + 1] - bf, bb[i:i + 1] - bb))
                rows.append((jnp.where(jj == i, 2.0, e) * (q[i:i + 1] * k)).astype(BF16))
            ps.append(jnp.concatenate(rows, axis=0))
        return [(off, _mm(p, expand)) for off, p in zip(offs, ps)]

    def intra_apply(weights):
        outs = []
        for off, a in weights:
            v = v_s[pl.ds(off, c), :].astype(F32)
            av = (a.reshape(c, c, C_VW) * v[None]).reshape(c * c, C_VW).astype(BF16)
            outs.append(_mm(row_sum, av))
        return outs

    st_s[...] = jnp.zeros_like(st_s)
    dirs = ((qdf_s, kdf_s, glf_s), (qdb_s, kdb_s, glb_s))

    def scan(it, carry):
        steps = [it * GLA_SCAN_STEPS + s for s in range(GLA_SCAN_STEPS)]
        chunk = [steps, [_bwd_chunk(n, nc_c, nc_l) for n in steps]]
        sls = [[pl.ds(pl.multiple_of(ci * c, c), c) for ci in chunk[d]] for d in range(2)]
        upd = [[_tn(v_s[sl, :], dirs[d][1][sl, :]) for sl in sls[d]] for d in range(2)]
        weights = []
        for s in range(0, GLA_SCAN_STEPS, GLA_INTRA_CHUNKS):
            weights += intra(steps[s])
        st = [st_s[0], st_s[1]]
        inter = [[], []]
        for s in range(GLA_SCAN_STEPS):
            for d in range(2):
                inter[d].append(_nt(dirs[d][0][sls[d][s], :], st[d].astype(BF16)))
                gl = dirs[d][2][pl.ds(pl.multiple_of(chunk[d][s] * 8, 8), 8), :][0:1]
                st[d] = st[d] * gl + upd[d][s] * st_mask
        st_s[0] = st[0]
        st_s[1] = st[1]
        local = intra_apply(weights)
        for s in range(GLA_SCAN_STEPS):
            of_s[sls[0][s], :] = local[s] + inter[0][s]
            ob_s[sls[1][s], :] = inter[1][s]
        return carry
    lax.fori_loop(0, (nc_c + nc_l) // GLA_SCAN_STEPS, scan, 0)

    def epilogue(z_ref, o_ref, base, n_tiles):
        def body(t, carry):
            off = pl.multiple_of(t * GLA_TILE, GLA_TILE)
            sl = pl.ds(pl.multiple_of(base + t * GLA_TILE, GLA_TILE), GLA_TILE)
            o = of_s[sl, :] + ob_s[sl, :]
            sq = o * o
            hi = sq.astype(BF16)
            lo = (sq - hi.astype(F32)).astype(BF16)
            ms = (_mm(hi, head_ones) + _mm(lo, head_ones)) * (1.0 / C_DV)
            o = o * lax.rsqrt(ms + 1e-6) * nw_ref[...]
            z = z_ref[0, pl.ds(off, GLA_TILE), :].astype(F32)
            o_ref[0, pl.ds(off, GLA_TILE), :] = (o * _silu(z)).astype(BF16)
            return carry
        lax.fori_loop(0, n_tiles, body, 0, unroll=2)

    epilogue(zc_ref, oc_ref, 0, lc // GLA_TILE)
    epilogue(zl_ref, ol_ref, lc, ll // GLA_TILE)


def _rope_tables(s):
    t = np.arange(s)
    nf = C_DK // 4
    inv = ROPE_THETA ** (-np.arange(nf, dtype=np.float64) / nf)
    lane = np.arange(C_DK)
    pos = np.where((lane // (C_DK // 2))[None, :] == 0, (t // GRID_W)[:, None], (t % GRID_W)[:, None])
    ang = pos.astype(np.float32) * inv.astype(np.float32)[lane % nf][None, :]
    cos = np.tile(np.cos(ang).astype(np.float32), (1, C_HEADS))
    sin = np.tile(np.sin(ang).astype(np.float32), (1, C_HEADS))
    return jnp.asarray(cos), jnp.asarray(sin)


def _gla(main_c, small_c, main_l, small_l, w2, b2, norm_w):
    b_, lc, _ = main_c.shape
    ll = main_l.shape[1]
    lt = lc + ll
    assert lc % GLA_TILE == 0 and ll % GLA_TILE == 0
    cos, sin = _rope_tables(ll)
    w2f = jnp.zeros((SMALL_W, C_QW), BF16).at[16:16 + C_RANK].set(w2[0].astype(BF16))
    w2b = jnp.zeros((SMALL_W, C_QW), BF16).at[16 + C_RANK:16 + 2 * C_RANK].set(w2[1].astype(BF16))
    nw = jnp.tile(norm_w.astype(F32), C_HEADS).reshape(1, C_VW)
    nchunks = lt // C_CHUNK

    def seg(l, w, idx):
        return pl.BlockSpec((1, l, w), lambda b: (b, 0, idx))

    def full(shape):
        return pl.BlockSpec(shape, lambda b: (0,) * len(shape))

    return pl.pallas_call(
        functools.partial(_gla_kernel, lc, ll),
        grid=(b_,),
        in_specs=[seg(lc, C_QW, 24), seg(lc, C_QW, 25), seg(lc, C_VW, 13), seg(lc, SMALL_W, 0), seg(lc, C_VW, 14),
                  seg(ll, C_QW, 24), seg(ll, C_QW, 25), seg(ll, C_VW, 13), seg(ll, SMALL_W, 0), seg(ll, C_VW, 14),
                  full((ll, C_QW)), full((ll, C_QW)), full((SMALL_W, C_QW)), full((SMALL_W, C_QW)),
                  full((2, C_QW)), full((1, C_VW))],
        out_specs=[seg(lc, C_VW, 0), seg(ll, C_VW, 0)],
        out_shape=[jax.ShapeDtypeStruct((b_, lc, C_VW), BF16), jax.ShapeDtypeStruct((b_, ll, C_VW), BF16)],
        scratch_shapes=[pltpu.VMEM((lt, C_QW), BF16), pltpu.VMEM((lt, C_QW), BF16), pltpu.VMEM((lt, C_VW), BF16),
                        pltpu.VMEM((lt, C_QW), F32), pltpu.VMEM((lt, C_QW), F32),
                        pltpu.VMEM((lt, C_QW), BF16), pltpu.VMEM((lt, C_QW), BF16),
                        pltpu.VMEM((lt, C_QW), BF16), pltpu.VMEM((lt, C_QW), BF16),
                        pltpu.VMEM((nchunks * 8, C_QW), F32), pltpu.VMEM((nchunks * 8, C_QW), F32),
                        pltpu.VMEM((lt, C_VW), F32), pltpu.VMEM((lt, C_VW), F32),
                        pltpu.VMEM((2, C_VW, C_QW), F32)],
        compiler_params=pltpu.CompilerParams(vmem_limit_bytes=VMEM_LIMIT),
        name="gla",
    )(main_c, main_c, main_c, small_c, main_c, main_l, main_l, main_l, small_l, main_l,
      cos, sin, w2f, w2b, b2.astype(F32), nw)


GDN_CHUNK = 128
GDN_BASE = 16
GDN_LEVELS = 3
M_INCL, M_STRICT, M_BASE, M_LEVEL, M_EYE = 0, 2, 4, 5, 8
N_MASKS = 9
GDN_PRE_CHUNKS = 4
GDN_REC_FIRST_STAGE = 2
CONV_HALO = 16
CONV_TILE = 256


def _gdn_kernel(lc, ll, n_seq,
                xqc_ref, xkc_ref, xvc_ref, sc_ref, zc_ref, xql_ref, xkl_ref, xvl_ref, sl_ref, zl_ref,
                cwq_ref, cwk_ref, cwv_ref, alog_ref, dtb_ref, nw_ref, oc_ref, ol_ref,
                pad_s, q_s, k_s, v_s, gf_s, gb_s, bef_s, beb_s, u_s, w_s, qd_s, kdt_s, qk_s, gl_s, msk_s, o_s, st_s):
    c = GDN_CHUNK
    nc_c, nc_l = lc // c, ll // c
    step = pl.program_id(0)
    h = jnp.minimum(step, n_seq - 1) % A_HEADS
    cur = step % 2
    prev = 1 - cur

    @pl.when(step == 0)
    def _():
        for buf in (u_s, w_s, qd_s, kdt_s, qk_s, gl_s):
            buf[1] = jnp.zeros(buf.shape[1:], buf.dtype)

    def conv_group(parts, seg_base, l, g0, n_tok):
        t = CONV_TILE
        for p, (x_ref, w_ref, dst, mode) in enumerate(parts):
            lo = _aligned(jnp.maximum(g0 - CONV_HALO, 0), CONV_HALO)
            hi = _aligned(jnp.minimum(g0 + n_tok, l - CONV_HALO), CONV_HALO)
            left = x_ref[0, pl.ds(lo, CONV_HALO), :].astype(F32)
            right = x_ref[0, pl.ds(hi, CONV_HALO), :].astype(F32)
            pad_s[p, 0:CONV_HALO, :] = jnp.where(g0 > 0, left, 0.0)
            pad_s[p, CONV_HALO:CONV_HALO + n_tok, :] = x_ref[0, pl.ds(_aligned(g0, n_tok), n_tok), :].astype(F32)
            pad_s[p, CONV_HALO + n_tok:2 * CONV_HALO + n_tok, :] = jnp.where(g0 + n_tok < l, right, 0.0)
            w = w_ref[...]
            for t0 in range(0, n_tok, t):
                acc = jnp.zeros((t, LANES), F32)
                for j in range(CONV_K):
                    r0 = CONV_HALO - CONV_K // 2 + j + t0
                    acc = acc + pad_s[p, r0:r0 + t, :] * w[j:j + 1]
                y = _silu(acc)
                if mode != "v":
                    y = y * lax.rsqrt(jnp.sum(y * y, -1, keepdims=True) + 1e-6)
                if mode == "q":
                    y = y * (A_DK ** -0.5)
                dst[pl.ds(_aligned(seg_base + g0 + t0, t), t), :] = y.astype(dst.dtype)

    parts_c = ((xqc_ref, cwq_ref, q_s, "q"), (xkc_ref, cwk_ref, k_s, "k"), (xvc_ref, cwv_ref, v_s, "v"))
    parts_l = ((xql_ref, cwq_ref, q_s, "q"), (xkl_ref, cwk_ref, k_s, "k"), (xvl_ref, cwv_ref, v_s, "v"))

    ti = _iota2((c, c), 0)
    tj = _iota2((c, c), 1)
    as_f32 = lambda m: jnp.where(m, 1.0, 0.0).astype(F32)
    tri_l = as_f32(tj <= ti)
    tri_u = as_f32(tj >= ti)
    msk_s[M_INCL] = tri_l
    msk_s[M_INCL + 1] = tri_u
    msk_s[M_STRICT] = as_f32(tj < ti)
    msk_s[M_STRICT + 1] = as_f32(tj > ti)
    msk_s[M_BASE] = -as_f32(ti // GDN_BASE == tj // GDN_BASE)
    for lv in range(GDN_LEVELS):
        s = GDN_BASE << lv
        msk_s[M_LEVEL + lv] = as_f32((ti // (2 * s) == tj // (2 * s)) & (ti // s != tj // s))
    msk_s[M_EYE] = as_f32(ti == tj)
    n_gate = 4 * A_HEADS
    neg_a = -jnp.exp(alog_ref[...])
    chan = _iota2((n_gate, c), 0)
    tri_lb, tri_ub = tri_l.astype(BF16), tri_u.astype(BF16)

    def col_bcast(rows, idx):
        row = jnp.sum(jnp.where(chan == idx, rows, 0.0), axis=0, keepdims=True)
        return jnp.broadcast_to(row, (c, c)).T

    def gates_group(s_ref, seg_base, g0, n_tok):
        for t0 in range(0, n_tok, c):
            xt = s_ref[0, pl.ds(_aligned(g0 + t0, c), c), :].T[:n_gate]
            g = jnp.where(chan < 2 * A_HEADS, neg_a * _softplus(xt + dtb_ref[...]), _sigmoid(xt))
            cum_f = _mm_01(tri_ub, g, right=True)
            cum_b = _mm_01(tri_lb, g, right=True)
            sl = pl.ds(_aligned(seg_base + g0 + t0, c), c)
            gf_s[sl, :] = col_bcast(cum_f, h)
            gb_s[sl, :] = col_bcast(cum_b, h + A_HEADS)
            bef_s[sl, :] = col_bcast(g, h + 2 * A_HEADS).astype(BF16)
            beb_s[sl, :] = col_bcast(g, h + 3 * A_HEADS).astype(BF16)

    def front_end(seg, g0, n_tok):
        parts, s_ref, base, l = ((parts_c, sc_ref, 0, lc), (parts_l, sl_ref, lc, ll))[seg]
        conv_group(parts, base, l, g0, n_tok)
        gates_group(s_ref, base, g0, n_tok)

    def prepass(first_chunk, per_step, it):
        chains = []
        for j in range(per_step):
            ci = first_chunk + it * per_step + j
            sl = pl.ds(_aligned(ci * c, c), c)
            kbf = k_s[sl, :]
            q = q_s[sl, :].astype(F32)
            k = kbf.astype(F32)
            gram = _nt(kbf, kbf)
            qk_raw = _nt(q_s[sl, :], kbf)
            for d, (g_s, be_s) in enumerate(((gf_s, bef_s), (gb_s, beb_s))):
                chains.append(dict(ci=ci, sl=sl, d=d, q=q, k=k, v=v_s[sl, :].astype(F32), gram=gram, qk_raw=qk_raw,
                                   gc=g_s[sl, :], beta=be_s[sl, :].astype(F32)))
        yield
        for ch in chains:
            gc = ch["gc"]
            ch["decay"] = jnp.exp(jnp.minimum(gc - gc.T, 0.0)) * msk_s[M_INCL + ch["d"]]
            ch["a"] = ch["gram"] * ch["beta"] * ch["decay"] * msk_s[M_STRICT + ch["d"]]
            n0 = ch["a"] * msk_s[M_BASE]
            ch["inv"] = msk_s[M_EYE] + n0
            ch["pw"] = n0.astype(BF16)
        yield
        n_sq = GDN_BASE.bit_length() - 2
        for ch in chains:
            ch["pw"] = _mm(ch["pw"], ch["pw"]).astype(BF16)
        yield
        for m in range(n_sq):
            for ch in chains:
                ch["inv_next"] = ch["inv"] + _mm(ch["inv"].astype(BF16), ch["pw"])
            if m + 1 < n_sq:
                for ch in chains:
                    ch["pw"] = _mm(ch["pw"], ch["pw"]).astype(BF16)
            for ch in chains:
                ch["inv"] = ch["inv_next"]
            yield
        for lv in range(GDN_LEVELS):
            for ch in chains:
                ch["invb"] = ch["inv"].astype(BF16)
                ch["t"] = _mm((ch["a"] * msk_s[M_LEVEL + lv]).astype(BF16), ch["invb"]).astype(BF16)
            yield
            for ch in chains:
                ch["inv"] = ch["inv"] - _mm(ch["invb"], ch["t"])
            yield
        for ch in chains:
            gc, beta, q, k, d, sl = ch["gc"], ch["beta"], ch["q"], ch["k"], ch["d"], ch["sl"]
            eg = jnp.exp(gc)
            rhs = jnp.concatenate([ch["v"] * beta, k * beta * eg], axis=-1).astype(BF16)
            sol = _mm(ch["inv"].astype(BF16), rhs)
            g_last = gc[c - 1:c] if d == 0 else gc[0:1]
            u_s[cur, d, sl, :] = sol[:, :A_DV].astype(BF16)
            w_s[cur, d, sl, :] = sol[:, A_DV:].astype(BF16)
            qk_s[cur, d, sl, :] = (ch["qk_raw"] * ch["decay"]).astype(BF16)
            qd_s[cur, d, sl, :] = (q * eg).astype(BF16)
            kdt_s[cur, d, sl, :] = (k * jnp.exp(g_last - gc)).T.astype(BF16)
            gl_s[cur, d, pl.ds(_aligned(ch["ci"] * 8, 8), 8), :] = jnp.exp(
                jnp.broadcast_to(g_last, (8, LANES)))

    def recurrence(first_step, n_steps, it):
        st = [st_s[0], st_s[1]]
        for j in range(n_steps):
            n = first_step + it * n_steps + j
            cis = (n, _bwd_chunk(n, nc_c, nc_l))
            sls = [pl.ds(_aligned(ci * c, c), c) for ci in cis]
            stb = [x.astype(BF16) for x in st]
            w_st = [_mm(w_s[prev, d, sls[d], :], stb[d]) for d in range(2)]
            o_st = [_mm(qd_s[prev, d, sls[d], :], stb[d]) for d in range(2)]
            yield
            vnb = [(u_s[prev, d, sls[d], :].astype(F32) - w_st[d]).astype(BF16) for d in range(2)]
            for d in range(2):
                gl = gl_s[prev, d, pl.ds(_aligned(cis[d] * 8, 8), 8), :][0:1]
                st[d] = st[d] * gl + _mm(kdt_s[prev, d, sls[d], :], vnb[d])
            for d in range(2):
                o_s[sls[d], :] += o_st[d] + _mm(qk_s[prev, d, sls[d], :], vnb[d])
            yield
        st_s[0] = st[0]
        st_s[1] = st[1]

    o_s[...] = jnp.zeros_like(o_s)
    st_s[...] = jnp.zeros_like(st_s)

    def merged(first, per_step, next_group, it, carry):
        pre = prepass(first, per_step, it)
        rec = recurrence(first, per_step, it)
        for i, _ in enumerate(pre):
            if i == 0 and next_group is not None:
                front_end(*next_group(it))
            if i >= GDN_REC_FIRST_STAGE:
                next(rec, None)
        for _ in rec:
            pass
        return carry

    per_c = math.gcd(nc_c, GDN_PRE_CHUNKS)
    per_l = math.gcd(nc_l, GDN_PRE_CHUNKS)
    it_c, it_l = nc_c // per_c, nc_l // per_l
    front_end(0, 0, per_c * c)
    if it_c > 1:
        lax.fori_loop(0, it_c - 1, functools.partial(
            merged, 0, per_c, lambda it: (0, (it + 1) * (per_c * c), per_c * c)), 0)
    merged(0, per_c, lambda it: (1, 0, per_l * c), it_c - 1, 0)
    if it_l > 1:
        lax.fori_loop(0, it_l - 1, functools.partial(
            merged, nc_c, per_l, lambda it: (1, (it + 1) * (per_l * c), per_l * c)), 0)
    merged(nc_c, per_l, None, it_l - 1, 0)

    def epilogue(z_ref, o_ref, base, n_tiles):
        def body(t, carry):
            off = pl.multiple_of(t * GDN_CHUNK, GDN_CHUNK)
            o = o_s[pl.ds(pl.multiple_of(base + t * GDN_CHUNK, GDN_CHUNK), GDN_CHUNK), :]
            o = o * lax.rsqrt(jnp.mean(o * o, -1, keepdims=True) + 1e-6) * nw_ref[...]
            z = z_ref[0, pl.ds(off, GDN_CHUNK), :].astype(F32)
            o_ref[0, pl.ds(off, GDN_CHUNK), :] = (o * _silu(z)).astype(BF16)
            return carry
        lax.fori_loop(0, n_tiles, body, 0, unroll=2)

    epilogue(zc_ref, oc_ref, 0, lc // GDN_CHUNK)
    epilogue(zl_ref, ol_ref, lc, ll // GDN_CHUNK)


def _gdn(main_c, small_c, main_l, small_l, conv_w, a_log, dt_bias, norm_w):
    b_, lc, _ = main_c.shape
    ll = main_l.shape[1]
    lt = lc + ll
    assert lc % 256 == 0 and ll % 256 == 0
    n_gate = 4 * A_HEADS
    pad8 = lambda p: jnp.broadcast_to(
        jnp.zeros((n_gate,), F32).at[:2 * A_HEADS].set(p.astype(F32).reshape(-1))[:, None], (n_gate, GDN_CHUNK))
    cw = conv_w.astype(F32)

    n_seq = b_ * A_HEADS
    pre_seq = lambda s: jnp.minimum(s, n_seq - 1)
    rec_seq = lambda s: jnp.maximum(s - 1, 0)

    def seg(l, w, part, seq):
        return pl.BlockSpec((1, l, w), lambda s: (seq(s) // A_HEADS, 0, part * A_HEADS + seq(s) % A_HEADS))

    def cw_spec(part):
        return pl.BlockSpec((CONV_K, A_DK), lambda s: (0, part * A_HEADS + pre_seq(s) % A_HEADS))

    const = lambda shape: pl.BlockSpec(shape, lambda s: (0,) * len(shape))
    in_specs = []
    for l in (lc, ll):
        in_specs += [seg(l, A_DK, 0, pre_seq), seg(l, A_DK, 1, pre_seq), seg(l, A_DK, 2, pre_seq),
                     pl.BlockSpec((1, l, SMALL_W), lambda s: (pre_seq(s) // A_HEADS, 0, 0)),
                     seg(l, A_DV, 3, rec_seq)]
    in_specs += [cw_spec(0), cw_spec(1), cw_spec(2), const((n_gate, GDN_CHUNK)), const((n_gate, GDN_CHUNK)),
                 const((1, A_DV))]
    tok = lambda dt: pltpu.VMEM((lt, LANES), dt)
    halves = lambda: pltpu.VMEM((2, 2, lt, LANES), BF16)
    return pl.pallas_call(
        functools.partial(_gdn_kernel, lc, ll, n_seq),
        grid=(n_seq + 1,),
        in_specs=in_specs,
        out_specs=[seg(lc, A_DV, 0, rec_seq), seg(ll, A_DV, 0, rec_seq)],
        out_shape=[jax.ShapeDtypeStruct((b_, lc, A_W), BF16), jax.ShapeDtypeStruct((b_, ll, A_W), BF16)],
        scratch_shapes=[pltpu.VMEM((3, min(GDN_PRE_CHUNKS * GDN_CHUNK, max(lc, ll)) + 2 * CONV_HALO, LANES), F32),
                        tok(BF16), tok(BF16), tok(BF16), tok(F32), tok(F32), tok(BF16), tok(BF16),
                        halves(), halves(), halves(), halves(), halves(),
                        pltpu.VMEM((2, 2, (lt // GDN_CHUNK) * 8, LANES), F32),
                        pltpu.VMEM((N_MASKS, GDN_CHUNK, GDN_CHUNK), F32),
                        tok(F32), pltpu.VMEM((2, A_DK, A_DV), F32)],
        compiler_params=pltpu.CompilerParams(vmem_limit_bytes=GDN_VMEM_LIMIT,
                                             dimension_semantics=("arbitrary",)),
        name="gdn",
    )(main_c, main_c, main_c, small_c, main_c, main_l, main_l, main_l, small_l, main_l,
      cw, cw, cw, pad8(a_log), pad8(dt_bias), norm_w.astype(F32).reshape(1, A_DV))


def _permute_w_in(w):
    sizes = (3 * A_W, 4 * A_HEADS, A_W, B_W, B_W, B_W, B_W, C_QW, C_QW, C_VW, 2 * C_RANK, C_VW)
    offs = np.concatenate([[0], np.cumsum(sizes)])
    part = lambda i: w[..., offs[i]:offs[i + 1]]
    pad = jnp.zeros(w.shape[:-1] + (SMALL_W - 4 * A_HEADS - 2 * C_RANK,), w.dtype)
    order = [part(0), part(2), part(3), part(4), part(5), part(6), part(7), part(8), part(9), part(11),
             part(1), part(10), pad]
    return jnp.concatenate(order, axis=-1).astype(BF16)


def kernel(x, c, ctx, c_ctx, w_mod, b_mod, w_in, conv_w, a_log, dt_bias, gdn_norm, rpb,
           gla_w2, gla_b2, gla_norm, w_out, ln_g, ln_b):
    depth = w_mod.shape[0]
    b_ = x.shape[0]
    alpha = (2 * depth) ** 0.25
    n_mod = -(-(b_ + 1) // 8) * 8
    c_all = jnp.zeros((n_mod, D_MODEL), F32).at[:b_].set(c).at[b_].set(c_ctx)
    w_perm = _permute_w_in(w_in)
    w_o = w_out.astype(BF16)
    na_bias = _na_bias_table(rpb)
    b_mod3 = b_mod.reshape(depth, 1, -1)
    xl, xc = x, ctx
    for i in range(depth):
        ctx_out = i < depth - 1
        mod = _modulation(c_all, w_mod, b_mod3, i).reshape(n_mod, 3, 1, D_MODEL)
        sh_l, sc_l, gt_l = mod[:b_, 0], mod[:b_, 1], mod[:b_, 2]
        sh_c, sc_c, gt_c = mod[b_:b_ + 1, 0], mod[b_:b_ + 1, 1], mod[b_:b_ + 1, 2]
        main_l, small_l = _project(xl, sh_l, sc_l, w_perm, i, True)
        main_c, small_c = _project(xc, sh_c, sc_c, w_perm, i, False)
        ya_c, ya_l = _gdn(main_c, small_c, main_l, small_l, conv_w[i], a_log[i], dt_bias[i], gdn_norm[i])
        yb_l = _na_latent(main_l, main_c, na_bias, i)
        yc_c, yc_l = _gla(main_c, small_c, main_l, small_l, gla_w2[i], gla_b2[i], gla_norm[i])
        if ctx_out:
            yb_c = _na_ctx(main_c)
            xc = _out_project(xc, ya_c, yb_c, yc_c, w_o, i, gt_c, ln_g[i], ln_b[i], False, alpha)
        xl = _out_project(xl, ya_l, yb_l, yc_l, w_o, i, gt_l, ln_g[i], ln_b[i], True, alpha)
    return xl
```

```python
import functools
import math

import numpy as np
import jax
import jax.numpy as jnp
from jax import lax
from jax.experimental import pallas as pl
from jax.experimental.pallas import tpu as pltpu

F32 = jnp.float32
BF16 = jnp.bfloat16
HI = lax.Precision.HIGHEST

D_MODEL = 1024
GRID_W = 64
A_HEADS, A_DK, A_DV, CONV_K, A_CHUNK = 4, 128, 128, 5, 64
B_HEADS, B_DH, WIN_R, WIN_C = 4, 64, 8, 16
C_HEADS, C_DK, C_DV, C_RANK, C_CHUNK = 4, 32, 64, 16, 16
C_GATE_NORM = 16.0
ROPE_THETA = 10000.0
LN_EPS = 1e-6
NEG_INF = -1e30

A_W = A_HEADS * A_DK
B_W = B_HEADS * B_DH
C_QW = C_HEADS * C_DK
C_VW = C_HEADS * C_DV
MAIN_W = 3 * A_W + A_W + 4 * B_W + 2 * C_QW + 2 * C_VW
SMALL_W = 128
PROJ_W = MAIN_W + SMALL_W
LANES = 128
VMEM_LIMIT = 56 * 1024 * 1024
GDN_VMEM_LIMIT = 60 * 1024 * 1024
PROJ_TILE = 1024


def _nt(a, b, precision=None):
    return lax.dot_general(a, b, (((1,), (1,)), ((), ())), precision=precision,
                           preferred_element_type=F32)


def _tn(a, b):
    return lax.dot_general(a, b, (((0,), (0,)), ((), ())), preferred_element_type=F32)


def _mm(a, b, precision=None):
    return jnp.dot(a, b, precision=precision, preferred_element_type=F32)


def _mm_01(m01, x, right=False):
    hi = x.astype(BF16)
    lo = (x - hi.astype(F32)).astype(BF16)
    return _mm(hi, m01) + _mm(lo, m01) if right else _mm(m01, hi) + _mm(m01, lo)


def _sigmoid(x):
    return 1.0 / (1.0 + jnp.exp(-x))


def _silu(x):
    return x * _sigmoid(x)


def _softplus(x):
    return jnp.maximum(x, 0.0) + jnp.log(1.0 + jnp.exp(-jnp.abs(x)))


def _iota2(shape, axis):
    return lax.broadcasted_iota(jnp.int32, shape, axis)


def _aligned(x, m):
    return x if isinstance(x, int) else pl.multiple_of(x, m)


def _mod_kernel(c_ref, w_ref, b_ref, o_ref):
    o_ref[...] = _mm(_silu(c_ref[...]), w_ref[...], HI) + b_ref[...]


def _modulation(c_all, w_mod, b_mod, layer):
    n = c_all.shape[0]
    tn = 768
    return pl.pallas_call(
        _mod_kernel,
        grid=(3 * D_MODEL // tn,),
        in_specs=[pl.BlockSpec((n, D_MODEL), lambda j: (0, 0)),
                  pl.BlockSpec((None, D_MODEL, tn), lambda j: (layer, 0, j)),
                  pl.BlockSpec((None, 1, tn), lambda j: (layer, 0, j))],
        out_specs=pl.BlockSpec((n, tn), lambda j: (0, j)),
        out_shape=jax.ShapeDtypeStruct((n, 3 * D_MODEL), F32),
        name="adaln_mod",
    )(c_all, w_mod, b_mod)


def _layer_norm(x):
    mu = jnp.mean(x, -1, keepdims=True)
    xc = x - mu
    var = jnp.mean(xc * xc, -1, keepdims=True)
    return xc * lax.rsqrt(var + LN_EPS)


def _proj_kernel(x_ref, sh_ref, sc_ref, w_ref, o_ref, s_ref):
    m = (_layer_norm(x_ref[0]) * (1.0 + sc_ref[0]) + sh_ref[0]).astype(BF16)
    step = 640
    for c0 in range(0, MAIN_W, step):
        o_ref[0, :, c0:c0 + step] = _mm(m, w_ref[:, c0:c0 + step]).astype(BF16)
    s_ref[0] = _mm(m, w_ref[:, MAIN_W:])


def _project(x, sh, sc, w_perm, layer, per_batch):
    b_, l, _ = x.shape
    t = min(PROJ_TILE, l)
    mod_map = (lambda b, i: (b, 0, 0)) if per_batch else (lambda b, i: (0, 0, 0))
    return pl.pallas_call(
        _proj_kernel,
        grid=(b_, l // t),
        in_specs=[pl.BlockSpec((1, t, D_MODEL), lambda b, i: (b, i, 0)),
                  pl.BlockSpec((1, 1, D_MODEL), mod_map),
                  pl.BlockSpec((1, 1, D_MODEL), mod_map),
                  pl.BlockSpec((None, D_MODEL, PROJ_W), lambda b, i: (layer, 0, 0))],
        out_specs=[pl.BlockSpec((1, t, MAIN_W), lambda b, i: (b, i, 0)),
                   pl.BlockSpec((1, t, SMALL_W), lambda b, i: (b, i, 0))],
        out_shape=[jax.ShapeDtypeStruct((b_, l, MAIN_W), BF16),
                   jax.ShapeDtypeStruct((b_, l, SMALL_W), F32)],
        compiler_params=pltpu.CompilerParams(vmem_limit_bytes=VMEM_LIMIT),
        name="ln_mod_proj",
    )(x, sh, sc, w_perm)


def _out_kernel(alpha, x_ref, ya_ref, yb_ref, yc_ref, w_ref, gt_ref, g_ref, b_ref, o_ref):
    y = jnp.concatenate([ya_ref[0], yb_ref[0], yc_ref[0]], axis=-1)
    h = alpha * x_ref[0] + gt_ref[0] * _mm(y, w_ref[...])
    o_ref[0] = _layer_norm(h) * g_ref[...] + b_ref[...]


def _out_project(x, ya, yb, yc, w_out, layer, gt, ln_g, ln_b, per_batch, alpha):
    b_, l, _ = x.shape
    t = min(PROJ_TILE, l)
    mod_map = (lambda b, i: (b, 0, 0)) if per_batch else (lambda b, i: (0, 0, 0))
    tok = lambda w: pl.BlockSpec((1, t, w), lambda b, i: (b, i, 0))
    return pl.pallas_call(
        functools.partial(_out_kernel, alpha),
        grid=(b_, l // t),
        in_specs=[tok(D_MODEL), tok(A_W), tok(B_W), tok(C_VW),
                  pl.BlockSpec((None, D_MODEL, D_MODEL), lambda b, i: (layer, 0, 0)),
                  pl.BlockSpec((1, 1, D_MODEL), mod_map),
                  pl.BlockSpec((1, D_MODEL), lambda b, i: (0, 0)),
                  pl.BlockSpec((1, D_MODEL), lambda b, i: (0, 0))],
        out_specs=tok(D_MODEL),
        out_shape=jax.ShapeDtypeStruct((b_, l, D_MODEL), F32),
        compiler_params=pltpu.CompilerParams(vmem_limit_bytes=VMEM_LIMIT),
        name="out_proj_postnorm",
    )(x, ya, yb, yc, w_out, gt, ln_g.reshape(1, -1), ln_b.reshape(1, -1))


NA_ROWS_PER_STEP = 8


def _stack_heads(q):
    head = _iota2(q.shape, 1) // B_DH
    return jnp.concatenate([jnp.where(head == h, q, jnp.zeros_like(q)) for h in range(B_HEADS)], axis=0)


def _unstack_heads(o, n):
    head = _iota2((n, B_W), 1) // B_DH
    out = jnp.zeros((n, B_W), F32)
    for h in range(B_HEADS):
        out = jnp.where(head == h, o[h * n:(h + 1) * n], out)
    return out


def _na_kernel(rows, q_ref, k_ref, v_ref, kc_ref, vc_ref, z_ref, bias_ref, o_ref):
    kc = kc_ref[0]
    vc = vc_ref[0]
    n_win = WIN_R * GRID_W
    for rr in range(NA_ROWS_PER_STEP):
        r = pl.program_id(1) * NA_ROWS_PER_STEP + rr
        r0 = jnp.clip(r - WIN_R // 2, 0, rows - WIN_R)
        start = pl.multiple_of(r0 * GRID_W, GRID_W)
        kw = k_ref[0, pl.ds(start, n_win), :]
        vw = v_ref[0, pl.ds(start, n_win), :]
        tok = slice(rr * GRID_W, (rr + 1) * GRID_W)
        qs = _stack_heads(q_ref[0, tok, :] * (B_DH ** -0.5))
        s_win = _nt(qs, kw)
        s_ctx = _nt(qs, kc)
        p_win, p_ctx, inv = [], [], []
        for h in range(B_HEADS):
            blk = slice(h * GRID_W, (h + 1) * GRID_W)
            sw = s_win[blk] + bias_ref[r - r0, blk, :]
            sc = s_ctx[blk]
            m = jnp.maximum(jnp.max(sw, -1, keepdims=True), jnp.max(sc, -1, keepdims=True))
            pw = jnp.exp(sw - m)
            pc = jnp.exp(sc - m)
            inv.append(1.0 / (jnp.sum(pw, -1, keepdims=True) + jnp.sum(pc, -1, keepdims=True)))
            p_win.append(pw.astype(BF16))
            p_ctx.append(pc.astype(BF16))
        o = _mm(jnp.concatenate(p_win, axis=0), vw) + _mm(jnp.concatenate(p_ctx, axis=0), vc)
        o = _unstack_heads(o * jnp.concatenate(inv, axis=0), GRID_W)
        o_ref[0, tok, :] = (o * _silu(z_ref[0, tok, :].astype(F32))).astype(BF16)


def _na_bias_table(rpb):
    cq = np.arange(GRID_W)
    c0 = np.clip(cq - WIN_C // 2, 0, GRID_W - WIN_C)
    col_ok = (cq[None, :] >= c0[:, None]) & (cq[None, :] < c0[:, None] + WIN_C)
    dj = np.clip(cq[None, :] - cq[:, None] + (WIN_C - 1), 0, 2 * WIN_C - 2)
    onehot = (dj[None] == np.arange(2 * WIN_C - 1)[:, None, None]).astype(np.float32)
    t = jnp.einsum('lhdk,kqc->lhdqc', rpb.astype(F32), jnp.asarray(onehot), precision=HI)
    t = jnp.where(col_ok[None, None, None], t, NEG_INF)
    slabs = [jnp.transpose(t[:, :, WIN_R - 1 - v:2 * WIN_R - 1 - v], (0, 1, 3, 2, 4)) for v in range(WIN_R)]
    return jnp.stack(slabs, axis=1).reshape(rpb.shape[0], WIN_R, B_HEADS * GRID_W, WIN_R * GRID_W)


def _na_latent(main_l, main_c, bias, layer):
    b_, s, _ = main_l.shape
    lc = main_c.shape[1]
    rows = s // GRID_W
    assert rows >= WIN_R and rows % NA_ROWS_PER_STEP == 0
    tq = NA_ROWS_PER_STEP * GRID_W
    col = lambda idx: (lambda b, i: (b, 0, idx))
    return pl.pallas_call(
        functools.partial(_na_kernel, rows),
        grid=(b_, rows // NA_ROWS_PER_STEP),
        in_specs=[pl.BlockSpec((1, tq, B_W), lambda b, i: (b, i, 8)),
                  pl.BlockSpec((1, s, B_W), col(9)),
                  pl.BlockSpec((1, s, B_W), col(10)),
                  pl.BlockSpec((1, lc, B_W), col(9)),
                  pl.BlockSpec((1, lc, B_W), col(10)),
                  pl.BlockSpec((1, tq, B_W), lambda b, i: (b, i, 11)),
                  pl.BlockSpec((None,) + bias.shape[1:], lambda b, i: (layer, 0, 0, 0))],
        out_specs=pl.BlockSpec((1, tq, B_W), lambda b, i: (b, i, 0)),
        out_shape=jax.ShapeDtypeStruct((b_, s, B_W), BF16),
        compiler_params=pltpu.CompilerParams(vmem_limit_bytes=VMEM_LIMIT),
        name="na_latent",
    )(main_l, main_l, main_l, main_c, main_c, main_l, bias)


def _na_ctx_kernel(q_ref, k_ref, v_ref, z_ref, o_ref):
    lc = q_ref.shape[1]
    qs = _stack_heads(q_ref[0] * (B_DH ** -0.5))
    s = _nt(qs, k_ref[0])
    p = jnp.exp(s - jnp.max(s, -1, keepdims=True))
    o = _mm(p.astype(BF16), v_ref[0]) / jnp.sum(p, -1, keepdims=True)
    o = _unstack_heads(o, lc)
    o_ref[0] = (o * _silu(z_ref[0].astype(F32))).astype(BF16)


def _na_ctx(main_c):
    b_, lc, _ = main_c.shape
    col = lambda idx: pl.BlockSpec((1, lc, B_W), lambda b: (b, 0, idx))
    return pl.pallas_call(
        _na_ctx_kernel,
        grid=(b_,),
        in_specs=[col(8), col(9), col(10), col(11)],
        out_specs=pl.BlockSpec((1, lc, B_W), lambda b: (b, 0, 0)),
        out_shape=jax.ShapeDtypeStruct((b_, lc, B_W), BF16),
        name="na_ctx",
    )(main_c, main_c, main_c, main_c)


GLA_TILE = 128
GLA_INTRA_CHUNKS = 2
GLA_SCAN_STEPS = 8


def _bwd_chunk(n, nc_c, nc_l):
    return jnp.where(n < nc_c, nc_c - 1 - n, nc_c + nc_l - 1 - (n - nc_c))


def _gla_kernel(lc, ll,
                qc_ref, kc_ref, vc_ref, rc_ref, zc_ref, ql_ref, kl_ref, vl_ref, rl_ref, zl_ref,
                cos_ref, sin_ref, w2f_ref, w2b_ref, b2_ref, nw_ref, oc_ref, ol_ref,
                q_s, k_s, v_s, bf_s, bb_s, qdf_s, kdf_s, qdb_s, kdb_s, glf_s, glb_s, of_s, ob_s, st_s):
    c = C_CHUNK
    nc_c, nc_l = lc // c, ll // c
    per_tile = GLA_TILE // c
    scale = C_DK ** -0.5

    ti = _iota2((GLA_TILE, GLA_TILE), 0)
    tj = _iota2((GLA_TILE, GLA_TILE), 1)
    same = (ti // c) == (tj // c)
    tri_l = jnp.where(same & (tj <= ti), 1.0, 0.0).astype(BF16)
    tri_u = jnp.where(same & (tj >= ti), 1.0, 0.0).astype(BF16)
    rot = (jnp.where((tj % 16 < 8) & (ti == tj + 8), -1.0, 0.0)
           + jnp.where((tj % 16 >= 8) & (ti == tj - 8), 1.0, 0.0)).astype(BF16)
    expand = jnp.where(_iota2((C_QW, C_VW), 0) // C_DK == _iota2((C_QW, C_VW), 1) // C_DV,
                       1.0, 0.0).astype(BF16)
    st_mask = jnp.where(_iota2((C_VW, C_QW), 0) // C_DV == _iota2((C_VW, C_QW), 1) // C_DK,
                        1.0, 0.0).astype(F32)
    head_ones = jnp.where(_iota2((C_VW, C_VW), 0) // C_DV == _iota2((C_VW, C_VW), 1) // C_DV,
                          1.0, 0.0).astype(BF16)

    def log_sigmoid(x):
        return jnp.minimum(x, 0.0) - jnp.log(1.0 + jnp.exp(-jnp.abs(x)))

    def prologue(q_ref, k_ref, v_ref, r_ref, base, n_tiles, rope):
        def body(t, carry):
            off = pl.multiple_of(t * GLA_TILE, GLA_TILE)
            uoff = pl.multiple_of(base + t * GLA_TILE, GLA_TILE)
            qb = q_ref[0, pl.ds(off, GLA_TILE), :]
            kb = k_ref[0, pl.ds(off, GLA_TILE), :]
            q = qb.astype(F32)
            k = kb.astype(F32)
            if rope:
                cs = cos_ref[pl.ds(off, GLA_TILE), :]
                sn = sin_ref[pl.ds(off, GLA_TILE), :]
                q = q * cs + _mm(qb, rot) * sn
                k = k * cs + _mm(kb, rot) * sn
            q = q * scale
            r = r_ref[0, pl.ds(off, GLA_TILE), :].astype(BF16)
            gkf = log_sigmoid(_mm(r, w2f_ref[...]) + b2_ref[0:1, :]) / C_GATE_NORM
            gkb = log_sigmoid(_mm(r, w2b_ref[...]) + b2_ref[1:2, :]) / C_GATE_NORM
            bf = _mm_01(tri_l, gkf)
            bb = _mm_01(tri_u, gkb)
            tf = jnp.concatenate([jnp.broadcast_to(bf[i * c + c - 1:i * c + c], (c, C_QW)) for i in range(per_tile)], axis=0)
            tb = jnp.concatenate([jnp.broadcast_to(bb[i * c:i * c + 1], (c, C_QW)) for i in range(per_tile)], axis=0)
            sl = pl.ds(uoff, GLA_TILE)
            q_s[sl, :] = q.astype(BF16)
            k_s[sl, :] = k.astype(BF16)
            v_s[sl, :] = v_ref[0, pl.ds(off, GLA_TILE), :]
            bf_s[sl, :] = bf
            bb_s[sl, :] = bb
            qdf_s[sl, :] = (q * jnp.exp(bf)).astype(BF16)
            kdf_s[sl, :] = (k * jnp.exp(tf - bf)).astype(BF16)
            qdb_s[sl, :] = (q * jnp.exp(bb)).astype(BF16)
            kdb_s[sl, :] = (k * jnp.exp(tb - bb)).astype(BF16)
            goff = pl.multiple_of((base // c + t * per_tile) * 8, 8)
            glf_s[pl.ds(goff, per_tile * 8), :] = jnp.exp(
                jnp.concatenate([tf[i * c:i * c + 8] for i in range(per_tile)], axis=0))
            glb_s[pl.ds(goff, per_tile * 8), :] = jnp.exp(
                jnp.concatenate([tb[i * c:i * c + 8] for i in range(per_tile)], axis=0))
            return carry
        lax.fori_loop(0, n_tiles, body, 0, unroll=min(4, n_tiles))

    prologue(qc_ref, kc_ref, vc_ref, rc_ref, 0, lc // GLA_TILE, False)
    prologue(ql_ref, kl_ref, vl_ref, rl_ref, lc, ll // GLA_TILE, True)

    jj = _iota2((c, C_QW), 0)
    row_sum = jnp.where(_iota2((c, c * c), 1) // c == _iota2((c, c * c), 0), 1.0, 0.0).astype(BF16)

    def intra(first_chunk):
        offs = [pl.multiple_of((first_chunk + j) * c, c) for j in range(GLA_INTRA_CHUNKS)]
        ps = []
        for off in offs:
            q = q_s[pl.ds(off, c), :].astype(F32)
            k = k_s[pl.ds(off, c), :].astype(F32)
            bf = bf_s[pl.ds(off, c), :]
            bb = bb_s[pl.ds(off, c), :]
            rows = []
            for i in range(c):
                e = jnp.exp(jnp.where(jj < i, bf[i:i + 1] - bf, bb[i:i + 1] - bb))
                rows.append((jnp.where(jj == i, 2.0, e) * (q[i:i + 1] * k)).astype(BF16))
            ps.append(jnp.concatenate(rows, axis=0))
        return [(off, _mm(p, expand)) for off, p in zip(offs, ps)]

    def intra_apply(weights):
        outs = []
        for off, a in weights:
            v = v_s[pl.ds(off, c), :].astype(F32)
            av = (a.reshape(c, c, C_VW) * v[None]).reshape(c * c, C_VW).astype(BF16)
            outs.append(_mm(row_sum, av))
        return outs

    st_s[...] = jnp.zeros_like(st_s)
    dirs = ((qdf_s, kdf_s, glf_s), (qdb_s, kdb_s, glb_s))

    def scan(it, carry):
        steps = [it * GLA_SCAN_STEPS + s for s in range(GLA_SCAN_STEPS)]
        chunk = [steps, [_bwd_chunk(n, nc_c, nc_l) for n in steps]]
        sls = [[pl.ds(pl.multiple_of(ci * c, c), c) for ci in chunk[d]] for d in range(2)]
        upd = [[_tn(v_s[sl, :], dirs[d][1][sl, :]) for sl in sls[d]] for d in range(2)]
        weights = []
        for s in range(0, GLA_SCAN_STEPS, GLA_INTRA_CHUNKS):
            weights += intra(steps[s])
        st = [st_s[0], st_s[1]]
        inter = [[], []]
        for s in range(GLA_SCAN_STEPS):
            for d in range(2):
                inter[d].append(_nt(dirs[d][0][sls[d][s], :], st[d].astype(BF16)))
                gl = dirs[d][2][pl.ds(pl.multiple_of(chunk[d][s] * 8, 8), 8), :][0:1]
                st[d] = st[d] * gl + upd[d][s] * st_mask
        st_s[0] = st[0]
        st_s[1] = st[1]
        local = intra_apply(weights)
        for s in range(GLA_SCAN_STEPS):
            of_s[sls[0][s], :] = local[s] + inter[0][s]
            ob_s[sls[1][s], :] = inter[1][s]
        return carry
    lax.fori_loop(0, (nc_c + nc_l) // GLA_SCAN_STEPS, scan, 0)

    def epilogue(z_ref, o_ref, base, n_tiles):
        def body(t, carry):
            off = pl.multiple_of(t * GLA_TILE, GLA_TILE)
            sl = pl.ds(pl.multiple_of(base + t * GLA_TILE, GLA_TILE), GLA_TILE)
            o = of_s[sl, :] + ob_s[sl, :]
            sq = o * o
            hi = sq.astype(BF16)
            lo = (sq - hi.astype(F32)).astype(BF16)
            ms = (_mm(hi, head_ones) + _mm(lo, head_ones)) * (1.0 / C_DV)
            o = o * lax.rsqrt(ms + 1e-6) * nw_ref[...]
            z = z_ref[0, pl.ds(off, GLA_TILE), :].astype(F32)
            o_ref[0, pl.ds(off, GLA_TILE), :] = (o * _silu(z)).astype(BF16)
            return carry
        lax.fori_loop(0, n_tiles, body, 0, unroll=2)

    epilogue(zc_ref, oc_ref, 0, lc // GLA_TILE)
    epilogue(zl_ref, ol_ref, lc, ll // GLA_TILE)


def _rope_tables(s):
    t = np.arange(s)
    nf = C_DK // 4
    inv = ROPE_THETA ** (-np.arange(nf, dtype=np.float64) / nf)
    lane = np.arange(C_DK)
    pos = np.where((lane // (C_DK // 2))[None, :] == 0, (t // GRID_W)[:, None], (t % GRID_W)[:, None])
    ang = pos.astype(np.float32) * inv.astype(np.float32)[lane % nf][None, :]
    cos = np.tile(np.cos(ang).astype(np.float32), (1, C_HEADS))
    sin = np.tile(np.sin(ang).astype(np.float32), (1, C_HEADS))
    return jnp.asarray(cos), jnp.asarray(sin)


def _gla(main_c, small_c, main_l, small_l, w2, b2, norm_w):
    b_, lc, _ = main_c.shape
    ll = main_l.shape[1]
    lt = lc + ll
    assert lc % GLA_TILE == 0 and ll % GLA_TILE == 0
    cos, sin = _rope_tables(ll)
    w2f = jnp.zeros((SMALL_W, C_QW), BF16).at[16:16 + C_RANK].set(w2[0].astype(BF16))
    w2b = jnp.zeros((SMALL_W, C_QW), BF16).at[16 + C_RANK:16 + 2 * C_RANK].set(w2[1].astype(BF16))
    nw = jnp.tile(norm_w.astype(F32), C_HEADS).reshape(1, C_VW)
    nchunks = lt // C_CHUNK

    def seg(l, w, idx):
        return pl.BlockSpec((1, l, w), lambda b: (b, 0, idx))

    def full(shape):
        return pl.BlockSpec(shape, lambda b: (0,) * len(shape))

    return pl.pallas_call(
        functools.partial(_gla_kernel, lc, ll),
        grid=(b_,),
        in_specs=[seg(lc, C_QW, 24), seg(lc, C_QW, 25), seg(lc, C_VW, 13), seg(lc, SMALL_W, 0), seg(lc, C_VW, 14),
                  seg(ll, C_QW, 24), seg(ll, C_QW, 25), seg(ll, C_VW, 13), seg(ll, SMALL_W, 0), seg(ll, C_VW, 14),
                  full((ll, C_QW)), full((ll, C_QW)), full((SMALL_W, C_QW)), full((SMALL_W, C_QW)),
                  full((2, C_QW)), full((1, C_VW))],
        out_specs=[seg(lc, C_VW, 0), seg(ll, C_VW, 0)],
        out_shape=[jax.ShapeDtypeStruct((b_, lc, C_VW), BF16), jax.ShapeDtypeStruct((b_, ll, C_VW), BF16)],
        scratch_shapes=[pltpu.VMEM((lt, C_QW), BF16), pltpu.VMEM((lt, C_QW), BF16), pltpu.VMEM((lt, C_VW), BF16),
                        pltpu.VMEM((lt, C_QW), F32), pltpu.VMEM((lt, C_QW), F32),
                        pltpu.VMEM((lt, C_QW), BF16), pltpu.VMEM((lt, C_QW), BF16),
                        pltpu.VMEM((lt, C_QW), BF16), pltpu.VMEM((lt, C_QW), BF16),
                        pltpu.VMEM((nchunks * 8, C_QW), F32), pltpu.VMEM((nchunks * 8, C_QW), F32),
                        pltpu.VMEM((lt, C_VW), F32), pltpu.VMEM((lt, C_VW), F32),
                        pltpu.VMEM((2, C_VW, C_QW), F32)],
        compiler_params=pltpu.CompilerParams(vmem_limit_bytes=VMEM_LIMIT),
        name="gla",
    )(main_c, main_c, main_c, small_c, main_c, main_l, main_l, main_l, small_l, main_l,
      cos, sin, w2f, w2b, b2.astype(F32), nw)


GDN_CHUNK = 128
GDN_BASE = 16
GDN_LEVELS = 3
M_INCL, M_STRICT, M_BASE, M_LEVEL, M_EYE = 0, 2, 4, 5, 8
N_MASKS = 9
GDN_PRE_CHUNKS = 4
GDN_REC_FIRST_STAGE = 2
CONV_HALO = 16
CONV_TILE = 256


def _gdn_kernel(lc, ll, n_seq,
                xqc_ref, xkc_ref, xvc_ref, sc_ref, zc_ref, xql_ref, xkl_ref, xvl_ref, sl_ref, zl_ref,
                cwq_ref, cwk_ref, cwv_ref, alog_ref, dtb_ref, nw_ref, oc_ref, ol_ref,
                pad_s, q_s, k_s, v_s, gf_s, gb_s, bef_s, beb_s, u_s, w_s, qd_s, kdt_s, qk_s, gl_s, msk_s, o_s, st_s):
    c = GDN_CHUNK
    nc_c, nc_l = lc // c, ll // c
    step = pl.program_id(0)
    h = jnp.minimum(step, n_seq - 1) % A_HEADS
    cur = step % 2
    prev = 1 - cur

    @pl.when(step == 0)
    def _():
        for buf in (u_s, w_s, qd_s, kdt_s, qk_s, gl_s):
            buf[1] = jnp.zeros(buf.shape[1:], buf.dtype)
        ti = _iota2((c, c), 0)
        tj = _iota2((c, c), 1)
        as_f32 = lambda m: jnp.where(m, 1.0, 0.0).astype(F32)
        msk_s[M_INCL] = as_f32(tj <= ti)
        msk_s[M_INCL + 1] = as_f32(tj >= ti)
        msk_s[M_STRICT] = as_f32(tj < ti)
        msk_s[M_STRICT + 1] = as_f32(tj > ti)
        msk_s[M_BASE] = -as_f32(ti // GDN_BASE == tj // GDN_BASE)
        for lv in range(GDN_LEVELS):
            s = GDN_BASE << lv
            msk_s[M_LEVEL + lv] = as_f32((ti // (2 * s) == tj // (2 * s)) & (ti // s != tj // s))
        msk_s[M_EYE] = as_f32(ti == tj)

    def conv_group(parts, seg_base, l, g0, n_tok):
        t = CONV_TILE
        for p, (x_ref, w_ref, dst, mode) in enumerate(parts):
            lo = _aligned(jnp.maximum(g0 - CONV_HALO, 0), CONV_HALO)
            hi = _aligned(jnp.minimum(g0 + n_tok, l - CONV_HALO), CONV_HALO)
            left = x_ref[0, pl.ds(lo, CONV_HALO), :].astype(F32)
            right = x_ref[0, pl.ds(hi, CONV_HALO), :].astype(F32)
            pad_s[p, 0:CONV_HALO, :] = jnp.where(g0 > 0, left, 0.0)
            pad_s[p, CONV_HALO:CONV_HALO + n_tok, :] = x_ref[0, pl.ds(_aligned(g0, n_tok), n_tok), :].astype(F32)
            pad_s[p, CONV_HALO + n_tok:2 * CONV_HALO + n_tok, :] = jnp.where(g0 + n_tok < l, right, 0.0)
            w = w_ref[...]
            for t0 in range(0, n_tok, t):
                acc = jnp.zeros((t, LANES), F32)
                for j in range(CONV_K):
                    r0 = CONV_HALO - CONV_K // 2 + j + t0
                    acc = acc + pad_s[p, r0:r0 + t, :] * w[j:j + 1]
                y = _silu(acc)
                if mode != "v":
                    y = y * lax.rsqrt(jnp.sum(y * y, -1, keepdims=True) + 1e-6)
                if mode == "q":
                    y = y * (A_DK ** -0.5)
                dst[pl.ds(_aligned(seg_base + g0 + t0, t), t), :] = y.astype(dst.dtype)

    parts_c = ((xqc_ref, cwq_ref, q_s, "q"), (xkc_ref, cwk_ref, k_s, "k"), (xvc_ref, cwv_ref, v_s, "v"))
    parts_l = ((xql_ref, cwq_ref, q_s, "q"), (xkl_ref, cwk_ref, k_s, "k"), (xvl_ref, cwv_ref, v_s, "v"))

    n_gate = 4 * A_HEADS
    neg_a = -jnp.exp(alog_ref[...])
    chan = _iota2((n_gate, c), 0)

    def col_bcast(rows, idx):
        row = jnp.sum(jnp.where(chan == idx, rows, 0.0), axis=0, keepdims=True)
        return jnp.broadcast_to(row, (c, c)).T

    def gates_group(s_ref, seg_base, g0, n_tok):
        tri_lb = msk_s[M_INCL].astype(BF16)
        tri_ub = msk_s[M_INCL + 1].astype(BF16)
        for t0 in range(0, n_tok, c):
            xt = s_ref[0, pl.ds(_aligned(g0 + t0, c), c), :].T[:n_gate]
            g = jnp.where(chan < 2 * A_HEADS, neg_a * _softplus(xt + dtb_ref[...]), _sigmoid(xt))
            cum_f = _mm_01(tri_ub, g, right=True)
            cum_b = _mm_01(tri_lb, g, right=True)
            sl = pl.ds(_aligned(seg_base + g0 + t0, c), c)
            gf_s[sl, :] = col_bcast(cum_f, h)
            gb_s[sl, :] = col_bcast(cum_b, h + A_HEADS)
            bef_s[sl, :] = col_bcast(g, h + 2 * A_HEADS).astype(BF16)
            beb_s[sl, :] = col_bcast(g, h + 3 * A_HEADS).astype(BF16)

    def front_end(seg, g0, n_tok):
        parts, s_ref, base, l = ((parts_c, sc_ref, 0, lc), (parts_l, sl_ref, lc, ll))[seg]
        conv_group(parts, base, l, g0, n_tok)
        gates_group(s_ref, base, g0, n_tok)

    def prepass(first_chunk, per_step, it):
        chains = []
        for j in range(per_step):
            ci = first_chunk + it * per_step + j
            sl = pl.ds(_aligned(ci * c, c), c)
            kbf = k_s[sl, :]
            q = q_s[sl, :].astype(F32)
            k = kbf.astype(F32)
            gram = _nt(kbf, kbf)
            qk_raw = _nt(q_s[sl, :], kbf)
            for d, (g_s, be_s) in enumerate(((gf_s, bef_s), (gb_s, beb_s))):
                chains.append(dict(ci=ci, sl=sl, d=d, q=q, k=k, v=v_s[sl, :].astype(F32), gram=gram, qk_raw=qk_raw,
                                   gc=g_s[sl, :], beta=be_s[sl, :].astype(F32)))
        yield
        for ch in chains:
            gc = ch["gc"]
            ch["decay"] = jnp.exp(jnp.minimum(gc - gc.T, 0.0)) * msk_s[M_INCL + ch["d"]]
            ch["a"] = ch["gram"] * ch["beta"] * ch["decay"] * msk_s[M_STRICT + ch["d"]]
            n0 = ch["a"] * msk_s[M_BASE]
            ch["inv"] = msk_s[M_EYE] + n0
            ch["pw"] = n0.astype(BF16)
        yield
        n_sq = GDN_BASE.bit_length() - 2
        for ch in chains:
            ch["pw"] = _mm(ch["pw"], ch["pw"]).astype(BF16)
        yield
        for m in range(n_sq):
            for ch in chains:
                ch["inv_next"] = ch["inv"] + _mm(ch["inv"].astype(BF16), ch["pw"])
            if m + 1 < n_sq:
                for ch in chains:
                    ch["pw"] = _mm(ch["pw"], ch["pw"]).astype(BF16)
            for ch in chains:
                ch["inv"] = ch["inv_next"]
            yield
        for lv in range(GDN_LEVELS):
            for ch in chains:
                ch["invb"] = ch["inv"].astype(BF16)
                ch["t"] = _mm((ch["a"] * msk_s[M_LEVEL + lv]).astype(BF16), ch["invb"]).astype(BF16)
            yield
            for ch in chains:
                ch["inv"] = ch["inv"] - _mm(ch["invb"], ch["t"])
            yield
        for ch in chains:
            gc, beta, q, k, d, sl = ch["gc"], ch["beta"], ch["q"], ch["k"], ch["d"], ch["sl"]
            eg = jnp.exp(gc)
            rhs = jnp.concatenate([ch["v"] * beta, k * beta * eg], axis=-1).astype(BF16)
            sol = _mm(ch["inv"].astype(BF16), rhs)
            g_last = gc[c - 1:c] if d == 0 else gc[0:1]
            u_s[cur, d, sl, :] = sol[:, :A_DV].astype(BF16)
            w_s[cur, d, sl, :] = sol[:, A_DV:].astype(BF16)
            qk_s[cur, d, sl, :] = (ch["qk_raw"] * ch["decay"]).astype(BF16)
            qd_s[cur, d, sl, :] = (q * eg).astype(BF16)
            kdt_s[cur, d, sl, :] = (k * jnp.exp(g_last - gc)).T.astype(BF16)
            gl_s[cur, d, pl.ds(_aligned(ch["ci"] * 8, 8), 8), :] = jnp.exp(
                jnp.broadcast_to(g_last, (8, LANES)))

    def recurrence(first_step, n_steps, it):
        st = [st_s[0], st_s[1]]
        for j in range(n_steps):
            n = first_step + it * n_steps + j
            cis = (n, _bwd_chunk(n, nc_c, nc_l))
            sls = [pl.ds(_aligned(ci * c, c), c) for ci in cis]
            stb = [x.astype(BF16) for x in st]
            w_st = [_mm(w_s[prev, d, sls[d], :], stb[d]) for d in range(2)]
            o_st = [_mm(qd_s[prev, d, sls[d], :], stb[d]) for d in range(2)]
            yield
            vnb = [(u_s[prev, d, sls[d], :].astype(F32) - w_st[d]).astype(BF16) for d in range(2)]
            for d in range(2):
                gl = gl_s[prev, d, pl.ds(_aligned(cis[d] * 8, 8), 8), :][0:1]
                st[d] = st[d] * gl + _mm(kdt_s[prev, d, sls[d], :], vnb[d])
            for d in range(2):
                o_s[sls[d], :] += o_st[d] + _mm(qk_s[prev, d, sls[d], :], vnb[d])
            yield
        st_s[0] = st[0]
        st_s[1] = st[1]

    o_s[...] = jnp.zeros_like(o_s)
    st_s[...] = jnp.zeros_like(st_s)

    def merged(first, per_step, next_group, it, carry):
        pre = prepass(first, per_step, it)
        rec = recurrence(first, per_step, it)
        for i, _ in enumerate(pre):
            if i == 0 and next_group is not None:
                front_end(*next_group(it))
            if i >= GDN_REC_FIRST_STAGE:
                next(rec, None)
        for _ in rec:
            pass
        return carry

    per_c = math.gcd(nc_c, GDN_PRE_CHUNKS)
    per_l = math.gcd(nc_l, GDN_PRE_CHUNKS)
    it_c, it_l = nc_c // per_c, nc_l // per_l
    front_end(0, 0, per_c * c)
    if it_c > 1:
        lax.fori_loop(0, it_c - 1, functools.partial(
            merged, 0, per_c, lambda it: (0, (it + 1) * (per_c * c), per_c * c)), 0)
    merged(0, per_c, lambda it: (1, 0, per_l * c), it_c - 1, 0)
    if it_l > 1:
        lax.fori_loop(0, it_l - 1, functools.partial(
            merged, nc_c, per_l, lambda it: (1, (it + 1) * (per_l * c), per_l * c)), 0)
    merged(nc_c, per_l, None, it_l - 1, 0)

    def epilogue(z_ref, o_ref, base, n_tiles):
        def body(t, carry):
            off = pl.multiple_of(t * GDN_CHUNK, GDN_CHUNK)
            o = o_s[pl.ds(pl.multiple_of(base + t * GDN_CHUNK, GDN_CHUNK), GDN_CHUNK), :]
            o = o * lax.rsqrt(jnp.mean(o * o, -1, keepdims=True) + 1e-6) * nw_ref[...]
            z = z_ref[0, pl.ds(off, GDN_CHUNK), :].astype(F32)
            o_ref[0, pl.ds(off, GDN_CHUNK), :] = (o * _silu(z)).astype(BF16)
            return carry
        lax.fori_loop(0, n_tiles, body, 0, unroll=2)

    epilogue(zc_ref, oc_ref, 0, lc // GDN_CHUNK)
    epilogue(zl_ref, ol_ref, lc, ll // GDN_CHUNK)


def _gdn(main_c, small_c, main_l, small_l, conv_w, a_log, dt_bias, norm_w):
    b_, lc, _ = main_c.shape
    ll = main_l.shape[1]
    lt = lc + ll
    assert lc % 256 == 0 and ll % 256 == 0
    n_gate = 4 * A_HEADS
    pad8 = lambda p: jnp.broadcast_to(
        jnp.zeros((n_gate,), F32).at[:2 * A_HEADS].set(p.astype(F32).reshape(-1))[:, None], (n_gate, GDN_CHUNK))
    cw = conv_w.astype(F32)

    n_seq = b_ * A_HEADS
    pre_seq = lambda s: jnp.minimum(s, n_seq - 1)
    rec_seq = lambda s: jnp.maximum(s - 1, 0)

    def seg(l, w, part, seq):
        return pl.BlockSpec((1, l, w), lambda s: (seq(s) // A_HEADS, 0, part * A_HEADS + seq(s) % A_HEADS))

    def cw_spec(part):
        return pl.BlockSpec((CONV_K, A_DK), lambda s: (0, part * A_HEADS + pre_seq(s) % A_HEADS))

    const = lambda shape: pl.BlockSpec(shape, lambda s: (0,) * len(shape))
    in_specs = []
    for l in (lc, ll):
        in_specs += [seg(l, A_DK, 0, pre_seq), seg(l, A_DK, 1, pre_seq), seg(l, A_DK, 2, pre_seq),
                     pl.BlockSpec((1, l, SMALL_W), lambda s: (pre_seq(s) // A_HEADS, 0, 0)),
                     seg(l, A_DV, 3, rec_seq)]
    in_specs += [cw_spec(0), cw_spec(1), cw_spec(2), const((n_gate, GDN_CHUNK)), const((n_gate, GDN_CHUNK)),
                 const((1, A_DV))]
    tok = lambda dt: pltpu.VMEM((lt, LANES), dt)
    halves = lambda: pltpu.VMEM((2, 2, lt, LANES), BF16)
    return pl.pallas_call(
        functools.partial(_gdn_kernel, lc, ll, n_seq),
        grid=(n_seq + 1,),
        in_specs=in_specs,
        out_specs=[seg(lc, A_DV, 0, rec_seq), seg(ll, A_DV, 0, rec_seq)],
        out_shape=[jax.ShapeDtypeStruct((b_, lc, A_W), BF16), jax.ShapeDtypeStruct((b_, ll, A_W), BF16)],
        scratch_shapes=[pltpu.VMEM((3, min(GDN_PRE_CHUNKS * GDN_CHUNK, max(lc, ll)) + 2 * CONV_HALO, LANES), F32),
                        tok(BF16), tok(BF16), tok(BF16), tok(F32), tok(F32), tok(BF16), tok(BF16),
                        halves(), halves(), halves(), halves(), halves(),
                        pltpu.VMEM((2, 2, (lt // GDN_CHUNK) * 8, LANES), F32),
                        pltpu.VMEM((N_MASKS, GDN_CHUNK, GDN_CHUNK), F32),
                        tok(F32), pltpu.VMEM((2, A_DK, A_DV), F32)],
        compiler_params=pltpu.CompilerParams(vmem_limit_bytes=GDN_VMEM_LIMIT,
                                             dimension_semantics=("arbitrary",)),
        name="gdn",
    )(main_c, main_c, main_c, small_c, main_c, main_l, main_l, main_l, small_l, main_l,
      cw, cw, cw, pad8(a_log), pad8(dt_bias), norm_w.astype(F32).reshape(1, A_DV))


def _permute_w_in(w):
    sizes = (3 * A_W, 4 * A_HEADS, A_W, B_W, B_W, B_W, B_W, C_QW, C_QW, C_VW, 2 * C_RANK, C_VW)
    offs = np.concatenate([[0], np.cumsum(sizes)])
    part = lambda i: w[..., offs[i]:offs[i + 1]]
    pad = jnp.zeros(w.shape[:-1] + (SMALL_W - 4 * A_HEADS - 2 * C_RANK,), w.dtype)
    order = [part(0), part(2), part(3), part(4), part(5), part(6), part(7), part(8), part(9), part(11),
             part(1), part(10), pad]
    return jnp.concatenate(order, axis=-1).astype(BF16)


def kernel(x, c, ctx, c_ctx, w_mod, b_mod, w_in, conv_w, a_log, dt_bias, gdn_norm, rpb,
           gla_w2, gla_b2, gla_norm, w_out, ln_g, ln_b):
    depth = w_mod.shape[0]
    b_ = x.shape[0]
    alpha = (2 * depth) ** 0.25
    n_mod = -(-(b_ + 1) // 8) * 8
    c_all = jnp.zeros((n_mod, D_MODEL), F32).at[:b_].set(c).at[b_].set(c_ctx)
    w_perm = _permute_w_in(w_in)
    w_o = w_out.astype(BF16)
    na_bias = _na_bias_table(rpb)
    b_mod3 = b_mod.reshape(depth, 1, -1)
    xl, xc = x, ctx
    for i in range(depth):
        ctx_out = i < depth - 1
        mod = _modulation(c_all, w_mod, b_mod3, i).reshape(n_mod, 3, 1, D_MODEL)
        sh_l, sc_l, gt_l = mod[:b_, 0], mod[:b_, 1], mod[:b_, 2]
        sh_c, sc_c, gt_c = mod[b_:b_ + 1, 0], mod[b_:b_ + 1, 1], mod[b_:b_ + 1, 2]
        main_l, small_l = _project(xl, sh_l, sc_l, w_perm, i, True)
        flat = lambda t: t.reshape(1, -1, t.shape[-1])
        main_c, small_c = (t.reshape(b_, -1, t.shape[-1]) for t in _project(flat(xc), sh_c, sc_c, w_perm, i, False))
        ya_c, ya_l = _gdn(main_c, small_c, main_l, small_l, conv_w[i], a_log[i], dt_bias[i], gdn_norm[i])
        yb_l = _na_latent(main_l, main_c, na_bias, i)
        yc_c, yc_l = _gla(main_c, small_c, main_l, small_l, gla_w2[i], gla_b2[i], gla_norm[i])
        if ctx_out:
            yb_c = _na_ctx(main_c)
            xc = _out_project(flat(xc), flat(ya_c), flat(yb_c), flat(yc_c), w_o, i, gt_c, ln_g[i], ln_b[i],
                              False, alpha).reshape(xc.shape)
        xl = _out_project(xl, ya_l, yb_l, yc_l, w_o, i, gt_l, ln_g[i], ln_b[i], True, alpha)
    return xl
```

```python
import functools
import math

import numpy as np
import jax
import jax.numpy as jnp
from jax import lax
from jax.experimental import pallas as pl
from jax.experimental.pallas import tpu as pltpu

F32 = jnp.float32
BF16 = jnp.bfloat16
HI = lax.Precision.HIGHEST

D_MODEL = 1024
GRID_W = 64
A_HEADS, A_DK, A_DV, CONV_K, A_CHUNK = 4, 128, 128, 5, 64
B_HEADS, B_DH, WIN_R, WIN_C = 4, 64, 8, 16
C_HEADS, C_DK, C_DV, C_RANK, C_CHUNK = 4, 32, 64, 16, 16
C_GATE_NORM = 16.0
ROPE_THETA = 10000.0
LN_EPS = 1e-6
NEG_INF = -1e30

A_W = A_HEADS * A_DK
B_W = B_HEADS * B_DH
C_QW = C_HEADS * C_DK
C_VW = C_HEADS * C_DV
MAIN_W = 3 * A_W + A_W + 4 * B_W + 2 * C_QW + 2 * C_VW
SMALL_W = 128
PROJ_W = MAIN_W + SMALL_W
LANES = 128
VMEM_LIMIT = 56 * 1024 * 1024
GDN_VMEM_LIMIT = 60 * 1024 * 1024
PROJ_TILE = 1024


def _nt(a, b, precision=None):
    return lax.dot_general(a, b, (((1,), (1,)), ((), ())), precision=precision,
                           preferred_element_type=F32)


def _tn(a, b):
    return lax.dot_general(a, b, (((0,), (0,)), ((), ())), preferred_element_type=F32)


def _mm(a, b, precision=None):
    return jnp.dot(a, b, precision=precision, preferred_element_type=F32)


def _mm_01(m01, x, right=False):
    hi = x.astype(BF16)
    lo = (x - hi.astype(F32)).astype(BF16)
    return _mm(hi, m01) + _mm(lo, m01) if right else _mm(m01, hi) + _mm(m01, lo)


def _sigmoid(x):
    return 1.0 / (1.0 + jnp.exp(-x))


def _silu(x):
    return x * _sigmoid(x)


def _softplus(x):
    return jnp.maximum(x, 0.0) + jnp.log(1.0 + jnp.exp(-jnp.abs(x)))


def _iota2(shape, axis):
    return lax.broadcasted_iota(jnp.int32, shape, axis)


def _aligned(x, m):
    return x if isinstance(x, int) else pl.multiple_of(x, m)


def _mod_kernel(c_ref, w_ref, b_ref, o_ref):
    o_ref[...] = _mm(_silu(c_ref[...]), w_ref[...], HI) + b_ref[...]


def _modulation(c_all, w_mod, b_mod, layer):
    n = c_all.shape[0]
    tn = 768
    return pl.pallas_call(
        _mod_kernel,
        grid=(3 * D_MODEL // tn,),
        in_specs=[pl.BlockSpec((n, D_MODEL), lambda j: (0, 0)),
                  pl.BlockSpec((None, D_MODEL, tn), lambda j: (layer, 0, j)),
                  pl.BlockSpec((None, 1, tn), lambda j: (layer, 0, j))],
        out_specs=pl.BlockSpec((n, tn), lambda j: (0, j)),
        out_shape=jax.ShapeDtypeStruct((n, 3 * D_MODEL), F32),
        name="adaln_mod",
    )(c_all, w_mod, b_mod)


def _layer_norm(x):
    mu = jnp.mean(x, -1, keepdims=True)
    xc = x - mu
    var = jnp.mean(xc * xc, -1, keepdims=True)
    return xc * lax.rsqrt(var + LN_EPS)


def _proj_kernel(x_ref, sh_ref, sc_ref, w_ref, o_ref, s_ref):
    m = (_layer_norm(x_ref[0]) * (1.0 + sc_ref[0]) + sh_ref[0]).astype(BF16)
    step = 640
    for c0 in range(0, MAIN_W, step):
        o_ref[0, :, c0:c0 + step] = _mm(m, w_ref[:, c0:c0 + step]).astype(BF16)
    s_ref[0] = _mm(m, w_ref[:, MAIN_W:])


def _project(x, sh, sc, w_perm, layer, per_batch):
    b_, l, _ = x.shape
    t = min(PROJ_TILE, l)
    mod_map = (lambda b, i: (b, 0, 0)) if per_batch else (lambda b, i: (0, 0, 0))
    return pl.pallas_call(
        _proj_kernel,
        grid=(b_, l // t),
        in_specs=[pl.BlockSpec((1, t, D_MODEL), lambda b, i: (b, i, 0)),
                  pl.BlockSpec((1, 1, D_MODEL), mod_map),
                  pl.BlockSpec((1, 1, D_MODEL), mod_map),
                  pl.BlockSpec((None, D_MODEL, PROJ_W), lambda b, i: (layer, 0, 0))],
        out_specs=[pl.BlockSpec((1, t, MAIN_W), lambda b, i: (b, i, 0)),
                   pl.BlockSpec((1, t, SMALL_W), lambda b, i: (b, i, 0))],
        out_shape=[jax.ShapeDtypeStruct((b_, l, MAIN_W), BF16),
                   jax.ShapeDtypeStruct((b_, l, SMALL_W), F32)],
        compiler_params=pltpu.CompilerParams(vmem_limit_bytes=VMEM_LIMIT),
        name="ln_mod_proj",
    )(x, sh, sc, w_perm)


def _out_kernel(alpha, x_ref, ya_ref, yb_ref, yc_ref, w_ref, gt_ref, g_ref, b_ref, o_ref):
    y = jnp.concatenate([ya_ref[0], yb_ref[0], yc_ref[0]], axis=-1)
    h = alpha * x_ref[0] + gt_ref[0] * _mm(y, w_ref[...])
    o_ref[0] = _layer_norm(h) * g_ref[...] + b_ref[...]


def _out_project(x, ya, yb, yc, w_out, layer, gt, ln_g, ln_b, per_batch, alpha):
    b_, l, _ = x.shape
    t = min(PROJ_TILE, l)
    mod_map = (lambda b, i: (b, 0, 0)) if per_batch else (lambda b, i: (0, 0, 0))
    tok = lambda w: pl.BlockSpec((1, t, w), lambda b, i: (b, i, 0))
    return pl.pallas_call(
        functools.partial(_out_kernel, alpha),
        grid=(b_, l // t),
        in_specs=[tok(D_MODEL), tok(A_W), tok(B_W), tok(C_VW),
                  pl.BlockSpec((None, D_MODEL, D_MODEL), lambda b, i: (layer, 0, 0)),
                  pl.BlockSpec((1, 1, D_MODEL), mod_map),
                  pl.BlockSpec((1, D_MODEL), lambda b, i: (0, 0)),
                  pl.BlockSpec((1, D_MODEL), lambda b, i: (0, 0))],
        out_specs=tok(D_MODEL),
        out_shape=jax.ShapeDtypeStruct((b_, l, D_MODEL), F32),
        compiler_params=pltpu.CompilerParams(vmem_limit_bytes=VMEM_LIMIT),
        name="out_proj_postnorm",
    )(x, ya, yb, yc, w_out, gt, ln_g.reshape(1, -1), ln_b.reshape(1, -1))


NA_ROWS_PER_STEP = 8


def _stack_heads(q):
    head = _iota2(q.shape, 1) // B_DH
    return jnp.concatenate([jnp.where(head == h, q, jnp.zeros_like(q)) for h in range(B_HEADS)], axis=0)


def _unstack_heads(o, n):
    head = _iota2((n, B_W), 1) // B_DH
    out = jnp.zeros((n, B_W), F32)
    for h in range(B_HEADS):
        out = jnp.where(head == h, o[h * n:(h + 1) * n], out)
    return out


def _na_kernel(rows, q_ref, k_ref, v_ref, kc_ref, vc_ref, z_ref, bias_ref, o_ref):
    kc = kc_ref[0]
    vc = vc_ref[0]
    n_win = WIN_R * GRID_W
    for rr in range(NA_ROWS_PER_STEP):
        r = pl.program_id(1) * NA_ROWS_PER_STEP + rr
        r0 = jnp.clip(r - WIN_R // 2, 0, rows - WIN_R)
        start = pl.multiple_of(r0 * GRID_W, GRID_W)
        kw = k_ref[0, pl.ds(start, n_win), :]
        vw = v_ref[0, pl.ds(start, n_win), :]
        tok = slice(rr * GRID_W, (rr + 1) * GRID_W)
        qs = _stack_heads(q_ref[0, tok, :] * (B_DH ** -0.5))
        s_win = _nt(qs, kw)
        s_ctx = _nt(qs, kc)
        p_win, p_ctx, inv = [], [], []
        for h in range(B_HEADS):
            blk = slice(h * GRID_W, (h + 1) * GRID_W)
            sw = s_win[blk] + bias_ref[r - r0, blk, :]
            sc = s_ctx[blk]
            m = jnp.maximum(jnp.max(sw, -1, keepdims=True), jnp.max(sc, -1, keepdims=True))
            pw = jnp.exp(sw - m)
            pc = jnp.exp(sc - m)
            inv.append(1.0 / (jnp.sum(pw, -1, keepdims=True) + jnp.sum(pc, -1, keepdims=True)))
            p_win.append(pw.astype(BF16))
            p_ctx.append(pc.astype(BF16))
        o = _mm(jnp.concatenate(p_win, axis=0), vw) + _mm(jnp.concatenate(p_ctx, axis=0), vc)
        o = _unstack_heads(o * jnp.concatenate(inv, axis=0), GRID_W)
        o_ref[0, tok, :] = (o * _silu(z_ref[0, tok, :].astype(F32))).astype(BF16)


def _na_bias_table(rpb):
    cq = np.arange(GRID_W)
    c0 = np.clip(cq - WIN_C // 2, 0, GRID_W - WIN_C)
    col_ok = (cq[None, :] >= c0[:, None]) & (cq[None, :] < c0[:, None] + WIN_C)
    dj = np.clip(cq[None, :] - cq[:, None] + (WIN_C - 1), 0, 2 * WIN_C - 2)
    onehot = (dj[None] == np.arange(2 * WIN_C - 1)[:, None, None]).astype(np.float32)
    t = jnp.einsum('lhdk,kqc->lhdqc', rpb.astype(F32), jnp.asarray(onehot), precision=HI)
    t = jnp.where(col_ok[None, None, None], t, NEG_INF)
    slabs = [jnp.transpose(t[:, :, WIN_R - 1 - v:2 * WIN_R - 1 - v], (0, 1, 3, 2, 4)) for v in range(WIN_R)]
    return jnp.stack(slabs, axis=1).reshape(rpb.shape[0], WIN_R, B_HEADS * GRID_W, WIN_R * GRID_W)


def _na_latent(main_l, main_c, bias, layer):
    b_, s, _ = main_l.shape
    lc = main_c.shape[1]
    rows = s // GRID_W
    assert rows >= WIN_R and rows % NA_ROWS_PER_STEP == 0
    tq = NA_ROWS_PER_STEP * GRID_W
    col = lambda idx: (lambda b, i: (b, 0, idx))
    return pl.pallas_call(
        functools.partial(_na_kernel, rows),
        grid=(b_, rows // NA_ROWS_PER_STEP),
        in_specs=[pl.BlockSpec((1, tq, B_W), lambda b, i: (b, i, 8)),
                  pl.BlockSpec((1, s, B_W), col(9)),
                  pl.BlockSpec((1, s, B_W), col(10)),
                  pl.BlockSpec((1, lc, B_W), col(9)),
                  pl.BlockSpec((1, lc, B_W), col(10)),
                  pl.BlockSpec((1, tq, B_W), lambda b, i: (b, i, 11)),
                  pl.BlockSpec((None,) + bias.shape[1:], lambda b, i: (layer, 0, 0, 0))],
        out_specs=pl.BlockSpec((1, tq, B_W), lambda b, i: (b, i, 0)),
        out_shape=jax.ShapeDtypeStruct((b_, s, B_W), BF16),
        compiler_params=pltpu.CompilerParams(vmem_limit_bytes=VMEM_LIMIT),
        name="na_latent",
    )(main_l, main_l, main_l, main_c, main_c, main_l, bias)


def _na_ctx_kernel(q_ref, k_ref, v_ref, z_ref, o_ref):
    lc = q_ref.shape[1]
    qs = _stack_heads(q_ref[0] * (B_DH ** -0.5))
    s = _nt(qs, k_ref[0])
    p = jnp.exp(s - jnp.max(s, -1, keepdims=True))
    o = _mm(p.astype(BF16), v_ref[0]) / jnp.sum(p, -1, keepdims=True)
    o = _unstack_heads(o, lc)
    o_ref[0] = (o * _silu(z_ref[0].astype(F32))).astype(BF16)


def _na_ctx(main_c):
    b_, lc, _ = main_c.shape
    col = lambda idx: pl.BlockSpec((1, lc, B_W), lambda b: (b, 0, idx))
    return pl.pallas_call(
        _na_ctx_kernel,
        grid=(b_,),
        in_specs=[col(8), col(9), col(10), col(11)],
        out_specs=pl.BlockSpec((1, lc, B_W), lambda b: (b, 0, 0)),
        out_shape=jax.ShapeDtypeStruct((b_, lc, B_W), BF16),
        name="na_ctx",
    )(main_c, main_c, main_c, main_c)


GLA_TILE = 128
GLA_INTRA_CHUNKS = 2
GLA_SCAN_STEPS = 16


def _bwd_chunk(n, nc_c, nc_l):
    return jnp.where(n < nc_c, nc_c - 1 - n, nc_c + nc_l - 1 - (n - nc_c))


def _gla_kernel(lc, ll,
                qc_ref, kc_ref, vc_ref, rc_ref, zc_ref, ql_ref, kl_ref, vl_ref, rl_ref, zl_ref,
                cos_ref, sin_ref, w2f_ref, w2b_ref, b2_ref, nw_ref, oc_ref, ol_ref,
                q_s, k_s, v_s, bf_s, bb_s, qdf_s, kdf_s, qdb_s, kdb_s, glf_s, glb_s, of_s, ob_s, st_s):
    c = C_CHUNK
    nc_c, nc_l = lc // c, ll // c
    per_tile = GLA_TILE // c
    scale = C_DK ** -0.5

    ti = _iota2((GLA_TILE, GLA_TILE), 0)
    tj = _iota2((GLA_TILE, GLA_TILE), 1)
    same = (ti // c) == (tj // c)
    tri_l = jnp.where(same & (tj <= ti), 1.0, 0.0).astype(BF16)
    tri_u = jnp.where(same & (tj >= ti), 1.0, 0.0).astype(BF16)
    rot = (jnp.where((tj % 16 < 8) & (ti == tj + 8), -1.0, 0.0)
           + jnp.where((tj % 16 >= 8) & (ti == tj - 8), 1.0, 0.0)).astype(BF16)
    expand = jnp.where(_iota2((C_QW, C_VW), 0) // C_DK == _iota2((C_QW, C_VW), 1) // C_DV,
                       1.0, 0.0).astype(BF16)
    st_mask = jnp.where(_iota2((C_VW, C_QW), 0) // C_DV == _iota2((C_VW, C_QW), 1) // C_DK,
                        1.0, 0.0).astype(F32)
    head_ones = jnp.where(_iota2((C_VW, C_VW), 0) // C_DV == _iota2((C_VW, C_VW), 1) // C_DV,
                          1.0, 0.0).astype(BF16)

    def log_sigmoid(x):
        return jnp.minimum(x, 0.0) - jnp.log(1.0 + jnp.exp(-jnp.abs(x)))

    def prologue(q_ref, k_ref, v_ref, r_ref, base, n_tiles, rope):
        def body(t, carry):
            off = pl.multiple_of(t * GLA_TILE, GLA_TILE)
            uoff = pl.multiple_of(base + t * GLA_TILE, GLA_TILE)
            qb = q_ref[0, pl.ds(off, GLA_TILE), :]
            kb = k_ref[0, pl.ds(off, GLA_TILE), :]
            q = qb.astype(F32)
            k = kb.astype(F32)
            if rope:
                cs = cos_ref[pl.ds(off, GLA_TILE), :]
                sn = sin_ref[pl.ds(off, GLA_TILE), :]
                q = q * cs + _mm(qb, rot) * sn
                k = k * cs + _mm(kb, rot) * sn
            q = q * scale
            r = r_ref[0, pl.ds(off, GLA_TILE), :].astype(BF16)
            gkf = log_sigmoid(_mm(r, w2f_ref[...]) + b2_ref[0:1, :]) / C_GATE_NORM
            gkb = log_sigmoid(_mm(r, w2b_ref[...]) + b2_ref[1:2, :]) / C_GATE_NORM
            bf = _mm_01(tri_l, gkf)
            bb = _mm_01(tri_u, gkb)
            tf = jnp.concatenate([jnp.broadcast_to(bf[i * c + c - 1:i * c + c], (c, C_QW)) for i in range(per_tile)], axis=0)
            tb = jnp.concatenate([jnp.broadcast_to(bb[i * c:i * c + 1], (c, C_QW)) for i in range(per_tile)], axis=0)
            sl = pl.ds(uoff, GLA_TILE)
            q_s[sl, :] = q.astype(BF16)
            k_s[sl, :] = k.astype(BF16)
            v_s[sl, :] = v_ref[0, pl.ds(off, GLA_TILE), :]
            bf_s[sl, :] = bf
            bb_s[sl, :] = bb
            qdf_s[sl, :] = (q * jnp.exp(bf)).astype(BF16)
            kdf_s[sl, :] = (k * jnp.exp(tf - bf)).astype(BF16)
            qdb_s[sl, :] = (q * jnp.exp(bb)).astype(BF16)
            kdb_s[sl, :] = (k * jnp.exp(tb - bb)).astype(BF16)
            goff = pl.multiple_of((base // c + t * per_tile) * 8, 8)
            glf_s[pl.ds(goff, per_tile * 8), :] = jnp.exp(
                jnp.concatenate([tf[i * c:i * c + 8] for i in range(per_tile)], axis=0))
            glb_s[pl.ds(goff, per_tile * 8), :] = jnp.exp(
                jnp.concatenate([tb[i * c:i * c + 8] for i in range(per_tile)], axis=0))
            return carry
        lax.fori_loop(0, n_tiles, body, 0, unroll=min(4, n_tiles))

    prologue(qc_ref, kc_ref, vc_ref, rc_ref, 0, lc // GLA_TILE, False)
    prologue(ql_ref, kl_ref, vl_ref, rl_ref, lc, ll // GLA_TILE, True)

    jj = _iota2((c, C_QW), 0)
    row_sum = jnp.where(_iota2((c, c * c), 1) // c == _iota2((c, c * c), 0), 1.0, 0.0).astype(BF16)

    def intra(first_chunk):
        offs = [pl.multiple_of((first_chunk + j) * c, c) for j in range(GLA_INTRA_CHUNKS)]
        ps = []
        for off in offs:
            q = q_s[pl.ds(off, c), :].astype(F32)
            k = k_s[pl.ds(off, c), :].astype(F32)
            bf = bf_s[pl.ds(off, c), :]
            bb = bb_s[pl.ds(off, c), :]
            rows = []
            for i in range(c):
                e = jnp.exp(jnp.where(jj < i, bf[i:i + 1] - bf, bb[i:i + 1] - bb))
                rows.append((jnp.where(jj == i, 2.0, e) * (q[i:i + 1] * k)).astype(BF16))
            ps.append(jnp.concatenate(rows, axis=0))
        return [(off, _mm(p, expand)) for off, p in zip(offs, ps)]

    def intra_apply(weights):
        outs = []
        for off, a in weights:
            v = v_s[pl.ds(off, c), :].astype(F32)
            av = (a.reshape(c, c, C_VW) * v[None]).reshape(c * c, C_VW).astype(BF16)
            outs.append(_mm(row_sum, av))
        return outs

    st_s[...] = jnp.zeros_like(st_s)
    dirs = ((qdf_s, kdf_s, glf_s), (qdb_s, kdb_s, glb_s))

    def scan(it, carry):
        steps = [it * GLA_SCAN_STEPS + s for s in range(GLA_SCAN_STEPS)]
        chunk = [steps, [_bwd_chunk(n, nc_c, nc_l) for n in steps]]
        sls = [[pl.ds(pl.multiple_of(ci * c, c), c) for ci in chunk[d]] for d in range(2)]
        upd = [[_tn(v_s[sl, :], dirs[d][1][sl, :]) for sl in sls[d]] for d in range(2)]
        weights = []
        for s in range(0, GLA_SCAN_STEPS, GLA_INTRA_CHUNKS):
            weights += intra(steps[s])
        st = [st_s[0], st_s[1]]
        inter = [[], []]
        for s in range(GLA_SCAN_STEPS):
            for d in range(2):
                inter[d].append(_nt(dirs[d][0][sls[d][s], :], st[d].astype(BF16)))
                gl = dirs[d][2][pl.ds(pl.multiple_of(chunk[d][s] * 8, 8), 8), :][0:1]
                st[d] = st[d] * gl + upd[d][s] * st_mask
        st_s[0] = st[0]
        st_s[1] = st[1]
        local = intra_apply(weights)
        for s in range(GLA_SCAN_STEPS):
            of_s[sls[0][s], :] = local[s] + inter[0][s]
            ob_s[sls[1][s], :] = inter[1][s]
        return carry
    lax.fori_loop(0, (nc_c + nc_l) // GLA_SCAN_STEPS, scan, 0)

    def epilogue(z_ref, o_ref, base, n_tiles):
        def body(t, carry):
            off = pl.multiple_of(t * GLA_TILE, GLA_TILE)
            sl = pl.ds(pl.multiple_of(base + t * GLA_TILE, GLA_TILE), GLA_TILE)
            o = of_s[sl, :] + ob_s[sl, :]
            sq = o * o
            hi = sq.astype(BF16)
            lo = (sq - hi.astype(F32)).astype(BF16)
            ms = (_mm(hi, head_ones) + _mm(lo, head_ones)) * (1.0 / C_DV)
            o = o * lax.rsqrt(ms + 1e-6) * nw_ref[...]
            z = z_ref[0, pl.ds(off, GLA_TILE), :].astype(F32)
            o_ref[0, pl.ds(off, GLA_TILE), :] = (o * _silu(z)).astype(BF16)
            return carry
        lax.fori_loop(0, n_tiles, body, 0, unroll=2)

    epilogue(zc_ref, oc_ref, 0, lc // GLA_TILE)
    epilogue(zl_ref, ol_ref, lc, ll // GLA_TILE)


def _rope_tables(s):
    t = np.arange(s)
    nf = C_DK // 4
    inv = ROPE_THETA ** (-np.arange(nf, dtype=np.float64) / nf)
    lane = np.arange(C_DK)
    pos = np.where((lane // (C_DK // 2))[None, :] == 0, (t // GRID_W)[:, None], (t % GRID_W)[:, None])
    ang = pos.astype(np.float32) * inv.astype(np.float32)[lane % nf][None, :]
    cos = np.tile(np.cos(ang).astype(np.float32), (1, C_HEADS))
    sin = np.tile(np.sin(ang).astype(np.float32), (1, C_HEADS))
    return jnp.asarray(cos), jnp.asarray(sin)


def _gla(main_c, small_c, main_l, small_l, w2, b2, norm_w):
    b_, lc, _ = main_c.shape
    ll = main_l.shape[1]
    lt = lc + ll
    assert lc % GLA_TILE == 0 and ll % GLA_TILE == 0
    cos, sin = _rope_tables(ll)
    w2f = jnp.zeros((SMALL_W, C_QW), BF16).at[16:16 + C_RANK].set(w2[0].astype(BF16))
    w2b = jnp.zeros((SMALL_W, C_QW), BF16).at[16 + C_RANK:16 + 2 * C_RANK].set(w2[1].astype(BF16))
    nw = jnp.tile(norm_w.astype(F32), C_HEADS).reshape(1, C_VW)
    nchunks = lt // C_CHUNK

    def seg(l, w, idx):
        return pl.BlockSpec((1, l, w), lambda b: (b, 0, idx))

    def full(shape):
        return pl.BlockSpec(shape, lambda b: (0,) * len(shape))

    return pl.pallas_call(
        functools.partial(_gla_kernel, lc, ll),
        grid=(b_,),
        in_specs=[seg(lc, C_QW, 24), seg(lc, C_QW, 25), seg(lc, C_VW, 13), seg(lc, SMALL_W, 0), seg(lc, C_VW, 14),
                  seg(ll, C_QW, 24), seg(ll, C_QW, 25), seg(ll, C_VW, 13), seg(ll, SMALL_W, 0), seg(ll, C_VW, 14),
                  full((ll, C_QW)), full((ll, C_QW)), full((SMALL_W, C_QW)), full((SMALL_W, C_QW)),
                  full((2, C_QW)), full((1, C_VW))],
        out_specs=[seg(lc, C_VW, 0), seg(ll, C_VW, 0)],
        out_shape=[jax.ShapeDtypeStruct((b_, lc, C_VW), BF16), jax.ShapeDtypeStruct((b_, ll, C_VW), BF16)],
        scratch_shapes=[pltpu.VMEM((lt, C_QW), BF16), pltpu.VMEM((lt, C_QW), BF16), pltpu.VMEM((lt, C_VW), BF16),
                        pltpu.VMEM((lt, C_QW), F32), pltpu.VMEM((lt, C_QW), F32),
                        pltpu.VMEM((lt, C_QW), BF16), pltpu.VMEM((lt, C_QW), BF16),
                        pltpu.VMEM((lt, C_QW), BF16), pltpu.VMEM((lt, C_QW), BF16),
                        pltpu.VMEM((nchunks * 8, C_QW), F32), pltpu.VMEM((nchunks * 8, C_QW), F32),
                        pltpu.VMEM((lt, C_VW), F32), pltpu.VMEM((lt, C_VW), F32),
                        pltpu.VMEM((2, C_VW, C_QW), F32)],
        compiler_params=pltpu.CompilerParams(vmem_limit_bytes=VMEM_LIMIT),
        name="gla",
    )(main_c, main_c, main_c, small_c, main_c, main_l, main_l, main_l, small_l, main_l,
      cos, sin, w2f, w2b, b2.astype(F32), nw)


GDN_CHUNK = 128
GDN_BASE = 16
GDN_LEVELS = 3
M_INCL, M_STRICT, M_BASE, M_LEVEL, M_EYE = 0, 2, 4, 5, 8
N_MASKS = 9
GDN_PRE_CHUNKS = 4
GDN_REC_FIRST_STAGE = 2
CONV_HALO = 16
CONV_TILE = 256


def _gdn_kernel(lc, ll, n_seq,
                xqc_ref, xkc_ref, xvc_ref, sc_ref, zc_ref, xql_ref, xkl_ref, xvl_ref, sl_ref, zl_ref,
                cwq_ref, cwk_ref, cwv_ref, alog_ref, dtb_ref, nw_ref, oc_ref, ol_ref,
                pad_s, q_s, k_s, v_s, gf_s, gb_s, bef_s, beb_s, u_s, w_s, qd_s, kdt_s, qk_s, gl_s, msk_s, o_s, st_s):
    c = GDN_CHUNK
    nc_c, nc_l = lc // c, ll // c
    step = pl.program_id(0)
    h = jnp.minimum(step, n_seq - 1) % A_HEADS
    cur = step % 2
    prev = 1 - cur

    @pl.when(step == 0)
    def _():
        for buf in (u_s, w_s, qd_s, kdt_s, qk_s, gl_s):
            buf[1] = jnp.zeros(buf.shape[1:], buf.dtype)
        ti = _iota2((c, c), 0)
        tj = _iota2((c, c), 1)
        as_f32 = lambda m: jnp.where(m, 1.0, 0.0).astype(F32)
        msk_s[M_INCL] = as_f32(tj <= ti)
        msk_s[M_INCL + 1] = as_f32(tj >= ti)
        msk_s[M_STRICT] = as_f32(tj < ti)
        msk_s[M_STRICT + 1] = as_f32(tj > ti)
        msk_s[M_BASE] = -as_f32(ti // GDN_BASE == tj // GDN_BASE)
        for lv in range(GDN_LEVELS):
            s = GDN_BASE << lv
            msk_s[M_LEVEL + lv] = as_f32((ti // (2 * s) == tj // (2 * s)) & (ti // s != tj // s))
        msk_s[M_EYE] = as_f32(ti == tj)

    def conv_group(parts, seg_base, l, g0, n_tok):
        t = CONV_TILE
        for p, (x_ref, w_ref, dst, mode) in enumerate(parts):
            lo = _aligned(jnp.maximum(g0 - CONV_HALO, 0), CONV_HALO)
            hi = _aligned(jnp.minimum(g0 + n_tok, l - CONV_HALO), CONV_HALO)
            left = x_ref[0, pl.ds(lo, CONV_HALO), :].astype(F32)
            right = x_ref[0, pl.ds(hi, CONV_HALO), :].astype(F32)
            pad_s[p, 0:CONV_HALO, :] = jnp.where(g0 > 0, left, 0.0)
            pad_s[p, CONV_HALO:CONV_HALO + n_tok, :] = x_ref[0, pl.ds(_aligned(g0, n_tok), n_tok), :].astype(F32)
            pad_s[p, CONV_HALO + n_tok:2 * CONV_HALO + n_tok, :] = jnp.where(g0 + n_tok < l, right, 0.0)
            w = w_ref[...]
            for t0 in range(0, n_tok, t):
                acc = jnp.zeros((t, LANES), F32)
                for j in range(CONV_K):
                    r0 = CONV_HALO - CONV_K // 2 + j + t0
                    acc = acc + pad_s[p, r0:r0 + t, :] * w[j:j + 1]
                y = _silu(acc)
                if mode != "v":
                    y = y * lax.rsqrt(jnp.sum(y * y, -1, keepdims=True) + 1e-6)
                if mode == "q":
                    y = y * (A_DK ** -0.5)
                dst[pl.ds(_aligned(seg_base + g0 + t0, t), t), :] = y.astype(dst.dtype)

    parts_c = ((xqc_ref, cwq_ref, q_s, "q"), (xkc_ref, cwk_ref, k_s, "k"), (xvc_ref, cwv_ref, v_s, "v"))
    parts_l = ((xql_ref, cwq_ref, q_s, "q"), (xkl_ref, cwk_ref, k_s, "k"), (xvl_ref, cwv_ref, v_s, "v"))

    n_gate = 4 * A_HEADS
    neg_a = -jnp.exp(alog_ref[...])
    chan = _iota2((n_gate, c), 0)

    def col_bcast(rows, idx):
        row = jnp.sum(jnp.where(chan == idx, rows, 0.0), axis=0, keepdims=True)
        return jnp.broadcast_to(row, (c, c)).T

    def gates_group(s_ref, seg_base, g0, n_tok):
        tri_lb = msk_s[M_INCL].astype(BF16)
        tri_ub = msk_s[M_INCL + 1].astype(BF16)
        for t0 in range(0, n_tok, c):
            xt = s_ref[0, pl.ds(_aligned(g0 + t0, c), c), :].T[:n_gate]
            g = jnp.where(chan < 2 * A_HEADS, neg_a * _softplus(xt + dtb_ref[...]), _sigmoid(xt))
            cum_f = _mm_01(tri_ub, g, right=True)
            cum_b = _mm_01(tri_lb, g, right=True)
            sl = pl.ds(_aligned(seg_base + g0 + t0, c), c)
            gf_s[sl, :] = col_bcast(cum_f, h)
            gb_s[sl, :] = col_bcast(cum_b, h + A_HEADS)
            bef_s[sl, :] = col_bcast(g, h + 2 * A_HEADS).astype(BF16)
            beb_s[sl, :] = col_bcast(g, h + 3 * A_HEADS).astype(BF16)

    def front_end(seg, g0, n_tok):
        parts, s_ref, base, l = ((parts_c, sc_ref, 0, lc), (parts_l, sl_ref, lc, ll))[seg]
        conv_group(parts, base, l, g0, n_tok)
        gates_group(s_ref, base, g0, n_tok)

    def prepass(first_chunk, per_step, it):
        chains = []
        for j in range(per_step):
            ci = first_chunk + it * per_step + j
            sl = pl.ds(_aligned(ci * c, c), c)
            kbf = k_s[sl, :]
            q = q_s[sl, :].astype(F32)
            k = kbf.astype(F32)
            gram = _nt(kbf, kbf)
            qk_raw = _nt(q_s[sl, :], kbf)
            for d, (g_s, be_s) in enumerate(((gf_s, bef_s), (gb_s, beb_s))):
                chains.append(dict(ci=ci, sl=sl, d=d, q=q, k=k, v=v_s[sl, :].astype(F32), gram=gram, qk_raw=qk_raw,
                                   gc=g_s[sl, :], beta=be_s[sl, :].astype(F32)))
        yield
        for ch in chains:
            gc = ch["gc"]
            ch["decay"] = jnp.exp(jnp.minimum(gc - gc.T, 0.0)) * msk_s[M_INCL + ch["d"]]
            ch["a"] = ch["gram"] * ch["beta"] * ch["decay"] * msk_s[M_STRICT + ch["d"]]
            n0 = ch["a"] * msk_s[M_BASE]
            ch["inv"] = msk_s[M_EYE] + n0
            ch["pw"] = n0.astype(BF16)
        yield
        n_sq = GDN_BASE.bit_length() - 2
        for ch in chains:
            ch["pw"] = _mm(ch["pw"], ch["pw"]).astype(BF16)
        yield
        for m in range(n_sq):
            for ch in chains:
                ch["inv_next"] = ch["inv"] + _mm(ch["inv"].astype(BF16), ch["pw"])
            if m + 1 < n_sq:
                for ch in chains:
                    ch["pw"] = _mm(ch["pw"], ch["pw"]).astype(BF16)
            for ch in chains:
                ch["inv"] = ch["inv_next"]
            yield
        for lv in range(GDN_LEVELS):
            for ch in chains:
                ch["invb"] = ch["inv"].astype(BF16)
                ch["t"] = _mm((ch["a"] * msk_s[M_LEVEL + lv]).astype(BF16), ch["invb"]).astype(BF16)
            yield
            for ch in chains:
                ch["inv"] = ch["inv"] - _mm(ch["invb"], ch["t"])
            yield
        for ch in chains:
            gc, beta, q, k, d, sl = ch["gc"], ch["beta"], ch["q"], ch["k"], ch["d"], ch["sl"]
            eg = jnp.exp(gc)
            rhs = jnp.concatenate([ch["v"] * beta, k * beta * eg], axis=-1).astype(BF16)
            sol = _mm(ch["inv"].astype(BF16), rhs)
            g_last = gc[c - 1:c] if d == 0 else gc[0:1]
            u_s[cur, d, sl, :] = sol[:, :A_DV].astype(BF16)
            w_s[cur, d, sl, :] = sol[:, A_DV:].astype(BF16)
            qk_s[cur, d, sl, :] = (ch["qk_raw"] * ch["decay"]).astype(BF16)
            qd_s[cur, d, sl, :] = (q * eg).astype(BF16)
            kdt_s[cur, d, sl, :] = (k * jnp.exp(g_last - gc)).T.astype(BF16)
            gl_s[cur, d, pl.ds(_aligned(ch["ci"] * 8, 8), 8), :] = jnp.exp(
                jnp.broadcast_to(g_last, (8, LANES)))

    def recurrence(first_step, n_steps, it):
        st = [st_s[0], st_s[1]]
        for j in range(n_steps):
            n = first_step + it * n_steps + j
            cis = (n, _bwd_chunk(n, nc_c, nc_l))
            sls = [pl.ds(_aligned(ci * c, c), c) for ci in cis]
            stb = [x.astype(BF16) for x in st]
            w_st = [_mm(w_s[prev, d, sls[d], :], stb[d]) for d in range(2)]
            o_st = [_mm(qd_s[prev, d, sls[d], :], stb[d]) for d in range(2)]
            yield
            vnb = [(u_s[prev, d, sls[d], :].astype(F32) - w_st[d]).astype(BF16) for d in range(2)]
            for d in range(2):
                gl = gl_s[prev, d, pl.ds(_aligned(cis[d] * 8, 8), 8), :][0:1]
                st[d] = st[d] * gl + _mm(kdt_s[prev, d, sls[d], :], vnb[d])
            for d in range(2):
                o_s[sls[d], :] += o_st[d] + _mm(qk_s[prev, d, sls[d], :], vnb[d])
            yield
        st_s[0] = st[0]
        st_s[1] = st[1]

    o_s[...] = jnp.zeros_like(o_s)
    st_s[...] = jnp.zeros_like(st_s)

    def merged(first, per_step, next_group, it, carry):
        pre = prepass(first, per_step, it)
        rec = recurrence(first, per_step, it)
        for i, _ in enumerate(pre):
            if i == 0 and next_group is not None:
                front_end(*next_group(it))
            if i >= GDN_REC_FIRST_STAGE:
                next(rec, None)
        for _ in rec:
            pass
        return carry

    per_c = math.gcd(nc_c, GDN_PRE_CHUNKS)
    per_l = math.gcd(nc_l, GDN_PRE_CHUNKS)
    it_c, it_l = nc_c // per_c, nc_l // per_l
    front_end(0, 0, per_c * c)
    if it_c > 1:
        lax.fori_loop(0, it_c - 1, functools.partial(
            merged, 0, per_c, lambda it: (0, (it + 1) * (per_c * c), per_c * c)), 0)
    merged(0, per_c, lambda it: (1, 0, per_l * c), it_c - 1, 0)
    if it_l > 1:
        lax.fori_loop(0, it_l - 1, functools.partial(
            merged, nc_c, per_l, lambda it: (1, (it + 1) * (per_l * c), per_l * c)), 0)
    merged(nc_c, per_l, None, it_l - 1, 0)

    def epilogue(z_ref, o_ref, base, n_tiles):
        def body(t, carry):
            off = pl.multiple_of(t * GDN_CHUNK, GDN_CHUNK)
            o = o_s[pl.ds(pl.multiple_of(base + t * GDN_CHUNK, GDN_CHUNK), GDN_CHUNK), :]
            o = o * lax.rsqrt(jnp.mean(o * o, -1, keepdims=True) + 1e-6) * nw_ref[...]
            z = z_ref[0, pl.ds(off, GDN_CHUNK), :].astype(F32)
            o_ref[0, pl.ds(off, GDN_CHUNK), :] = (o * _silu(z)).astype(BF16)
            return carry
        lax.fori_loop(0, n_tiles, body, 0, unroll=2)

    epilogue(zc_ref, oc_ref, 0, lc // GDN_CHUNK)
    epilogue(zl_ref, ol_ref, lc, ll // GDN_CHUNK)


def _gdn(main_c, small_c, main_l, small_l, conv_w, a_log, dt_bias, norm_w):
    b_, lc, _ = main_c.shape
    ll = main_l.shape[1]
    lt = lc + ll
    assert lc % 256 == 0 and ll % 256 == 0
    n_gate = 4 * A_HEADS
    pad8 = lambda p: jnp.broadcast_to(
        jnp.zeros((n_gate,), F32).at[:2 * A_HEADS].set(p.astype(F32).reshape(-1))[:, None], (n_gate, GDN_CHUNK))
    cw = conv_w.astype(F32)

    n_seq = b_ * A_HEADS
    pre_seq = lambda s: jnp.minimum(s, n_seq - 1)
    rec_seq = lambda s: jnp.maximum(s - 1, 0)

    def seg(l, w, part, seq):
        return pl.BlockSpec((1, l, w), lambda s: (seq(s) // A_HEADS, 0, part * A_HEADS + seq(s) % A_HEADS))

    def cw_spec(part):
        return pl.BlockSpec((CONV_K, A_DK), lambda s: (0, part * A_HEADS + pre_seq(s) % A_HEADS))

    const = lambda shape: pl.BlockSpec(shape, lambda s: (0,) * len(shape))
    in_specs = []
    for l in (lc, ll):
        in_specs += [seg(l, A_DK, 0, pre_seq), seg(l, A_DK, 1, pre_seq), seg(l, A_DK, 2, pre_seq),
                     pl.BlockSpec((1, l, SMALL_W), lambda s: (pre_seq(s) // A_HEADS, 0, 0)),
                     seg(l, A_DV, 3, rec_seq)]
    in_specs += [cw_spec(0), cw_spec(1), cw_spec(2), const((n_gate, GDN_CHUNK)), const((n_gate, GDN_CHUNK)),
                 const((1, A_DV))]
    tok = lambda dt: pltpu.VMEM((lt, LANES), dt)
    halves = lambda: pltpu.VMEM((2, 2, lt, LANES), BF16)
    return pl.pallas_call(
        functools.partial(_gdn_kernel, lc, ll, n_seq),
        grid=(n_seq + 1,),
        in_specs=in_specs,
        out_specs=[seg(lc, A_DV, 0, rec_seq), seg(ll, A_DV, 0, rec_seq)],
        out_shape=[jax.ShapeDtypeStruct((b_, lc, A_W), BF16), jax.ShapeDtypeStruct((b_, ll, A_W), BF16)],
        scratch_shapes=[pltpu.VMEM((3, min(GDN_PRE_CHUNKS * GDN_CHUNK, max(lc, ll)) + 2 * CONV_HALO, LANES), F32),
                        tok(BF16), tok(BF16), tok(BF16), tok(F32), tok(F32), tok(BF16), tok(BF16),
                        halves(), halves(), halves(), halves(), halves(),
                        pltpu.VMEM((2, 2, (lt // GDN_CHUNK) * 8, LANES), F32),
                        pltpu.VMEM((N_MASKS, GDN_CHUNK, GDN_CHUNK), F32),
                        tok(F32), pltpu.VMEM((2, A_DK, A_DV), F32)],
        compiler_params=pltpu.CompilerParams(vmem_limit_bytes=GDN_VMEM_LIMIT,
                                             dimension_semantics=("arbitrary",)),
        name="gdn",
    )(main_c, main_c, main_c, small_c, main_c, main_l, main_l, main_l, small_l, main_l,
      cw, cw, cw, pad8(a_log), pad8(dt_bias), norm_w.astype(F32).reshape(1, A_DV))


def _permute_w_in(w):
    sizes = (3 * A_W, 4 * A_HEADS, A_W, B_W, B_W, B_W, B_W, C_QW, C_QW, C_VW, 2 * C_RANK, C_VW)
    offs = np.concatenate([[0], np.cumsum(sizes)])
    part = lambda i: w[..., offs[i]:offs[i + 1]]
    pad = jnp.zeros(w.shape[:-1] + (SMALL_W - 4 * A_HEADS - 2 * C_RANK,), w.dtype)
    order = [part(0), part(2), part(3), part(4), part(5), part(6), part(7), part(8), part(9), part(11),
             part(1), part(10), pad]
    return jnp.concatenate(order, axis=-1).astype(BF16)


def kernel(x, c, ctx, c_ctx, w_mod, b_mod, w_in, conv_w, a_log, dt_bias, gdn_norm, rpb,
           gla_w2, gla_b2, gla_norm, w_out, ln_g, ln_b):
    depth = w_mod.shape[0]
    b_ = x.shape[0]
    alpha = (2 * depth) ** 0.25
    n_mod = -(-(b_ + 1) // 8) * 8
    c_all = jnp.zeros((n_mod, D_MODEL), F32).at[:b_].set(c).at[b_].set(c_ctx)
    w_perm = _permute_w_in(w_in)
    w_o = w_out.astype(BF16)
    na_bias = _na_bias_table(rpb)
    b_mod3 = b_mod.reshape(depth, 1, -1)
    xl, xc = x, ctx
    for i in range(depth):
        ctx_out = i < depth - 1
        mod = _modulation(c_all, w_mod, b_mod3, i).reshape(n_mod, 3, 1, D_MODEL)
        sh_l, sc_l, gt_l = mod[:b_, 0], mod[:b_, 1], mod[:b_, 2]
        sh_c, sc_c, gt_c = mod[b_:b_ + 1, 0], mod[b_:b_ + 1, 1], mod[b_:b_ + 1, 2]
        main_l, small_l = _project(xl, sh_l, sc_l, w_perm, i, True)
        flat = lambda t: t.reshape(1, -1, t.shape[-1])
        main_c, small_c = (t.reshape(b_, -1, t.shape[-1]) for t in _project(flat(xc), sh_c, sc_c, w_perm, i, False))
        ya_c, ya_l = _gdn(main_c, small_c, main_l, small_l, conv_w[i], a_log[i], dt_bias[i], gdn_norm[i])
        yb_l = _na_latent(main_l, main_c, na_bias, i)
        yc_c, yc_l = _gla(main_c, small_c, main_l, small_l, gla_w2[i], gla_b2[i], gla_norm[i])
        if ctx_out:
            yb_c = _na_ctx(main_c)
            xc = _out_project(flat(xc), flat(ya_c), flat(yb_c), flat(yc_c), w_o, i, gt_c, ln_g[i], ln_b[i],
                              False, alpha).reshape(xc.shape)
        xl = _out_project(xl, ya_l, yb_l, yc_l, w_o, i, gt_l, ln_g[i], ln_b[i], True, alpha)
    return xl
```

```python
import functools
import math

import numpy as np
import jax
import jax.numpy as jnp
from jax import lax
from jax.experimental import pallas as pl
from jax.experimental.pallas import tpu as pltpu

F32 = jnp.float32
BF16 = jnp.bfloat16
HI = lax.Precision.HIGHEST

D_MODEL = 1024
GRID_W = 64
A_HEADS, A_DK, A_DV, CONV_K, A_CHUNK = 4, 128, 128, 5, 64
B_HEADS, B_DH, WIN_R, WIN_C = 4, 64, 8, 16
C_HEADS, C_DK, C_DV, C_RANK, C_CHUNK = 4, 32, 64, 16, 16
C_GATE_NORM = 16.0
ROPE_THETA = 10000.0
LN_EPS = 1e-6
NEG_INF = -1e30

A_W = A_HEADS * A_DK
B_W = B_HEADS * B_DH
C_QW = C_HEADS * C_DK
C_VW = C_HEADS * C_DV
MAIN_W = 3 * A_W + A_W + 4 * B_W + 2 * C_QW + 2 * C_VW
SMALL_W = 128
PROJ_W = MAIN_W + SMALL_W
LANES = 128
VMEM_LIMIT = 56 * 1024 * 1024
GDN_VMEM_LIMIT = 60 * 1024 * 1024
PROJ_TILE = 1024


def _nt(a, b, precision=None):
    return lax.dot_general(a, b, (((1,), (1,)), ((), ())), precision=precision,
                           preferred_element_type=F32)


def _tn(a, b):
    return lax.dot_general(a, b, (((0,), (0,)), ((), ())), preferred_element_type=F32)


def _mm(a, b, precision=None):
    return jnp.dot(a, b, precision=precision, preferred_element_type=F32)


def _mm_01(m01, x, right=False):
    hi = x.astype(BF16)
    lo = (x - hi.astype(F32)).astype(BF16)
    return _mm(hi, m01) + _mm(lo, m01) if right else _mm(m01, hi) + _mm(m01, lo)


def _sigmoid(x):
    return 1.0 / (1.0 + jnp.exp(-x))


def _silu(x):
    return x * _sigmoid(x)


def _softplus(x):
    return jnp.maximum(x, 0.0) + jnp.log(1.0 + jnp.exp(-jnp.abs(x)))


def _iota2(shape, axis):
    return lax.broadcasted_iota(jnp.int32, shape, axis)


def _aligned(x, m):
    return x if isinstance(x, int) else pl.multiple_of(x, m)


def _mod_kernel(c_ref, w_ref, b_ref, o_ref):
    o_ref[...] = _mm(_silu(c_ref[...]), w_ref[...], HI) + b_ref[...]


def _modulation(c_all, w_mod, b_mod, layer):
    n = c_all.shape[0]
    tn = 768
    return pl.pallas_call(
        _mod_kernel,
        grid=(3 * D_MODEL // tn,),
        in_specs=[pl.BlockSpec((n, D_MODEL), lambda j: (0, 0)),
                  pl.BlockSpec((None, D_MODEL, tn), lambda j: (layer, 0, j)),
                  pl.BlockSpec((None, 1, tn), lambda j: (layer, 0, j))],
        out_specs=pl.BlockSpec((n, tn), lambda j: (0, j)),
        out_shape=jax.ShapeDtypeStruct((n, 3 * D_MODEL), F32),
        name="adaln_mod",
    )(c_all, w_mod, b_mod)


def _layer_norm(x):
    mu = jnp.mean(x, -1, keepdims=True)
    xc = x - mu
    var = jnp.mean(xc * xc, -1, keepdims=True)
    return xc * lax.rsqrt(var + LN_EPS)


def _proj_kernel(x_ref, sh_ref, sc_ref, w_ref, o_ref, s_ref):
    m = (_layer_norm(x_ref[0]) * (1.0 + sc_ref[0]) + sh_ref[0]).astype(BF16)
    step = 768
    for c0 in range(0, MAIN_W, step):
        o_ref[0, :, c0:c0 + step] = _mm(m, w_ref[:, c0:c0 + step]).astype(BF16)
    s_ref[0] = _mm(m, w_ref[:, MAIN_W:])


def _project(x, sh, sc, w_perm, layer, per_batch):
    b_, l, _ = x.shape
    t = min(PROJ_TILE, l)
    mod_map = (lambda b, i: (b, 0, 0)) if per_batch else (lambda b, i: (0, 0, 0))
    return pl.pallas_call(
        _proj_kernel,
        grid=(b_, l // t),
        in_specs=[pl.BlockSpec((1, t, D_MODEL), lambda b, i: (b, i, 0)),
                  pl.BlockSpec((1, 1, D_MODEL), mod_map),
                  pl.BlockSpec((1, 1, D_MODEL), mod_map),
                  pl.BlockSpec((None, D_MODEL, PROJ_W), lambda b, i: (layer, 0, 0))],
        out_specs=[pl.BlockSpec((1, t, MAIN_W), lambda b, i: (b, i, 0)),
                   pl.BlockSpec((1, t, SMALL_W), lambda b, i: (b, i, 0))],
        out_shape=[jax.ShapeDtypeStruct((b_, l, MAIN_W), BF16),
                   jax.ShapeDtypeStruct((b_, l, SMALL_W), F32)],
        compiler_params=pltpu.CompilerParams(vmem_limit_bytes=VMEM_LIMIT),
        name="ln_mod_proj",
    )(x, sh, sc, w_perm)


def _out_kernel(alpha, x_ref, ya_ref, yb_ref, yc_ref, w_ref, gt_ref, g_ref, b_ref, o_ref):
    y = jnp.concatenate([ya_ref[0], yb_ref[0], yc_ref[0]], axis=-1)
    h = alpha * x_ref[0] + gt_ref[0] * _mm(y, w_ref[...])
    o_ref[0] = _layer_norm(h) * g_ref[...] + b_ref[...]


def _out_project(x, ya, yb, yc, w_out, layer, gt, ln_g, ln_b, per_batch, alpha):
    b_, l, _ = x.shape
    t = min(PROJ_TILE, l)
    mod_map = (lambda b, i: (b, 0, 0)) if per_batch else (lambda b, i: (0, 0, 0))
    tok = lambda w: pl.BlockSpec((1, t, w), lambda b, i: (b, i, 0))
    return pl.pallas_call(
        functools.partial(_out_kernel, alpha),
        grid=(b_, l // t),
        in_specs=[tok(D_MODEL), tok(A_W), tok(B_W), tok(C_VW),
                  pl.BlockSpec((None, D_MODEL, D_MODEL), lambda b, i: (layer, 0, 0)),
                  pl.BlockSpec((1, 1, D_MODEL), mod_map),
                  pl.BlockSpec((1, D_MODEL), lambda b, i: (0, 0)),
                  pl.BlockSpec((1, D_MODEL), lambda b, i: (0, 0))],
        out_specs=tok(D_MODEL),
        out_shape=jax.ShapeDtypeStruct((b_, l, D_MODEL), F32),
        compiler_params=pltpu.CompilerParams(vmem_limit_bytes=VMEM_LIMIT),
        name="out_proj_postnorm",
    )(x, ya, yb, yc, w_out, gt, ln_g.reshape(1, -1), ln_b.reshape(1, -1))


NA_ROWS_PER_STEP = 8


def _stack_heads(q):
    head = _iota2(q.shape, 1) // B_DH
    return jnp.concatenate([jnp.where(head == h, q, jnp.zeros_like(q)) for h in range(B_HEADS)], axis=0)


def _unstack_heads(o, n):
    head = _iota2((n, B_W), 1) // B_DH
    out = jnp.zeros((n, B_W), F32)
    for h in range(B_HEADS):
        out = jnp.where(head == h, o[h * n:(h + 1) * n], out)
    return out


def _na_kernel(rows, q_ref, k_ref, v_ref, kc_ref, vc_ref, z_ref, bias_ref, o_ref):
    kc = kc_ref[0]
    vc = vc_ref[0]
    n_win = WIN_R * GRID_W
    for rr in range(NA_ROWS_PER_STEP):
        r = pl.program_id(1) * NA_ROWS_PER_STEP + rr
        r0 = jnp.clip(r - WIN_R // 2, 0, rows - WIN_R)
        start = pl.multiple_of(r0 * GRID_W, GRID_W)
        kw = k_ref[0, pl.ds(start, n_win), :]
        vw = v_ref[0, pl.ds(start, n_win), :]
        tok = slice(rr * GRID_W, (rr + 1) * GRID_W)
        qs = _stack_heads(q_ref[0, tok, :] * (B_DH ** -0.5))
        s_win = _nt(qs, kw)
        s_ctx = _nt(qs, kc)
        p_win, p_ctx, inv = [], [], []
        for h in range(B_HEADS):
            blk = slice(h * GRID_W, (h + 1) * GRID_W)
            sw = s_win[blk] + bias_ref[r - r0, blk, :]
            sc = s_ctx[blk]
            m = jnp.maximum(jnp.max(sw, -1, keepdims=True), jnp.max(sc, -1, keepdims=True))
            pw = jnp.exp(sw - m)
            pc = jnp.exp(sc - m)
            inv.append(1.0 / (jnp.sum(pw, -1, keepdims=True) + jnp.sum(pc, -1, keepdims=True)))
            p_win.append(pw.astype(BF16))
            p_ctx.append(pc.astype(BF16))
        o = _mm(jnp.concatenate(p_win, axis=0), vw) + _mm(jnp.concatenate(p_ctx, axis=0), vc)
        o = _unstack_heads(o * jnp.concatenate(inv, axis=0), GRID_W)
        o_ref[0, tok, :] = (o * _silu(z_ref[0, tok, :].astype(F32))).astype(BF16)


def _na_bias_table(rpb):
    cq = np.arange(GRID_W)
    c0 = np.clip(cq - WIN_C // 2, 0, GRID_W - WIN_C)
    col_ok = (cq[None, :] >= c0[:, None]) & (cq[None, :] < c0[:, None] + WIN_C)
    dj = np.clip(cq[None, :] - cq[:, None] + (WIN_C - 1), 0, 2 * WIN_C - 2)
    onehot = (dj[None] == np.arange(2 * WIN_C - 1)[:, None, None]).astype(np.float32)
    t = jnp.einsum('lhdk,kqc->lhdqc', rpb.astype(F32), jnp.asarray(onehot), precision=HI)
    t = jnp.where(col_ok[None, None, None], t, NEG_INF)
    slabs = [jnp.transpose(t[:, :, WIN_R - 1 - v:2 * WIN_R - 1 - v], (0, 1, 3, 2, 4)) for v in range(WIN_R)]
    return jnp.stack(slabs, axis=1).reshape(rpb.shape[0], WIN_R, B_HEADS * GRID_W, WIN_R * GRID_W)


def _na_latent(main_l, main_c, bias, layer):
    b_, s, _ = main_l.shape
    lc = main_c.shape[1]
    rows = s // GRID_W
    assert rows >= WIN_R and rows % NA_ROWS_PER_STEP == 0
    tq = NA_ROWS_PER_STEP * GRID_W
    col = lambda idx: (lambda b, i: (b, 0, idx))
    return pl.pallas_call(
        functools.partial(_na_kernel, rows),
        grid=(b_, rows // NA_ROWS_PER_STEP),
        in_specs=[pl.BlockSpec((1, tq, B_W), lambda b, i: (b, i, 8)),
                  pl.BlockSpec((1, s, B_W), col(9)),
                  pl.BlockSpec((1, s, B_W), col(10)),
                  pl.BlockSpec((1, lc, B_W), col(9)),
                  pl.BlockSpec((1, lc, B_W), col(10)),
                  pl.BlockSpec((1, tq, B_W), lambda b, i: (b, i, 11)),
                  pl.BlockSpec((None,) + bias.shape[1:], lambda b, i: (layer, 0, 0, 0))],
        out_specs=pl.BlockSpec((1, tq, B_W), lambda b, i: (b, i, 0)),
        out_shape=jax.ShapeDtypeStruct((b_, s, B_W), BF16),
        compiler_params=pltpu.CompilerParams(vmem_limit_bytes=VMEM_LIMIT),
        name="na_latent",
    )(main_l, main_l, main_l, main_c, main_c, main_l, bias)


def _na_ctx_kernel(q_ref, k_ref, v_ref, z_ref, o_ref):
    lc = q_ref.shape[1]
    qs = _stack_heads(q_ref[0] * (B_DH ** -0.5))
    s = _nt(qs, k_ref[0])
    p = jnp.exp(s - jnp.max(s, -1, keepdims=True))
    o = _mm(p.astype(BF16), v_ref[0]) / jnp.sum(p, -1, keepdims=True)
    o = _unstack_heads(o, lc)
    o_ref[0] = (o * _silu(z_ref[0].astype(F32))).astype(BF16)


def _na_ctx(main_c):
    b_, lc, _ = main_c.shape
    col = lambda idx: pl.BlockSpec((1, lc, B_W), lambda b: (b, 0, idx))
    return pl.pallas_call(
        _na_ctx_kernel,
        grid=(b_,),
        in_specs=[col(8), col(9), col(10), col(11)],
        out_specs=pl.BlockSpec((1, lc, B_W), lambda b: (b, 0, 0)),
        out_shape=jax.ShapeDtypeStruct((b_, lc, B_W), BF16),
        name="na_ctx",
    )(main_c, main_c, main_c, main_c)


GLA_TILE = 128
GLA_INTRA_CHUNKS = 2
GLA_SCAN_STEPS = 16


def _bwd_chunk(n, nc_c, nc_l):
    return jnp.where(n < nc_c, nc_c - 1 - n, nc_c + nc_l - 1 - (n - nc_c))


def _gla_kernel(lc, ll,
                qc_ref, kc_ref, vc_ref, rc_ref, zc_ref, ql_ref, kl_ref, vl_ref, rl_ref, zl_ref,
                cos_ref, sin_ref, w2f_ref, w2b_ref, b2_ref, nw_ref, oc_ref, ol_ref,
                q_s, k_s, v_s, bf_s, bb_s, qdf_s, kdf_s, qdb_s, kdb_s, glf_s, glb_s, of_s, ob_s, st_s):
    c = C_CHUNK
    nc_c, nc_l = lc // c, ll // c
    per_tile = GLA_TILE // c
    scale = C_DK ** -0.5

    ti = _iota2((GLA_TILE, GLA_TILE), 0)
    tj = _iota2((GLA_TILE, GLA_TILE), 1)
    same = (ti // c) == (tj // c)
    tri_l = jnp.where(same & (tj <= ti), 1.0, 0.0).astype(BF16)
    tri_u = jnp.where(same & (tj >= ti), 1.0, 0.0).astype(BF16)
    rot = (jnp.where((tj % 16 < 8) & (ti == tj + 8), -1.0, 0.0)
           + jnp.where((tj % 16 >= 8) & (ti == tj - 8), 1.0, 0.0)).astype(BF16)
    expand = jnp.where(_iota2((C_QW, C_VW), 0) // C_DK == _iota2((C_QW, C_VW), 1) // C_DV,
                       1.0, 0.0).astype(BF16)
    st_mask = jnp.where(_iota2((C_VW, C_QW), 0) // C_DV == _iota2((C_VW, C_QW), 1) // C_DK,
                        1.0, 0.0).astype(F32)
    head_ones = jnp.where(_iota2((C_VW, C_VW), 0) // C_DV == _iota2((C_VW, C_VW), 1) // C_DV,
                          1.0, 0.0).astype(BF16)

    def log_sigmoid(x):
        return jnp.minimum(x, 0.0) - jnp.log(1.0 + jnp.exp(-jnp.abs(x)))

    def prologue(q_ref, k_ref, v_ref, r_ref, base, n_tiles, rope):
        def body(t, carry):
            off = pl.multiple_of(t * GLA_TILE, GLA_TILE)
            uoff = pl.multiple_of(base + t * GLA_TILE, GLA_TILE)
            qb = q_ref[0, pl.ds(off, GLA_TILE), :]
            kb = k_ref[0, pl.ds(off, GLA_TILE), :]
            q = qb.astype(F32)
            k = kb.astype(F32)
            if rope:
                cs = cos_ref[pl.ds(off, GLA_TILE), :]
                sn = sin_ref[pl.ds(off, GLA_TILE), :]
                q = q * cs + _mm(qb, rot) * sn
                k = k * cs + _mm(kb, rot) * sn
            q = q * scale
            r = r_ref[0, pl.ds(off, GLA_TILE), :].astype(BF16)
            gkf = log_sigmoid(_mm(r, w2f_ref[...]) + b2_ref[0:1, :]) / C_GATE_NORM
            gkb = log_sigmoid(_mm(r, w2b_ref[...]) + b2_ref[1:2, :]) / C_GATE_NORM
            bf = _mm_01(tri_l, gkf)
            bb = _mm_01(tri_u, gkb)
            tf = jnp.concatenate([jnp.broadcast_to(bf[i * c + c - 1:i * c + c], (c, C_QW)) for i in range(per_tile)], axis=0)
            tb = jnp.concatenate([jnp.broadcast_to(bb[i * c:i * c + 1], (c, C_QW)) for i in range(per_tile)], axis=0)
            sl = pl.ds(uoff, GLA_TILE)
            q_s[sl, :] = q.astype(BF16)
            k_s[sl, :] = k.astype(BF16)
            v_s[sl, :] = v_ref[0, pl.ds(off, GLA_TILE), :]
            bf_s[sl, :] = bf
            bb_s[sl, :] = bb
            qdf_s[sl, :] = (q * jnp.exp(bf)).astype(BF16)
            kdf_s[sl, :] = (k * jnp.exp(tf - bf)).astype(BF16)
            qdb_s[sl, :] = (q * jnp.exp(bb)).astype(BF16)
            kdb_s[sl, :] = (k * jnp.exp(tb - bb)).astype(BF16)
            goff = pl.multiple_of((base // c + t * per_tile) * 8, 8)
            glf_s[pl.ds(goff, per_tile * 8), :] = jnp.exp(
                jnp.concatenate([tf[i * c:i * c + 8] for i in range(per_tile)], axis=0))
            glb_s[pl.ds(goff, per_tile * 8), :] = jnp.exp(
                jnp.concatenate([tb[i * c:i * c + 8] for i in range(per_tile)], axis=0))
            return carry
        lax.fori_loop(0, n_tiles, body, 0, unroll=min(4, n_tiles))

    prologue(qc_ref, kc_ref, vc_ref, rc_ref, 0, lc // GLA_TILE, False)
    prologue(ql_ref, kl_ref, vl_ref, rl_ref, lc, ll // GLA_TILE, True)

    jj = _iota2((c, C_QW), 0)
    row_sum = jnp.where(_iota2((c, c * c), 1) // c == _iota2((c, c * c), 0), 1.0, 0.0).astype(BF16)

    def intra(first_chunk):
        offs = [pl.multiple_of((first_chunk + j) * c, c) for j in range(GLA_INTRA_CHUNKS)]
        ps = []
        for off in offs:
            q = q_s[pl.ds(off, c), :].astype(F32)
            k = k_s[pl.ds(off, c), :].astype(F32)
            bf = bf_s[pl.ds(off, c), :]
            bb = bb_s[pl.ds(off, c), :]
            rows = []
            for i in range(c):
                e = jnp.exp(jnp.where(jj < i, bf[i:i + 1] - bf, bb[i:i + 1] - bb))
                rows.append((jnp.where(jj == i, 2.0, e) * (q[i:i + 1] * k)).astype(BF16))
            ps.append(jnp.concatenate(rows, axis=0))
        return [(off, _mm(p, expand)) for off, p in zip(offs, ps)]

    def intra_apply(weights):
        outs = []
        for off, a in weights:
            v = v_s[pl.ds(off, c), :].astype(F32)
            av = (a.reshape(c, c, C_VW) * v[None]).reshape(c * c, C_VW).astype(BF16)
            outs.append(_mm(row_sum, av))
        return outs

    st_s[...] = jnp.zeros_like(st_s)
    dirs = ((qdf_s, kdf_s, glf_s), (qdb_s, kdb_s, glb_s))

    def scan(it, carry):
        steps = [it * GLA_SCAN_STEPS + s for s in range(GLA_SCAN_STEPS)]
        chunk = [steps, [_bwd_chunk(n, nc_c, nc_l) for n in steps]]
        sls = [[pl.ds(pl.multiple_of(ci * c, c), c) for ci in chunk[d]] for d in range(2)]
        upd = [[_tn(v_s[sl, :], dirs[d][1][sl, :]) for sl in sls[d]] for d in range(2)]
        weights = []
        for s in range(0, GLA_SCAN_STEPS, GLA_INTRA_CHUNKS):
            weights += intra(steps[s])
        st = [st_s[0], st_s[1]]
        inter = [[], []]
        for s in range(GLA_SCAN_STEPS):
            for d in range(2):
                inter[d].append(_nt(dirs[d][0][sls[d][s], :], st[d].astype(BF16)))
                gl = dirs[d][2][pl.ds(pl.multiple_of(chunk[d][s] * 8, 8), 8), :][0:1]
                st[d] = st[d] * gl + upd[d][s] * st_mask
        st_s[0] = st[0]
        st_s[1] = st[1]
        local = intra_apply(weights)
        for s in range(GLA_SCAN_STEPS):
            of_s[sls[0][s], :] = local[s] + inter[0][s]
            ob_s[sls[1][s], :] = inter[1][s]
        return carry
    lax.fori_loop(0, (nc_c + nc_l) // GLA_SCAN_STEPS, scan, 0)

    def epilogue(z_ref, o_ref, base, n_tiles):
        def body(t, carry):
            off = pl.multiple_of(t * GLA_TILE, GLA_TILE)
            sl = pl.ds(pl.multiple_of(base + t * GLA_TILE, GLA_TILE), GLA_TILE)
            o = of_s[sl, :] + ob_s[sl, :]
            sq = o * o
            hi = sq.astype(BF16)
            lo = (sq - hi.astype(F32)).astype(BF16)
            ms = (_mm(hi, head_ones) + _mm(lo, head_ones)) * (1.0 / C_DV)
            o = o * lax.rsqrt(ms + 1e-6) * nw_ref[...]
            z = z_ref[0, pl.ds(off, GLA_TILE), :].astype(F32)
            o_ref[0, pl.ds(off, GLA_TILE), :] = (o * _silu(z)).astype(BF16)
            return carry
        lax.fori_loop(0, n_tiles, body, 0, unroll=2)

    epilogue(zc_ref, oc_ref, 0, lc // GLA_TILE)
    epilogue(zl_ref, ol_ref, lc, ll // GLA_TILE)


def _rope_tables(s):
    t = np.arange(s)
    nf = C_DK // 4
    inv = ROPE_THETA ** (-np.arange(nf, dtype=np.float64) / nf)
    lane = np.arange(C_DK)
    pos = np.where((lane // (C_DK // 2))[None, :] == 0, (t // GRID_W)[:, None], (t % GRID_W)[:, None])
    ang = pos.astype(np.float32) * inv.astype(np.float32)[lane % nf][None, :]
    cos = np.tile(np.cos(ang).astype(np.float32), (1, C_HEADS))
    sin = np.tile(np.sin(ang).astype(np.float32), (1, C_HEADS))
    return jnp.asarray(cos), jnp.asarray(sin)


def _gla(main_c, small_c, main_l, small_l, w2, b2, norm_w):
    b_, lc, _ = main_c.shape
    ll = main_l.shape[1]
    lt = lc + ll
    assert lc % GLA_TILE == 0 and ll % GLA_TILE == 0
    cos, sin = _rope_tables(ll)
    w2f = jnp.zeros((SMALL_W, C_QW), BF16).at[16:16 + C_RANK].set(w2[0].astype(BF16))
    w2b = jnp.zeros((SMALL_W, C_QW), BF16).at[16 + C_RANK:16 + 2 * C_RANK].set(w2[1].astype(BF16))
    nw = jnp.tile(norm_w.astype(F32), C_HEADS).reshape(1, C_VW)
    nchunks = lt // C_CHUNK

    def seg(l, w, idx):
        return pl.BlockSpec((1, l, w), lambda b: (b, 0, idx))

    def full(shape):
        return pl.BlockSpec(shape, lambda b: (0,) * len(shape))

    return pl.pallas_call(
        functools.partial(_gla_kernel, lc, ll),
        grid=(b_,),
        in_specs=[seg(lc, C_QW, 24), seg(lc, C_QW, 25), seg(lc, C_VW, 13), seg(lc, SMALL_W, 0), seg(lc, C_VW, 14),
                  seg(ll, C_QW, 24), seg(ll, C_QW, 25), seg(ll, C_VW, 13), seg(ll, SMALL_W, 0), seg(ll, C_VW, 14),
                  full((ll, C_QW)), full((ll, C_QW)), full((SMALL_W, C_QW)), full((SMALL_W, C_QW)),
                  full((2, C_QW)), full((1, C_VW))],
        out_specs=[seg(lc, C_VW, 0), seg(ll, C_VW, 0)],
        out_shape=[jax.ShapeDtypeStruct((b_, lc, C_VW), BF16), jax.ShapeDtypeStruct((b_, ll, C_VW), BF16)],
        scratch_shapes=[pltpu.VMEM((lt, C_QW), BF16), pltpu.VMEM((lt, C_QW), BF16), pltpu.VMEM((lt, C_VW), BF16),
                        pltpu.VMEM((lt, C_QW), F32), pltpu.VMEM((lt, C_QW), F32),
                        pltpu.VMEM((lt, C_QW), BF16), pltpu.VMEM((lt, C_QW), BF16),
                        pltpu.VMEM((lt, C_QW), BF16), pltpu.VMEM((lt, C_QW), BF16),
                        pltpu.VMEM((nchunks * 8, C_QW), F32), pltpu.VMEM((nchunks * 8, C_QW), F32),
                        pltpu.VMEM((lt, C_VW), F32), pltpu.VMEM((lt, C_VW), F32),
                        pltpu.VMEM((2, C_VW, C_QW), F32)],
        compiler_params=pltpu.CompilerParams(vmem_limit_bytes=VMEM_LIMIT),
        name="gla",
    )(main_c, main_c, main_c, small_c, main_c, main_l, main_l, main_l, small_l, main_l,
      cos, sin, w2f, w2b, b2.astype(F32), nw)


GDN_CHUNK = 128
GDN_BASE = 16
GDN_LEVELS = 3
M_INCL, M_STRICT, M_BASE, M_LEVEL, M_EYE = 0, 2, 4, 5, 8
N_MASKS = 9
GDN_PRE_CHUNKS = 4
GDN_REC_FIRST_STAGE = 2
CONV_HALO = 16
CONV_TILE = 256


def _gdn_kernel(lc, ll, n_seq,
                xqc_ref, xkc_ref, xvc_ref, sc_ref, zc_ref, xql_ref, xkl_ref, xvl_ref, sl_ref, zl_ref,
                cwq_ref, cwk_ref, cwv_ref, alog_ref, dtb_ref, nw_ref, oc_ref, ol_ref,
                pad_s, q_s, k_s, v_s, gf_s, gb_s, bef_s, beb_s, u_s, w_s, qd_s, kdt_s, qk_s, gl_s, msk_s, o_s, st_s):
    c = GDN_CHUNK
    nc_c, nc_l = lc // c, ll // c
    step = pl.program_id(0)
    h = jnp.minimum(step, n_seq - 1) % A_HEADS
    cur = step % 2
    prev = 1 - cur

    @pl.when(step == 0)
    def _():
        for buf in (u_s, w_s, qd_s, kdt_s, qk_s, gl_s):
            buf[1] = jnp.zeros(buf.shape[1:], buf.dtype)
        ti = _iota2((c, c), 0)
        tj = _iota2((c, c), 1)
        as_f32 = lambda m: jnp.where(m, 1.0, 0.0).astype(F32)
        msk_s[M_INCL] = as_f32(tj <= ti)
        msk_s[M_INCL + 1] = as_f32(tj >= ti)
        msk_s[M_STRICT] = as_f32(tj < ti)
        msk_s[M_STRICT + 1] = as_f32(tj > ti)
        msk_s[M_BASE] = -as_f32(ti // GDN_BASE == tj // GDN_BASE)
        for lv in range(GDN_LEVELS):
            s = GDN_BASE << lv
            msk_s[M_LEVEL + lv] = as_f32((ti // (2 * s) == tj // (2 * s)) & (ti // s != tj // s))
        msk_s[M_EYE] = as_f32(ti == tj)

    def conv_group(parts, seg_base, l, g0, n_tok):
        t = CONV_TILE
        for p, (x_ref, w_ref, dst, mode) in enumerate(parts):
            lo = _aligned(jnp.maximum(g0 - CONV_HALO, 0), CONV_HALO)
            hi = _aligned(jnp.minimum(g0 + n_tok, l - CONV_HALO), CONV_HALO)
            left = x_ref[0, pl.ds(lo, CONV_HALO), :].astype(F32)
            right = x_ref[0, pl.ds(hi, CONV_HALO), :].astype(F32)
            pad_s[p, 0:CONV_HALO, :] = jnp.where(g0 > 0, left, 0.0)
            pad_s[p, CONV_HALO:CONV_HALO + n_tok, :] = x_ref[0, pl.ds(_aligned(g0, n_tok), n_tok), :].astype(F32)
            pad_s[p, CONV_HALO + n_tok:2 * CONV_HALO + n_tok, :] = jnp.where(g0 + n_tok < l, right, 0.0)
            w = w_ref[...]
            for t0 in range(0, n_tok, t):
                acc = jnp.zeros((t, LANES), F32)
                for j in range(CONV_K):
                    r0 = CONV_HALO - CONV_K // 2 + j + t0
                    acc = acc + pad_s[p, r0:r0 + t, :] * w[j:j + 1]
                y = _silu(acc)
                if mode != "v":
                    y = y * lax.rsqrt(jnp.sum(y * y, -1, keepdims=True) + 1e-6)
                if mode == "q":
                    y = y * (A_DK ** -0.5)
                dst[pl.ds(_aligned(seg_base + g0 + t0, t), t), :] = y.astype(dst.dtype)

    parts_c = ((xqc_ref, cwq_ref, q_s, "q"), (xkc_ref, cwk_ref, k_s, "k"), (xvc_ref, cwv_ref, v_s, "v"))
    parts_l = ((xql_ref, cwq_ref, q_s, "q"), (xkl_ref, cwk_ref, k_s, "k"), (xvl_ref, cwv_ref, v_s, "v"))

    n_gate = 4 * A_HEADS
    neg_a = -jnp.exp(alog_ref[...])
    chan = _iota2((n_gate, c), 0)

    def col_bcast(rows, idx):
        row = jnp.sum(jnp.where(chan == idx, rows, 0.0), axis=0, keepdims=True)
        return jnp.broadcast_to(row, (c, c)).T

    def gates_group(s_ref, seg_base, g0, n_tok):
        tri_lb = msk_s[M_INCL].astype(BF16)
        tri_ub = msk_s[M_INCL + 1].astype(BF16)
        for t0 in range(0, n_tok, c):
            xt = s_ref[0, pl.ds(_aligned(g0 + t0, c), c), :].T[:n_gate]
            g = jnp.where(chan < 2 * A_HEADS, neg_a * _softplus(xt + dtb_ref[...]), _sigmoid(xt))
            cum_f = _mm_01(tri_ub, g, right=True)
            cum_b = _mm_01(tri_lb, g, right=True)
            sl = pl.ds(_aligned(seg_base + g0 + t0, c), c)
            gf_s[sl, :] = col_bcast(cum_f, h)
            gb_s[sl, :] = col_bcast(cum_b, h + A_HEADS)
            bef_s[sl, :] = col_bcast(g, h + 2 * A_HEADS).astype(BF16)
            beb_s[sl, :] = col_bcast(g, h + 3 * A_HEADS).astype(BF16)

    def front_end(seg, g0, n_tok):
        parts, s_ref, base, l = ((parts_c, sc_ref, 0, lc), (parts_l, sl_ref, lc, ll))[seg]
        conv_group(parts, base, l, g0, n_tok)
        gates_group(s_ref, base, g0, n_tok)

    def prepass(first_chunk, per_step, it):
        chains = []
        for j in range(per_step):
            ci = first_chunk + it * per_step + j
            sl = pl.ds(_aligned(ci * c, c), c)
            kbf = k_s[sl, :]
            q = q_s[sl, :].astype(F32)
            k = kbf.astype(F32)
            gram = _nt(kbf, kbf)
            qk_raw = _nt(q_s[sl, :], kbf)
            for d, (g_s, be_s) in enumerate(((gf_s, bef_s), (gb_s, beb_s))):
                chains.append(dict(ci=ci, sl=sl, d=d, q=q, k=k, v=v_s[sl, :].astype(F32), gram=gram, qk_raw=qk_raw,
                                   gc=g_s[sl, :], beta=be_s[sl, :].astype(F32)))
        yield
        for ch in chains:
            gc = ch["gc"]
            ch["decay"] = jnp.exp(jnp.minimum(gc - gc.T, 0.0)) * msk_s[M_INCL + ch["d"]]
            ch["a"] = ch["gram"] * ch["beta"] * ch["decay"] * msk_s[M_STRICT + ch["d"]]
            n0 = ch["a"] * msk_s[M_BASE]
            ch["inv"] = msk_s[M_EYE] + n0
            ch["pw"] = n0.astype(BF16)
        yield
        n_sq = GDN_BASE.bit_length() - 2
        for ch in chains:
            ch["pw"] = _mm(ch["pw"], ch["pw"]).astype(BF16)
        yield
        for m in range(n_sq):
            for ch in chains:
                ch["inv_next"] = ch["inv"] + _mm(ch["inv"].astype(BF16), ch["pw"])
            if m + 1 < n_sq:
                for ch in chains:
                    ch["pw"] = _mm(ch["pw"], ch["pw"]).astype(BF16)
            for ch in chains:
                ch["inv"] = ch["inv_next"]
            yield
        for lv in range(GDN_LEVELS):
            for ch in chains:
                ch["invb"] = ch["inv"].astype(BF16)
                ch["t"] = _mm((ch["a"] * msk_s[M_LEVEL + lv]).astype(BF16), ch["invb"]).astype(BF16)
            yield
            for ch in chains:
                ch["inv"] = ch["inv"] - _mm(ch["invb"], ch["t"])
            yield
        for ch in chains:
            gc, beta, q, k, d, sl = ch["gc"], ch["beta"], ch["q"], ch["k"], ch["d"], ch["sl"]
            eg = jnp.exp(gc)
            rhs = jnp.concatenate([ch["v"] * beta, k * beta * eg], axis=-1).astype(BF16)
            sol = _mm(ch["inv"].astype(BF16), rhs)
            g_last = gc[c - 1:c] if d == 0 else gc[0:1]
            u_s[cur, d, sl, :] = sol[:, :A_DV].astype(BF16)
            w_s[cur, d, sl, :] = sol[:, A_DV:].astype(BF16)
            qk_s[cur, d, sl, :] = (ch["qk_raw"] * ch["decay"]).astype(BF16)
            qd_s[cur, d, sl, :] = (q * eg).astype(BF16)
            kdt_s[cur, d, sl, :] = (k * jnp.exp(g_last - gc)).T.astype(BF16)
            gl_s[cur, d, pl.ds(_aligned(ch["ci"] * 8, 8), 8), :] = jnp.exp(
                jnp.broadcast_to(g_last, (8, LANES)))

    def recurrence(first_step, n_steps, it):
        st = [st_s[0], st_s[1]]
        for j in range(n_steps):
            n = first_step + it * n_steps + j
            cis = (n, _bwd_chunk(n, nc_c, nc_l))
            sls = [pl.ds(_aligned(ci * c, c), c) for ci in cis]
            stb = [x.astype(BF16) for x in st]
            w_st = [_mm(w_s[prev, d, sls[d], :], stb[d]) for d in range(2)]
            o_st = [_mm(qd_s[prev, d, sls[d], :], stb[d]) for d in range(2)]
            yield
            vnb = [(u_s[prev, d, sls[d], :].astype(F32) - w_st[d]).astype(BF16) for d in range(2)]
            for d in range(2):
                gl = gl_s[prev, d, pl.ds(_aligned(cis[d] * 8, 8), 8), :][0:1]
                st[d] = st[d] * gl + _mm(kdt_s[prev, d, sls[d], :], vnb[d])
            for d in range(2):
                o_s[sls[d], :] += o_st[d] + _mm(qk_s[prev, d, sls[d], :], vnb[d])
            yield
        st_s[0] = st[0]
        st_s[1] = st[1]

    o_s[...] = jnp.zeros_like(o_s)
    st_s[...] = jnp.zeros_like(st_s)

    def merged(first, per_step, next_group, it, carry):
        pre = prepass(first, per_step, it)
        rec = recurrence(first, per_step, it)
        for i, _ in enumerate(pre):
            if i == 0 and next_group is not None:
                front_end(*next_group(it))
            if i >= GDN_REC_FIRST_STAGE:
                next(rec, None)
        for _ in rec:
            pass
        return carry

    per_c = math.gcd(nc_c, GDN_PRE_CHUNKS)
    per_l = math.gcd(nc_l, GDN_PRE_CHUNKS)
    it_c, it_l = nc_c // per_c, nc_l // per_l
    front_end(0, 0, per_c * c)
    if it_c > 1:
        lax.fori_loop(0, it_c - 1, functools.partial(
            merged, 0, per_c, lambda it: (0, (it + 1) * (per_c * c), per_c * c)), 0)
    merged(0, per_c, lambda it: (1, 0, per_l * c), it_c - 1, 0)
    if it_l > 1:
        lax.fori_loop(0, it_l - 1, functools.partial(
            merged, nc_c, per_l, lambda it: (1, (it + 1) * (per_l * c), per_l * c)), 0)
    merged(nc_c, per_l, None, it_l - 1, 0)

    def epilogue(z_ref, o_ref, base, n_tiles):
        def body(t, carry):
            off = pl.multiple_of(t * GDN_CHUNK, GDN_CHUNK)
            o = o_s[pl.ds(pl.multiple_of(base + t * GDN_CHUNK, GDN_CHUNK), GDN_CHUNK), :]
            o = o * lax.rsqrt(jnp.mean(o * o, -1, keepdims=True) + 1e-6) * nw_ref[...]
            z = z_ref[0, pl.ds(off, GDN_CHUNK), :].astype(F32)
            o_ref[0, pl.ds(off, GDN_CHUNK), :] = (o * _silu(z)).astype(BF16)
            return carry
        lax.fori_loop(0, n_tiles, body, 0, unroll=2)

    epilogue(zc_ref, oc_ref, 0, lc // GDN_CHUNK)
    epilogue(zl_ref, ol_ref, lc, ll // GDN_CHUNK)


def _gdn(main_c, small_c, main_l, small_l, conv_w, a_log, dt_bias, norm_w):
    b_, lc, _ = main_c.shape
    ll = main_l.shape[1]
    lt = lc + ll
    assert lc % 256 == 0 and ll % 256 == 0
    n_gate = 4 * A_HEADS
    pad8 = lambda p: jnp.broadcast_to(
        jnp.zeros((n_gate,), F32).at[:2 * A_HEADS].set(p.astype(F32).reshape(-1))[:, None], (n_gate, GDN_CHUNK))
    cw = conv_w.astype(F32)

    n_seq = b_ * A_HEADS
    pre_seq = lambda s: jnp.minimum(s, n_seq - 1)
    rec_seq = lambda s: jnp.maximum(s - 1, 0)

    def seg(l, w, part, seq):
        return pl.BlockSpec((1, l, w), lambda s: (seq(s) // A_HEADS, 0, part * A_HEADS + seq(s) % A_HEADS))

    def cw_spec(part):
        return pl.BlockSpec((CONV_K, A_DK), lambda s: (0, part * A_HEADS + pre_seq(s) % A_HEADS))

    const = lambda shape: pl.BlockSpec(shape, lambda s: (0,) * len(shape))
    in_specs = []
    for l in (lc, ll):
        in_specs += [seg(l, A_DK, 0, pre_seq), seg(l, A_DK, 1, pre_seq), seg(l, A_DK, 2, pre_seq),
                     pl.BlockSpec((1, l, SMALL_W), lambda s: (pre_seq(s) // A_HEADS, 0, 0)),
                     seg(l, A_DV, 3, rec_seq)]
    in_specs += [cw_spec(0), cw_spec(1), cw_spec(2), const((n_gate, GDN_CHUNK)), const((n_gate, GDN_CHUNK)),
                 const((1, A_DV))]
    tok = lambda dt: pltpu.VMEM((lt, LANES), dt)
    halves = lambda: pltpu.VMEM((2, 2, lt, LANES), BF16)
    return pl.pallas_call(
        functools.partial(_gdn_kernel, lc, ll, n_seq),
        grid=(n_seq + 1,),
        in_specs=in_specs,
        out_specs=[seg(lc, A_DV, 0, rec_seq), seg(ll, A_DV, 0, rec_seq)],
        out_shape=[jax.ShapeDtypeStruct((b_, lc, A_W), BF16), jax.ShapeDtypeStruct((b_, ll, A_W), BF16)],
        scratch_shapes=[pltpu.VMEM((3, min(GDN_PRE_CHUNKS * GDN_CHUNK, max(lc, ll)) + 2 * CONV_HALO, LANES), F32),
                        tok(BF16), tok(BF16), tok(BF16), tok(F32), tok(F32), tok(BF16), tok(BF16),
                        halves(), halves(), halves(), halves(), halves(),
                        pltpu.VMEM((2, 2, (lt // GDN_CHUNK) * 8, LANES), F32),
                        pltpu.VMEM((N_MASKS, GDN_CHUNK, GDN_CHUNK), F32),
                        tok(F32), pltpu.VMEM((2, A_DK, A_DV), F32)],
        compiler_params=pltpu.CompilerParams(vmem_limit_bytes=GDN_VMEM_LIMIT,
                                             dimension_semantics=("arbitrary",)),
        name="gdn",
    )(main_c, main_c, main_c, small_c, main_c, main_l, main_l, main_l, small_l, main_l,
      cw, cw, cw, pad8(a_log), pad8(dt_bias), norm_w.astype(F32).reshape(1, A_DV))


def _permute_w_in(w):
    sizes = (3 * A_W, 4 * A_HEADS, A_W, B_W, B_W, B_W, B_W, C_QW, C_QW, C_VW, 2 * C_RANK, C_VW)
    offs = np.concatenate([[0], np.cumsum(sizes)])
    part = lambda i: w[..., offs[i]:offs[i + 1]]
    pad = jnp.zeros(w.shape[:-1] + (SMALL_W - 4 * A_HEADS - 2 * C_RANK,), w.dtype)
    order = [part(0), part(2), part(3), part(4), part(5), part(6), part(7), part(8), part(9), part(11),
             part(1), part(10), pad]
    return jnp.concatenate(order, axis=-1).astype(BF16)


def kernel(x, c, ctx, c_ctx, w_mod, b_mod, w_in, conv_w, a_log, dt_bias, gdn_norm, rpb,
           gla_w2, gla_b2, gla_norm, w_out, ln_g, ln_b):
    depth = w_mod.shape[0]
    b_ = x.shape[0]
    alpha = (2 * depth) ** 0.25
    n_mod = -(-(b_ + 1) // 8) * 8
    c_all = jnp.zeros((n_mod, D_MODEL), F32).at[:b_].set(c).at[b_].set(c_ctx)
    w_perm = _permute_w_in(w_in)
    w_o = w_out.astype(BF16)
    na_bias = _na_bias_table(rpb)
    b_mod3 = b_mod.reshape(depth, 1, -1)
    xl, xc = x, ctx
    for i in range(depth):
        ctx_out = i < depth - 1
        mod = _modulation(c_all, w_mod, b_mod3, i).reshape(n_mod, 3, 1, D_MODEL)
        sh_l, sc_l, gt_l = mod[:b_, 0], mod[:b_, 1], mod[:b_, 2]
        sh_c, sc_c, gt_c = mod[b_:b_ + 1, 0], mod[b_:b_ + 1, 1], mod[b_:b_ + 1, 2]
        main_l, small_l = _project(xl, sh_l, sc_l, w_perm, i, True)
        flat = lambda t: t.reshape(1, -1, t.shape[-1])
        main_c, small_c = (t.reshape(b_, -1, t.shape[-1]) for t in _project(flat(xc), sh_c, sc_c, w_perm, i, False))
        ya_c, ya_l = _gdn(main_c, small_c, main_l, small_l, conv_w[i], a_log[i], dt_bias[i], gdn_norm[i])
        yb_l = _na_latent(main_l, main_c, na_bias, i)
        yc_c, yc_l = _gla(main_c, small_c, main_l, small_l, gla_w2[i], gla_b2[i], gla_norm[i])
        if ctx_out:
            yb_c = _na_ctx(main_c)
            xc = _out_project(flat(xc), flat(ya_c), flat(yb_c), flat(yc_c), w_o, i, gt_c, ln_g[i], ln_b[i],
                              False, alpha).reshape(xc.shape)
        xl = _out_project(xl, ya_l, yb_l, yc_l, w_o, i, gt_l, ln_g[i], ln_b[i], True, alpha)
    return xl
```
